```python
import jax, jax.numpy as jnp
from jax import lax
import numpy as np

D_MODEL = 1024
BATCH = 8
SEQ = 2048
DEPTH = 4
DEC_BATCH = 32
DEC_SEQ = 8
PAST_LEN = 16384
PAGE_SIZE = 128

HEAD_DIM = 64
N_HEADS = D_MODEL // HEAD_DIM
D_FF = 4 * D_MODEL
ROPE_THETA = 10000.0
NORM_EPS = 1e-6
MIXER_PATTERN = 'ABC'
N_MIXERS = len(MIXER_PATTERN)
MOBA_KV_HEADS = 4
MOBA_BLOCK = 256
MOBA_TOPK = 3
NSA_KV_HEADS = 2
NSA_CMP_LEN = 32
NSA_CMP_STRIDE = 16
NSA_CMP_HIDDEN = 4 * HEAD_DIM
NSA_SLC_BLOCK = 64
NSA_SLC_TOPN = 16
NSA_WINDOW = 512
NSA_FORCE_SCORE = 1e9
MLA_Q_LORA = 384
MLA_KV_LORA = 256
MLA_NOPE = 64
MLA_ROPE = 32
MLA_V = 64
DENSE_Q_BLOCK = 128
SPARSE_Q_BLOCK = 64

kernel_name = 'hybrid_moba_nsa_mla_decode_step'

F32 = jnp.float32


def rmsnorm(x, g):
    xf = x.astype(F32)
    y = xf * lax.rsqrt(jnp.mean(xf * xf, axis=-1, keepdims=True) + NORM_EPS)
    return (y * g.astype(F32)).astype(x.dtype)


def rope(x, pos):
    half = x.shape[-1] // 2
    inv = ROPE_THETA ** (-jnp.arange(half, dtype=F32) / half)
    ang = pos.astype(F32)[:, None] * inv[None, :]
    ang = ang.reshape((ang.shape[0],) + (1,) * (x.ndim - 3) + (half,))
    cos, sin = jnp.cos(ang), jnp.sin(ang)
    xf = x.astype(F32)
    x1, x2 = xf[..., :half], xf[..., half:]
    return jnp.concatenate([x1 * cos - x2 * sin, x1 * sin + x2 * cos], axis=-1).astype(x.dtype)


def _masked_softmax(s, mask):
    s = jnp.where(mask, s.astype(F32), -jnp.inf)
    m = jnp.max(s, axis=-1, keepdims=True)
    m = jnp.where(jnp.isfinite(m), m, 0.0)
    e = jnp.where(mask, jnp.exp(s - m), 0.0)
    return e / jnp.maximum(jnp.sum(e, axis=-1, keepdims=True), jnp.finfo(F32).tiny)


def _seq(a, b):
    return lax.dynamic_index_in_dim(a, b, 0, keepdims=False)


def _gather_pages(cache, page_table, b):
    pages = _seq(page_table, b)
    rows = cache[pages]
    return rows.reshape((-1,) + cache.shape[2:])


def _map_query_blocks(fn, n_seq, n_q, qb):
    nc = n_q // qb
    b_idx = jnp.repeat(jnp.arange(n_seq), nc)
    c_idx = jnp.tile(jnp.arange(nc), n_seq)
    out = lax.map(lambda bc: fn(bc[0], bc[1]), (b_idx, c_idx))
    return out.reshape((n_seq, n_q) + out.shape[2:])


def _blocks(rows, blk):
    L = rows.shape[-3]
    pad = (-L) % blk
    rows = jnp.pad(rows, [(0, 0)] * (rows.ndim - 3) + [(0, pad), (0, 0), (0, 0)])
    return rows.reshape(rows.shape[:-3] + ((L + pad) // blk, blk) + rows.shape[-2:])


def sq_relu_mlp(h, w1, w2):
    return jnp.square(jax.nn.relu(h @ w1)) @ w2


def moba_project(h, pos, w_in, qn_g, kn_g):
    B, T, _ = h.shape
    nq = N_HEADS * HEAD_DIM
    a = h @ w_in
    q = rope(rmsnorm(a[..., :nq].reshape(B, T, N_HEADS, HEAD_DIM), qn_g), pos)
    kv = a[..., nq:].reshape(B, T, 2, MOBA_KV_HEADS, HEAD_DIM)
    k = rope(rmsnorm(kv[:, :, 0], kn_g), pos)
    return q, jnp.stack([k, kv[:, :, 1]], axis=2)


def moba_blocks(k, v):
    kb = _blocks(k, MOBA_BLOCK)
    vb = _blocks(v, MOBA_BLOCK)
    bmean = jnp.mean(kb.astype(F32), axis=-3).astype(kb.dtype)
    return kb, vb, bmean


def moba_core(q, kb, vb, bmean, qpos):
    nq, nb = q.shape[0], kb.shape[0]
    grp = N_HEADS // MOBA_KV_HEADS
    qg = q.reshape(nq, MOBA_KV_HEADS, grp, HEAD_DIM)
    own = qpos // MOBA_BLOCK
    gate = jnp.einsum('qgrd,ngd->qgrn', qg, bmean).astype(F32)
    past = jnp.arange(nb)[None, :] < own[:, None]
    gate = jnp.where(past[:, None, None, :], gate, -jnp.inf)
    _, top = lax.top_k(gate, min(MOBA_TOPK, nb))
    own_b = jnp.broadcast_to(own[:, None, None, None], top.shape[:-1] + (1,))
    idx = jnp.concatenate([top, own_b], axis=-1)
    slot_ok = jnp.concatenate([top < own[:, None, None, None], jnp.ones_like(own_b, dtype=bool)], axis=-1)
    g_ix = jnp.arange(MOBA_KV_HEADS)[None, :, None, None]
    kg = kb.transpose(2, 0, 1, 3)[g_ix, idx]
    vg = vb.transpose(2, 0, 1, 3)[g_ix, idx]
    s = jnp.einsum('qgrd,qgrsjd->qgrsj', qg, kg).astype(F32) * HEAD_DIM ** -0.5
    kpos = idx[..., None] * MOBA_BLOCK + jnp.arange(MOBA_BLOCK)
    mask = slot_ok[..., None] & (kpos <= qpos[:, None, None, None, None])
    shp = s.shape[:3] + (-1,)
    p = _masked_softmax(s.reshape(shp), mask.reshape(shp)).reshape(s.shape)
    o = jnp.einsum('qgrsj,qgrsjd->qgrd', p.astype(vg.dtype), vg)
    return o.reshape(nq, N_HEADS, HEAD_DIM)


def moba_prompt(h, w_in, qn_g, kn_g, w_o):
    B, T, _ = h.shape
    q, kv = moba_project(h, jnp.arange(T), w_in, qn_g, kn_g)
    kb, vb, bmean = moba_blocks(kv[:, :, 0], kv[:, :, 1])

    def chunk(b, c):
        s0 = c * SPARSE_Q_BLOCK
        qc = lax.dynamic_slice_in_dim(_seq(q, b), s0, SPARSE_Q_BLOCK, 0)
        return moba_core(qc, _seq(kb, b), _seq(vb, b), _seq(bmean, b), s0 + jnp.arange(SPARSE_Q_BLOCK))

    o = _map_query_blocks(chunk, B, T, SPARSE_Q_BLOCK)
    return o.reshape(B, T, -1) @ w_o, (kv,)


def moba_sample(h, cache_kv, page_table, w_in, qn_g, kn_g, w_o):
    DB, DS, _ = h.shape
    pos = PAST_LEN + jnp.arange(DS)
    q, kv = moba_project(h, pos, w_in, qn_g, kn_g)

    def one(b):
        rows = jnp.concatenate([_gather_pages(cache_kv, page_table, b), _seq(kv, b)], axis=0)
        kb, vb, bmean = moba_blocks(rows[:, 0], rows[:, 1])
        return moba_core(_seq(q, b), kb, vb, bmean, pos)

    o = lax.map(one, jnp.arange(DB))
    return o.reshape(DB, DS, -1) @ w_o, (kv,)


def nsa_project(h, pos, w_in, qn_g, kn_g):
    B, T, _ = h.shape
    nq = N_HEADS * HEAD_DIM
    nkv = 6 * NSA_KV_HEADS * HEAD_DIM
    a = h @ w_in
    q = rope(rmsnorm(a[..., :nq].reshape(B, T, N_HEADS, HEAD_DIM), qn_g), pos)
    kv = a[..., nq:nq + nkv].reshape(B, T, 3, 2, NSA_KV_HEADS, HEAD_DIM)
    k = rope(rmsnorm(kv[:, :, :, 0], kn_g[:, None, :]), pos)
    kv = jnp.stack([k, kv[:, :, :, 1]], axis=3)
    gate = jax.nn.sigmoid(a[..., nq + nkv:].astype(F32)).reshape(B, T, N_HEADS, 3).astype(h.dtype)
    return q, kv, gate


def nsa_compress(rows, pe, w1, b1, w2):
    L = rows.shape[-3]
    nc = (L - NSA_CMP_LEN) // NSA_CMP_STRIDE + 1
    idx = jnp.arange(nc)[:, None] * NSA_CMP_STRIDE + jnp.arange(NSA_CMP_LEN)[None, :]
    blk = jnp.take(rows, idx, axis=rows.ndim - 3) + pe[:, None, :]
    blk = jnp.swapaxes(blk, -3, -2)
    flat = blk.reshape(blk.shape[:-2] + (NSA_CMP_LEN * HEAD_DIM,))
    return jax.nn.gelu(flat @ w1 + b1) @ w2


def nsa_core(q, qpos, gate, kc, vc, ksb, vsb, kw, vw, wpos):
    nq, nc, nsb = q.shape[0], kc.shape[0], ksb.shape[0]
    grp = N_HEADS // NSA_KV_HEADS
    scale = HEAD_DIM ** -0.5
    qg = q.reshape(nq, NSA_KV_HEADS, grp, HEAD_DIM)
    c_start = jnp.arange(nc) * NSA_CMP_STRIDE
    c_ok = (c_start[None, :] + NSA_CMP_LEN - 1) <= qpos[:, None]
    s_c = jnp.einsum('qgrd,ngd->qgrn', qg, kc).astype(F32) * scale
    p_c = _masked_softmax(s_c, c_ok[:, None, None, :])
    o_c = jnp.einsum('qgrn,ngd->qgrd', p_c.astype(vc.dtype), vc)
    b_start = jnp.arange(nsb) * NSA_SLC_BLOCK
    overlap = ((c_start[:, None] < b_start[None, :] + NSA_SLC_BLOCK)
               & (c_start[:, None] + NSA_CMP_LEN > b_start[None, :])).astype(F32)
    imp = jnp.einsum('qgrn,nj->qgj', p_c, overlap)
    own = qpos // NSA_SLC_BLOCK
    jb = jnp.arange(nsb)[None, :]
    allowed = jb <= own[:, None]
    forced = (jb == 0) | (jb == own[:, None]) | (jb == own[:, None] - 1)
    imp = jnp.where(forced[:, None, :], NSA_FORCE_SCORE, imp)
    imp = jnp.where(allowed[:, None, :], imp, -jnp.inf)
    _, sel = lax.top_k(imp, min(NSA_SLC_TOPN, nsb))
    g_ix = jnp.arange(NSA_KV_HEADS)[None, :, None]
    ks = ksb.transpose(2, 0, 1, 3)[g_ix, sel]
    vs = vsb.transpose(2, 0, 1, 3)[g_ix, sel]
    s_s = jnp.einsum('qgrd,qgkjd->qgrkj', qg, ks).astype(F32) * scale
    kpos = sel[..., None] * NSA_SLC_BLOCK + jnp.arange(NSA_SLC_BLOCK)
    m_s = (sel <= own[:, None, None])[..., None] & (kpos <= qpos[:, None, None, None])
    m_s = jnp.broadcast_to(m_s[:, :, None], s_s.shape)
    shp = s_s.shape[:3] + (-1,)
    p_s = _masked_softmax(s_s.reshape(shp), m_s.reshape(shp)).reshape(s_s.shape)
    o_s = jnp.einsum('qgrkj,qgkjd->qgrd', p_s.astype(vs.dtype), vs)
    s_w = jnp.einsum('qgrd,wgd->qgrw', qg, kw).astype(F32) * scale
    dist = qpos[:, None] - wpos[None, :]
    m_w = (dist >= 0) & (dist < NSA_WINDOW) & (wpos[None, :] >= 0)
    p_w = _masked_softmax(s_w, m_w[:, None, None, :])
    o_w = jnp.einsum('qgrw,wgd->qgrd', p_w.astype(vw.dtype), vw)
    g = gate.reshape(nq, NSA_KV_HEADS, grp, 3)
    o = g[..., 0:1] * o_c + g[..., 1:2] * o_s + g[..., 2:3] * o_w
    return o.reshape(nq, N_HEADS, HEAD_DIM)


def nsa_prompt(h, w_in, qn_g, kn_g, cmp_pe, cmp_w1, cmp_b1, cmp_w2, w_o):
    B, T, _ = h.shape
    q, kv, gate = nsa_project(h, jnp.arange(T), w_in, qn_g, kn_g)
    kc = nsa_compress(kv[:, :, 0, 0], cmp_pe[0], cmp_w1[0], cmp_b1[0], cmp_w2[0])
    vc = nsa_compress(kv[:, :, 0, 1], cmp_pe[1], cmp_w1[1], cmp_b1[1], cmp_w2[1])
    ksb = _blocks(kv[:, :, 1, 0], NSA_SLC_BLOCK)
    vsb = _blocks(kv[:, :, 1, 1], NSA_SLC_BLOCK)
    win = jnp.pad(kv[:, :, 2], ((0, 0), (NSA_WINDOW - 1, 0), (0, 0), (0, 0), (0, 0)))
    n_w = NSA_WINDOW + SPARSE_Q_BLOCK - 1

    def chunk(b, c):
        s0 = c * SPARSE_Q_BLOCK
        qpos = s0 + jnp.arange(SPARSE_Q_BLOCK)
        qc = lax.dynamic_slice_in_dim(_seq(q, b), s0, SPARSE_Q_BLOCK, 0)
        gc = lax.dynamic_slice_in_dim(_seq(gate, b), s0, SPARSE_Q_BLOCK, 0)
        w = lax.dynamic_slice_in_dim(_seq(win, b), s0, n_w, 0)
        wpos = s0 - NSA_WINDOW + 1 + jnp.arange(n_w)
        return nsa_core(qc, qpos, gc, _seq(kc, b), _seq(vc, b), _seq(ksb, b), _seq(vsb, b),
                        w[:, 0], w[:, 1], wpos)

    o = _map_query_blocks(chunk, B, T, SPARSE_Q_BLOCK)
    wb = min(NSA_WINDOW, T)
    rows = kv[:, :, :2].reshape(B, T, 4, NSA_KV_HEADS, HEAD_DIM)
    return o.reshape(B, T, -1) @ w_o, (rows, kv[:, T - wb:, 2])


def nsa_sample(h, cache_kv, state_win, page_table, w_in, qn_g, kn_g, cmp_pe, cmp_w1, cmp_b1, cmp_w2, w_o):
    DB, DS, _ = h.shape
    pos = PAST_LEN + jnp.arange(DS)
    q, kv, gate = nsa_project(h, pos, w_in, qn_g, kn_g)
    rows_new = kv[:, :, :2].reshape(DB, DS, 4, NSA_KV_HEADS, HEAD_DIM)
    wb = state_win.shape[1]
    win_all = jnp.concatenate([state_win, kv[:, :, 2]], axis=1)
    wpos = PAST_LEN - wb + jnp.arange(wb + DS)

    def one(b):
        rows = jnp.concatenate([_gather_pages(cache_kv, page_table, b), _seq(rows_new, b)], axis=0)
        kc = nsa_compress(rows[:, 0], cmp_pe[0], cmp_w1[0], cmp_b1[0], cmp_w2[0])
        vc = nsa_compress(rows[:, 1], cmp_pe[1], cmp_w1[1], cmp_b1[1], cmp_w2[1])
        ksb = _blocks(rows[:, 2], NSA_SLC_BLOCK)
        vsb = _blocks(rows[:, 3], NSA_SLC_BLOCK)
        w = _seq(win_all, b)
        return nsa_core(_seq(q, b), pos, _seq(gate, b), kc, vc, ksb, vsb, w[:, 0], w[:, 1], wpos)

    o = lax.map(one, jnp.arange(DB))
    return o.reshape(DB, DS, -1) @ w_o, (rows_new, win_all[:, DS:])


def mla_project(h, pos, w_dqkv, g_q, g_kv, w_uq, qn_g, kn_g):
    B, T, _ = h.shape
    a = h @ w_dqkv
    cq = rmsnorm(a[..., :MLA_Q_LORA], g_q)
    ckv = rmsnorm(a[..., MLA_Q_LORA:MLA_Q_LORA + MLA_KV_LORA], g_kv)
    kpe = rope(rmsnorm(a[..., MLA_Q_LORA + MLA_KV_LORA:], kn_g[MLA_NOPE:]), pos)
    q = (cq @ w_uq).reshape(B, T, N_HEADS, MLA_NOPE + MLA_ROPE)
    q = jnp.concatenate([rmsnorm(q[..., :MLA_NOPE], qn_g[:MLA_NOPE]),
                         rope(rmsnorm(q[..., MLA_NOPE:], qn_g[MLA_NOPE:]), pos)], axis=-1)
    return q, jnp.concatenate([ckv, kpe], axis=-1)


def mla_expand(lat, w_ukv, kn_g):
    ckv, kpe = lat[..., :MLA_KV_LORA], lat[..., MLA_KV_LORA:]
    kv = (ckv @ w_ukv).reshape(lat.shape[:-1] + (N_HEADS, MLA_NOPE + MLA_V))
    k_nope = rmsnorm(kv[..., :MLA_NOPE], kn_g[:MLA_NOPE])
    k_pe = jnp.broadcast_to(kpe[..., None, :], k_nope.shape[:-1] + (MLA_ROPE,))
    return jnp.concatenate([k_nope, k_pe], axis=-1), kv[..., MLA_NOPE:]


def dense_causal_attend(q, k, v, qpos):
    s = jnp.einsum('qhd,lhd->hql', q, k).astype(F32) * q.shape[-1] ** -0.5
    mask = (jnp.arange(k.shape[0])[None, :] <= qpos[:, None])[None]
    p = _masked_softmax(s, mask)
    return jnp.einsum('hql,lhd->qhd', p.astype(v.dtype), v)


def mla_prompt(h, w_dqkv, g_q, g_kv, w_uq, w_ukv, qn_g, kn_g, w_o):
    B, T, _ = h.shape
    q, lat = mla_project(h, jnp.arange(T), w_dqkv, g_q, g_kv, w_uq, qn_g, kn_g)
    k, v = mla_expand(lat, w_ukv, kn_g)

    def chunk(b, c):
        s0 = c * DENSE_Q_BLOCK
        qc = lax.dynamic_slice_in_dim(_seq(q, b), s0, DENSE_Q_BLOCK, 0)
        return dense_causal_attend(qc, _seq(k, b), _seq(v, b), s0 + jnp.arange(DENSE_Q_BLOCK))

    o = _map_query_blocks(chunk, B, T, DENSE_Q_BLOCK)
    return o.reshape(B, T, -1) @ w_o, (lat,)


def mla_sample(h, cache_lat, page_table, w_dqkv, g_q, g_kv, w_uq, w_ukv, qn_g, kn_g, w_o):
    DB, DS, _ = h.shape
    pos = PAST_LEN + jnp.arange(DS)
    q, lat = mla_project(h, pos, w_dqkv, g_q, g_kv, w_uq, qn_g, kn_g)

    def one(b):
        rows = jnp.concatenate([_gather_pages(cache_lat, page_table, b), _seq(lat, b)], axis=0)
        k, v = mla_expand(rows, w_ukv, kn_g)
        return dense_causal_attend(_seq(q, b), k, v, pos)

    o = lax.map(one, jnp.arange(DB))
    return o.reshape(DB, DS, -1) @ w_o, (lat,)


def setup_inputs(seed: int = 0) -> dict:
    key = jax.random.key(seed)
    keys = iter(jax.random.split(key, 64))

    def nrm(shape, scale=1.0):
        return jax.random.normal(next(keys), shape, F32) * scale

    def gain(shape):
        return 1.0 + 0.05 * jax.random.normal(next(keys), shape, F32)

    n_pages = PAST_LEN // PAGE_SIZE
    n_used = DEC_BATCH * n_pages
    n_pool = n_used + n_used // 4
    win_buf = min(NSA_WINDOW, PAST_LEN)
    moba_in = N_HEADS * HEAD_DIM + 2 * MOBA_KV_HEADS * HEAD_DIM
    nsa_in = N_HEADS * HEAD_DIM + 6 * NSA_KV_HEADS * HEAD_DIM + 3 * N_HEADS
    mla_a = MLA_Q_LORA + MLA_KV_LORA + MLA_ROPE
    d_att = N_HEADS * HEAD_DIM
    inp = {}
    inp['x_prompt'] = nrm((BATCH, SEQ, D_MODEL))
    inp['x_sample'] = nrm((DEC_BATCH, DEC_SEQ, D_MODEL))
    inp['cache_kv_0'] = nrm((n_pool, PAGE_SIZE, 2, MOBA_KV_HEADS, HEAD_DIM))
    inp['cache_kv_1'] = nrm((n_pool, PAGE_SIZE, 4, NSA_KV_HEADS, HEAD_DIM))
    inp['state_win_1'] = nrm((DEC_BATCH, win_buf, 2, NSA_KV_HEADS, HEAD_DIM))
    inp['cache_lat_2'] = nrm((n_pool, PAGE_SIZE, MLA_KV_LORA + MLA_ROPE))
    inp['cache_kv_3'] = nrm((n_pool, PAGE_SIZE, 2, MOBA_KV_HEADS, HEAD_DIM))
    perm = jax.random.permutation(next(keys), n_pool)
    inp['page_table'] = perm[:n_used].reshape(DEC_BATCH, n_pages).astype(jnp.int32)
    inp['ln1_g'] = gain((DEPTH, D_MODEL))
    inp['ln2_g'] = gain((DEPTH, D_MODEL))
    inp['mlp_w1'] = nrm((DEPTH, D_MODEL, D_FF), D_MODEL ** -0.5)
    inp['mlp_w2'] = nrm((DEPTH, D_FF, D_MODEL), D_FF ** -0.5)
    inp['moba_w_in_0'] = nrm((D_MODEL, moba_in), D_MODEL ** -0.5)
    inp['moba_qn_0'] = gain((HEAD_DIM,))
    inp['moba_kn_0'] = gain((HEAD_DIM,))
    inp['moba_w_o_0'] = nrm((d_att, D_MODEL), d_att ** -0.5)
    inp['nsa_w_in_1'] = nrm((D_MODEL, nsa_in), D_MODEL ** -0.5)
    inp['nsa_qn_1'] = gain((HEAD_DIM,))
    inp['nsa_kn_1'] = gain((3, HEAD_DIM))
    inp['nsa_cmp_pe_1'] = nrm((2, NSA_CMP_LEN, HEAD_DIM), 0.1)
    inp['nsa_cmp_w1_1'] = nrm((2, NSA_CMP_LEN * HEAD_DIM, NSA_CMP_HIDDEN), (NSA_CMP_LEN * HEAD_DIM) ** -0.5)
    inp['nsa_cmp_b1_1'] = nrm((2, NSA_CMP_HIDDEN), 0.01)
    inp['nsa_cmp_w2_1'] = nrm((2, NSA_CMP_HIDDEN, HEAD_DIM), NSA_CMP_HIDDEN ** -0.5)
    inp['nsa_w_o_1'] = nrm((d_att, D_MODEL), d_att ** -0.5)
    inp['mla_w_dqkv_2'] = nrm((D_MODEL, mla_a), D_MODEL ** -0.5)
    inp['mla_g_q_2'] = gain((MLA_Q_LORA,))
    inp['mla_g_kv_2'] = gain((MLA_KV_LORA,))
    inp['mla_w_uq_2'] = nrm((MLA_Q_LORA, N_HEADS * (MLA_NOPE + MLA_ROPE)), MLA_Q_LORA ** -0.5)
    inp['mla_w_ukv_2'] = nrm((MLA_KV_LORA, N_HEADS * (MLA_NOPE + MLA_V)), MLA_KV_LORA ** -0.5)
    inp['mla_qn_2'] = gain((MLA_NOPE + MLA_ROPE,))
    inp['mla_kn_2'] = gain((MLA_NOPE + MLA_ROPE,))
    inp['mla_w_o_2'] = nrm((N_HEADS * MLA_V, D_MODEL), (N_HEADS * MLA_V) ** -0.5)
    inp['moba_w_in_3'] = nrm((D_MODEL, moba_in), D_MODEL ** -0.5)
    inp['moba_qn_3'] = gain((HEAD_DIM,))
    inp['moba_kn_3'] = gain((HEAD_DIM,))
    inp['moba_w_o_3'] = nrm((d_att, D_MODEL), d_att ** -0.5)
    return inp


def reference(x_prompt, x_sample, cache_kv_0, cache_kv_1, state_win_1, cache_lat_2, cache_kv_3, page_table,
              ln1_g, ln2_g, mlp_w1, mlp_w2,
              moba_w_in_0, moba_qn_0, moba_kn_0, moba_w_o_0,
              nsa_w_in_1, nsa_qn_1, nsa_kn_1, nsa_cmp_pe_1, nsa_cmp_w1_1, nsa_cmp_b1_1, nsa_cmp_w2_1, nsa_w_o_1,
              mla_w_dqkv_2, mla_g_q_2, mla_g_kv_2, mla_w_uq_2, mla_w_ukv_2, mla_qn_2, mla_kn_2, mla_w_o_2,
              moba_w_in_3, moba_qn_3, moba_kn_3, moba_w_o_3):
    mixer_w = (
        (moba_w_in_0, moba_qn_0, moba_kn_0, moba_w_o_0),
        (nsa_w_in_1, nsa_qn_1, nsa_kn_1, nsa_cmp_pe_1, nsa_cmp_w1_1, nsa_cmp_b1_1, nsa_cmp_w2_1, nsa_w_o_1),
        (mla_w_dqkv_2, mla_g_q_2, mla_g_kv_2, mla_w_uq_2, mla_w_ukv_2, mla_qn_2, mla_kn_2, mla_w_o_2),
        (moba_w_in_3, moba_qn_3, moba_kn_3, moba_w_o_3),
    )
    caches = ((cache_kv_0,), (cache_kv_1, state_win_1), (cache_lat_2,), (cache_kv_3,))
    prompt_fns = {'A': moba_prompt, 'B': nsa_prompt, 'C': mla_prompt}
    sample_fns = {'A': moba_sample, 'B': nsa_sample, 'C': mla_sample}
    xp, xs = x_prompt, x_sample
    p_new, s_new = [], []
    for i in range(DEPTH):
        kind = MIXER_PATTERN[i % N_MIXERS]
        yp, sp = prompt_fns[kind](rmsnorm(xp, ln1_g[i]), *mixer_w[i])
        ys, ss = sample_fns[kind](rmsnorm(xs, ln1_g[i]), *caches[i], page_table, *mixer_w[i])
        xp = xp + yp
        xs = xs + ys
        xp = xp + sq_relu_mlp(rmsnorm(xp, ln2_g[i]), mlp_w1[i], mlp_w2[i])
        xs = xs + sq_relu_mlp(rmsnorm(xs, ln2_g[i]), mlp_w1[i], mlp_w2[i])
        p_new.extend(sp)
        s_new.extend(ss)
    return (xp, xs, p_new[0], p_new[1], p_new[2], p_new[3], p_new[4],
            s_new[0], s_new[1], s_new[2], s_new[3], s_new[4])
```

```python
import functools

import numpy as np
import jax
import jax.numpy as jnp
from jax import lax
from jax.experimental import pallas as pl
from jax.experimental.pallas import tpu as pltpu

F32 = jnp.float32
BF16 = jnp.bfloat16

HEAD_DIM = 64
ROPE_THETA = 10000.0
NORM_EPS = 1e-6
PAGE = 128
MOBA_KV_HEADS = 4
MOBA_BLOCK = 256
MOBA_TOPK = 3
NSA_KV_HEADS = 2
NSA_CMP_LEN = 32
NSA_CMP_STRIDE = 16
NSA_CMP_HIDDEN = 256
NSA_SLC_BLOCK = 64
NSA_SLC_TOPN = 16
NSA_WINDOW = 512
NSA_FORCE_SCORE = 1e9
MLA_Q_LORA = 384
MLA_KV_LORA = 256
MLA_NOPE = 64
MLA_ROPE = 32
MLA_V = 64

LANES = 128
VMEM_LIMIT_BYTES = 56 * 1024 * 1024
NEG = -1e30
TINY = float(np.finfo(np.float32).tiny)


def _cparams(*sem):
    return pltpu.CompilerParams(dimension_semantics=sem, vmem_limit_bytes=VMEM_LIMIT_BYTES)


def _dot(a, b):
    return jnp.dot(a, b, preferred_element_type=F32)


def _dot_nt(a, b):
    return lax.dot_general(a, b, (((1,), (1,)), ((), ())), preferred_element_type=F32)


def _dot_tn(a, b):
    return lax.dot_general(a, b, (((0,), (0,)), ((), ())), preferred_element_type=F32)


def _split_bf16(x):
    hi = x.astype(BF16)
    lo = (x - hi.astype(F32)).astype(BF16)
    return hi, lo


def _full(shape):
    n = len(shape)
    return pl.BlockSpec(shape, lambda *_: (0,) * n)


def _slot_perm(kv_heads, n_heads=16):
    grp = n_heads // kv_heads
    cols = []
    for s in range(n_heads // 2):
        p, r = divmod(s, grp)
        for h in ((2 * p) * grp + r, (2 * p + 1) * grp + r):
            cols.append(h * HEAD_DIM + np.arange(HEAD_DIM))
    return np.concatenate(cols)


def _slot_heads(kv_heads, n_heads=16):
    grp = n_heads // kv_heads
    out = []
    for s in range(n_heads // 2):
        p, r = divmod(s, grp)
        out.append(((2 * p) * grp + r, (2 * p + 1) * grp + r))
    return out


def _group_mean_matrix(groups, width):
    m = np.zeros((width, width), np.float32)
    for lo, hi in groups:
        m[lo:hi, lo:hi] = 1.0 / (hi - lo)
    return m


def _rope_tables(pos, width, segs):
    pos = np.asarray(pos, np.float64)
    cos = np.ones((pos.shape[0], width), np.float64)
    sin = np.zeros((pos.shape[0], width), np.float64)
    up = np.zeros((1, width), np.float32)
    for lo, dim in segs:
        half = dim // 2
        inv = ROPE_THETA ** (-np.arange(half, dtype=np.float64) / half)
        ang = pos[:, None] * inv[None, :]
        cos[:, lo:lo + half] = np.cos(ang)
        cos[:, lo + half:lo + dim] = np.cos(ang)
        sin[:, lo:lo + half] = -np.sin(ang)
        sin[:, lo + half:lo + dim] = np.sin(ang)
        up[:, lo:lo + half] = 1.0
    return cos.astype(np.float32), sin.astype(np.float32), up


def _head64_segs(width):
    return [(lo, HEAD_DIM) for lo in range(0, width, HEAD_DIM)]


def _rms_rows(x, g):
    ms = jnp.mean(x * x, axis=-1, keepdims=True)
    return x * lax.rsqrt(ms + NORM_EPS) * g


def _group_norm(a, gm, gain):
    ms = _dot((a * a).astype(BF16), gm)
    return a * lax.rsqrt(ms + NORM_EPS) * gain


def _rope(y, cos, sin, up, shift):
    w = y.shape[-1]
    hi = pltpu.roll(y, w - shift, 1)
    lo = pltpu.roll(y, shift, 1)
    partner = jnp.where(up > 0.5, hi, lo)
    return y * cos + partner * sin


def _gelu_tanh(x):
    return 0.5 * x * (1.0 + jnp.tanh(0.7978845608028654 * (x + 0.044715 * (x * x * x))))


def _outproj_kernel(x_ref, o_ref, w_ref, y_ref):
    y_ref[...] = x_ref[...] + _dot(o_ref[...].astype(BF16), w_ref[...])


def _outproj(x, o, w, tn):
    n, d = x.shape
    k = o.shape[1]
    return pl.pallas_call(
        _outproj_kernel,
        out_shape=jax.ShapeDtypeStruct((n, d), F32),
        grid=(n // tn,),
        in_specs=[pl.BlockSpec((tn, d), lambda i: (i, 0)),
                  pl.BlockSpec((tn, k), lambda i: (i, 0)),
                  _full((k, d))],
        out_specs=pl.BlockSpec((tn, d), lambda i: (i, 0)),
        compiler_params=_cparams("parallel"),
    )(x, o, w)


def _mlp_kernel(x_ref, g_ref, w1_ref, w2_ref, y_ref, *, ff_chunk):
    x = x_ref[...]
    h = _rms_rows(x, g_ref[...]).astype(BF16)
    acc = x
    for c in range(w1_ref.shape[1] // ff_chunk):
        u = _dot(h, w1_ref[:, c * ff_chunk:(c + 1) * ff_chunk])
        u = jnp.maximum(u, 0.0)
        acc = acc + _dot((u * u).astype(BF16), w2_ref[c * ff_chunk:(c + 1) * ff_chunk, :])
    y_ref[...] = acc


def _mlp(x, g, w1, w2, tn):
    n, d = x.shape
    ff = w1.shape[1]
    return pl.pallas_call(
        functools.partial(_mlp_kernel, ff_chunk=1024),
        out_shape=jax.ShapeDtypeStruct((n, d), F32),
        grid=(n // tn,),
        in_specs=[pl.BlockSpec((tn, d), lambda i: (i, 0)),
                  _full((1, d)), _full((d, ff)), _full((ff, d))],
        out_specs=pl.BlockSpec((tn, d), lambda i: (i, 0)),
        compiler_params=_cparams("parallel"),
    )(x, g, w1, w2)


def _moba_proj_kernel(x_ref, g_ref, w_ref, gm_ref, c_ref, cos_ref, sin_ref, q_ref, kv_ref, kvb_ref):
    h = _rms_rows(x_ref[...], g_ref[...]).astype(BF16)
    gm = gm_ref[...]
    cos, sin = cos_ref[...], sin_ref[...]
    qg, kg, up = c_ref[0:1, :], c_ref[1:2, :], c_ref[2:3, :]
    for c in range(4):
        a = _dot(h, w_ref[:, c * 256:(c + 1) * 256])
        y = _rope(_group_norm(a, gm, qg), cos, sin, up, HEAD_DIM // 2)
        q_ref[:, c * 256:(c + 1) * 256] = (y * HEAD_DIM ** -0.5).astype(q_ref.dtype)
    a = _dot(h, w_ref[:, 1024:1280])
    k = _rope(_group_norm(a, gm, kg), cos, sin, up, HEAD_DIM // 2)
    v = _dot(h, w_ref[:, 1280:1536])
    kv_ref[:, 0:256] = k
    kv_ref[:, 256:512] = v
    kvb_ref[:, 0:256] = k.astype(BF16)
    kvb_ref[:, 256:512] = v.astype(BF16)


def _moba_proj(x, g, w, qn, kn, pos, tn, qdtype):
    n, d = x.shape
    period = pos.shape[0]
    cos, sin, up = _rope_tables(pos, 256, _head64_segs(256))
    consts = np.zeros((8, 256), np.float32)
    consts[2] = up[0]
    consts = jnp.asarray(consts).at[0].set(jnp.tile(qn, 4)).at[1].set(jnp.tile(kn, 4))
    gm = jnp.asarray(_group_mean_matrix([(lo, lo + 64) for lo in range(0, 256, 64)], 256), BF16)
    nper = period // tn
    tab = pl.BlockSpec((tn, 256), lambda i: (i % nper, 0))
    return pl.pallas_call(
        _moba_proj_kernel,
        out_shape=(jax.ShapeDtypeStruct((n, 1024), qdtype),
                   jax.ShapeDtypeStruct((n, 512), F32),
                   jax.ShapeDtypeStruct((n, 512), BF16)),
        grid=(n // tn,),
        in_specs=[pl.BlockSpec((tn, d), lambda i: (i, 0)), _full((1, d)), _full(w.shape),
                  _full((256, 256)), _full((8, 256)), tab, tab],
        out_specs=(pl.BlockSpec((tn, 1024), lambda i: (i, 0)),
                   pl.BlockSpec((tn, 512), lambda i: (i, 0)),
                   pl.BlockSpec((tn, 512), lambda i: (i, 0))),
        compiler_params=_cparams("parallel"),
    )(x, g, w, gm, consts, jnp.asarray(cos), jnp.asarray(sin))


def _topk_mask(score, k, n_valid):
    rows, w = score.shape
    lane = lax.broadcasted_iota(jnp.int32, (rows, w), 1)
    rank = jnp.zeros((rows, w), jnp.int32)
    for j in range(n_valid):
        sj = score[:, j:j + 1]
        ahead = (sj > score) | ((sj == score) & (lane > j))
        rank = rank + ahead.astype(jnp.int32)
    return rank < k


def _slot_rows(q_ref, n_slots):
    tq = q_ref.shape[0]
    lane = lax.broadcasted_iota(jnp.int32, (tq, LANES), 1)
    ev, od = [], []
    for s in range(n_slots):
        qs = q_ref[:, s * LANES:(s + 1) * LANES]
        ev.append(jnp.where(lane < HEAD_DIM, qs, jnp.zeros_like(qs)))
        od.append(jnp.where(lane >= HEAD_DIM, qs, jnp.zeros_like(qs)))
    return jnp.concatenate(ev + od, axis=0)


def _slot_out(o, n_slots, tq):
    lane = lax.broadcasted_iota(jnp.int32, (tq, LANES), 1)
    outs = []
    for s in range(n_slots):
        e = o[s * tq:(s + 1) * tq, :]
        d = o[(n_slots + s) * tq:(n_slots + s + 1) * tq, :]
        outs.append(jnp.where(lane < HEAD_DIM, e, d))
    return jnp.concatenate(outs, axis=1)


def _moba_prompt_kernel(q_ref, k_ref, v_ref, bm_ref, o_ref, m_ref, l_ref, acc_ref):
    i = pl.program_id(2)
    tq = q_ref.shape[0]
    nb = k_ref.shape[0] // MOBA_BLOCK
    rows = 8 * tq
    qa = _slot_rows(q_ref, 4)
    bhi, blo = _split_bf16(bm_ref[...])
    gate = _dot_nt(qa, bhi) + _dot_nt(qa, blo)
    blk = lax.broadcasted_iota(jnp.int32, (rows, nb), 1)
    gate = jnp.where(blk < i, gate, -jnp.inf)
    sel = _topk_mask(gate, MOBA_TOPK, nb) & (blk < i)
    self32 = sel.astype(F32)

    kd = k_ref[pl.ds(pl.multiple_of(i * MOBA_BLOCK, MOBA_BLOCK), MOBA_BLOCK), :]
    vd = v_ref[pl.ds(pl.multiple_of(i * MOBA_BLOCK, MOBA_BLOCK), MOBA_BLOCK), :]
    s = _dot_nt(qa, kd)
    qi = lax.broadcasted_iota(jnp.int32, (rows, MOBA_BLOCK), 0) & (tq - 1)
    ki = lax.broadcasted_iota(jnp.int32, (rows, MOBA_BLOCK), 1)
    ok = ki <= qi
    s = jnp.where(ok, s, NEG)
    m = jnp.max(s, axis=-1, keepdims=True)
    p = jnp.where(ok, jnp.exp(s - m), 0.0)
    m_ref[...] = m
    l_ref[...] = jnp.sum(p, axis=-1, keepdims=True)
    acc_ref[...] = _dot(p.astype(BF16), vd)

    for j in range(nb - 1):
        @pl.when(j < i)
        def _(j=j):
            kj = k_ref[j * MOBA_BLOCK:(j + 1) * MOBA_BLOCK, :]
            vj = v_ref[j * MOBA_BLOCK:(j + 1) * MOBA_BLOCK, :]
            okj = self32[:, j:j + 1] > 0.5
            sj = jnp.where(okj, _dot_nt(qa, kj), NEG)
            m_old = m_ref[...]
            m_new = jnp.maximum(m_old, jnp.max(sj, axis=-1, keepdims=True))
            pj = jnp.where(okj, jnp.exp(sj - m_new), 0.0)
            alpha = jnp.exp(m_old - m_new)
            m_ref[...] = m_new
            l_ref[...] = alpha * l_ref[...] + jnp.sum(pj, axis=-1, keepdims=True)
            acc_ref[...] = alpha * acc_ref[...] + _dot(pj.astype(BF16), vj)

    o = acc_ref[...] / l_ref[...]
    o_ref[...] = _slot_out(o, 4, tq).astype(o_ref.dtype)


def _moba_prompt_attn(q, kvb, bm, batch, seq):
    tq = MOBA_BLOCK
    nq = seq // tq
    nb = seq // MOBA_BLOCK
    return pl.pallas_call(
        _moba_prompt_kernel,
        out_shape=jax.ShapeDtypeStruct((batch * seq, 1024), BF16),
        grid=(batch, 2, nq),
        in_specs=[pl.BlockSpec((tq, 512), lambda b, p, i: (b * nq + i, p)),
                  pl.BlockSpec((seq, 128), lambda b, p, i: (b, p)),
                  pl.BlockSpec((seq, 128), lambda b, p, i: (b, 2 + p)),
                  pl.BlockSpec((nb, 128), lambda b, p, i: (b, p))],
        out_specs=pl.BlockSpec((tq, 512), lambda b, p, i: (b * nq + i, p)),
        scratch_shapes=[pltpu.VMEM((8 * tq, 1), F32), pltpu.VMEM((8 * tq, 1), F32),
                        pltpu.VMEM((8 * tq, 128), F32)],
        compiler_params=_cparams("parallel", "parallel", "arbitrary"),
    )(q, kvb, kvb, bm)


def _bmean_kernel(k_ref, o_ref):
    nb = o_ref.shape[0]
    k = k_ref[...].reshape(nb, MOBA_BLOCK, k_ref.shape[1])
    o_ref[...] = jnp.sum(k, axis=1) * (1.0 / MOBA_BLOCK)


def _moba_bmean(kv, batch, seq):
    nb = seq // MOBA_BLOCK
    return pl.pallas_call(
        _bmean_kernel,
        out_shape=jax.ShapeDtypeStruct((batch * nb, 256), F32),
        grid=(batch,),
        in_specs=[pl.BlockSpec((seq, 256), lambda b: (b, 0))],
        out_specs=pl.BlockSpec((nb, 256), lambda b: (b, 0)),
        compiler_params=_cparams("parallel"),
    )(kv)


def _moba_sample_kernel(pt_ref, q_ref, kvn_ref, *refs, bps, nb_past):
    pages = refs[:2 * bps]
    o_ref = refs[2 * bps]
    qa_ref, bm_ref, mst_ref, lst_ref, oacc_ref = refs[2 * bps + 1:]
    s = pl.program_id(1)
    rows = 128
    lane128 = lax.broadcasted_iota(jnp.int32, (8, LANES), 1)

    @pl.when(s == 0)
    def _():
        q = q_ref[...]
        z = jnp.zeros((8, LANES), F32)
        ev, od = [], []
        for sl in range(8):
            qs = q[:, sl * LANES:(sl + 1) * LANES]
            e = jnp.where(lane128 < HEAD_DIM, qs, 0.0)
            d = jnp.where(lane128 >= HEAD_DIM, qs, 0.0)
            if sl // 4 == 0:
                ev.append(jnp.concatenate([e, z], axis=1))
                od.append(jnp.concatenate([d, z], axis=1))
            else:
                ev.append(jnp.concatenate([z, e], axis=1))
                od.append(jnp.concatenate([z, d], axis=1))
        qa_ref[...] = jnp.concatenate(ev + od, axis=0).astype(BF16)
        bm_ref[...] = jnp.zeros_like(bm_ref)
        mst_ref[...] = jnp.zeros_like(mst_ref)
        lst_ref[...] = jnp.zeros_like(lst_ref)

    qa = qa_ref[...]
    col = lax.broadcasted_iota(jnp.int32, (rows, LANES), 1)
    for t in range(bps):
        jg = s * bps + t
        k = jnp.concatenate([pages[2 * t][:, 0:256], pages[2 * t + 1][:, 0:256]], axis=0)
        v = jnp.concatenate([pages[2 * t][:, 256:512], pages[2 * t + 1][:, 256:512]], axis=0)
        bm_ref[pl.ds(jg, 1), :] = jnp.sum(k, axis=0, keepdims=True) * (1.0 / MOBA_BLOCK)
        sc = _dot_nt(qa, k.astype(BF16))
        m = jnp.max(sc, axis=-1, keepdims=True)
        p = jnp.exp(sc - m)
        oacc_ref[jg] = _dot(p.astype(BF16), v.astype(BF16))
        mst_ref[...] = jnp.where(col == jg, m, mst_ref[...])
        lst_ref[...] = jnp.where(col == jg, jnp.sum(p, axis=-1, keepdims=True), lst_ref[...])

    @pl.when(s == pl.num_programs(1) - 1)
    def _():
        pad = jnp.zeros((LANES - 8, 256), F32)
        kn = jnp.concatenate([kvn_ref[:, 0:256], pad], axis=0).astype(BF16)
        vn = jnp.concatenate([kvn_ref[:, 256:512], pad], axis=0).astype(BF16)
        sn = _dot_nt(qa, kn)
        qi = lax.broadcasted_iota(jnp.int32, (rows, LANES), 0) & 7
        okn = col <= qi
        sn = jnp.where(okn, sn, NEG)
        mn = jnp.max(sn, axis=-1, keepdims=True)
        pn = jnp.where(okn, jnp.exp(sn - mn), 0.0)
        ln = jnp.sum(pn, axis=-1, keepdims=True)
        on = _dot(pn.astype(BF16), vn)
        bhi, blo = _split_bf16(bm_ref[...])
        gate = _dot_nt(qa, bhi) + _dot_nt(qa, blo)
        past = col < nb_past
        gate = jnp.where(past, gate, -jnp.inf)
        sel = _topk_mask(gate, MOBA_TOPK, nb_past) & past
        mst = mst_ref[...]
        mstar = jnp.maximum(jnp.max(jnp.where(sel, mst, NEG), axis=-1, keepdims=True), mn)
        w = jnp.where(sel, jnp.exp(mst - mstar), 0.0)
        wn = jnp.exp(mn - mstar)
        den = jnp.sum(w * lst_ref[...], axis=-1, keepdims=True) + wn * ln
        acc = wn * on
        for j in range(nb_past):
            acc = acc + w[:, j:j + 1] * oacc_ref[j]
        o = acc / den
        outs = []
        for sl in range(8):
            c = (sl // 4) * LANES
            e = o[sl * 8:(sl + 1) * 8, c:c + LANES]
            d = o[(8 + sl) * 8:(9 + sl) * 8, c:c + LANES]
            outs.append(jnp.where(lane128 < HEAD_DIM, e, d))
        o_ref[...] = jnp.concatenate(outs, axis=1)


def _moba_sample_attn(q, kvn, cache, page_table, bps=2):
    db, n_pages = page_table.shape
    nb_past = n_pages * PAGE // MOBA_BLOCK
    assert n_pages % (2 * bps) == 0 and nb_past < LANES
    nsteps = n_pages // (2 * bps)

    def page_spec(t):
        return pl.BlockSpec((None, PAGE, 512), lambda b, s, pt: (pt[b, s * 2 * bps + t], 0, 0))

    grid_spec = pltpu.PrefetchScalarGridSpec(
        num_scalar_prefetch=1,
        grid=(db, nsteps),
        in_specs=[pl.BlockSpec((8, 1024), lambda b, s, pt: (b, 0)),
                  pl.BlockSpec((8, 512), lambda b, s, pt: (b, 0))]
                 + [page_spec(t) for t in range(2 * bps)],
        out_specs=pl.BlockSpec((8, 1024), lambda b, s, pt: (b, 0)),
        scratch_shapes=[pltpu.VMEM((128, 256), BF16), pltpu.VMEM((LANES, 256), F32),
                        pltpu.VMEM((128, LANES), F32), pltpu.VMEM((128, LANES), F32),
                        pltpu.VMEM((nb_past, 128, 256), F32)],
    )
    return pl.pallas_call(
        functools.partial(_moba_sample_kernel, bps=bps, nb_past=nb_past),
        out_shape=jax.ShapeDtypeStruct((db * 8, 1024), F32),
        grid_spec=grid_spec,
        compiler_params=_cparams("parallel", "arbitrary"),
    )(page_table, q, kvn, *([cache] * (2 * bps)))


def _moba_layer(xp, xs, cache, page_table, ln_g, w_in, qn, kn, w_o, batch, seq, past_len):
    perm = _slot_perm(MOBA_KV_HEADS)
    w_in_p = jnp.concatenate([w_in[:, :1024][:, perm], w_in[:, 1024:]], axis=1).astype(BF16)
    w_o_p = w_o[perm, :].astype(BF16)
    g = ln_g.reshape(1, -1)
    dec = 8
    q, kv, kvb = _moba_proj(xp, g, w_in_p, qn, kn, np.arange(seq), 512, BF16)
    bm = _moba_bmean(kv, batch, seq)
    o = _moba_prompt_attn(q, kvb, bm, batch, seq)
    yp = _outproj(xp, o, w_o_p, 512)
    ns = xs.shape[0]
    pos_s = np.tile(past_len + np.arange(dec), ns // dec)
    qs, kvs, _ = _moba_proj(xs, g, w_in_p, qn, kn, pos_s, ns, F32)
    os_ = _moba_sample_attn(qs, kvs, cache.reshape(cache.shape[0], PAGE, 512), page_table)
    ys = _outproj(xs, os_, w_o_p, ns)
    return yp, ys, kv, kvs


def _nsa_proj_kernel(x_ref, g_ref, w_ref, gm_ref, c_ref, cos_ref, sin_ref,
                     q_ref, kv_ref, kvb_ref, gate_ref):
    h = _rms_rows(x_ref[...], g_ref[...]).astype(BF16)
    gm = gm_ref[...]
    cos, sin = cos_ref[...], sin_ref[...]
    qg, up = c_ref[0:1, :], c_ref[4:5, :]
    for c in range(4):
        a = _dot(h, w_ref[:, c * 256:(c + 1) * 256])
        y = _rope(_group_norm(a, gm, qg), cos, sin, up, HEAD_DIM // 2)
        q_ref[:, c * 256:(c + 1) * 256] = (y * HEAD_DIM ** -0.5).astype(q_ref.dtype)
    gm1 = gm_ref[0:128, 0:128]
    for br in range(3):
        a = _dot(h, w_ref[:, 1024 + br * 256:1024 + (br + 1) * 256])
        k = _rope(_group_norm(a[:, 0:128], gm1, c_ref[1 + br:2 + br, 0:128]),
                  cos[:, 0:128], sin[:, 0:128], up[:, 0:128], HEAD_DIM // 2)
        v = a[:, 128:256]
        kv_ref[:, br * 256:br * 256 + 128] = k
        kv_ref[:, br * 256 + 128:(br + 1) * 256] = v
        kvb_ref[:, br * 256:br * 256 + 128] = k.astype(BF16)
        kvb_ref[:, br * 256 + 128:(br + 1) * 256] = v.astype(BF16)
    a = _dot(h, w_ref[:, 1792:1920])
    gate_ref[...] = 1.0 / (1.0 + jnp.exp(-a))


def _nsa_proj(x, g, w, qn, kn, pos, tn, qdtype):
    n, d = x.shape
    period = pos.shape[0]
    cos, sin, up = _rope_tables(pos, 256, _head64_segs(256))
    consts = np.zeros((8, 256), np.float32)
    consts[4] = up[0]
    consts = jnp.asarray(consts).at[0].set(jnp.tile(qn, 4))
    for br in range(3):
        consts = consts.at[1 + br].set(jnp.tile(kn[br], 4))
    gm = jnp.asarray(_group_mean_matrix([(lo, lo + 64) for lo in range(0, 256, 64)], 256), BF16)
    nper = period // tn
    tab = pl.BlockSpec((tn, 256), lambda i: (i % nper, 0))
    row = lambda wd: pl.BlockSpec((tn, wd), lambda i: (i, 0))
    return pl.pallas_call(
        _nsa_proj_kernel,
        out_shape=(jax.ShapeDtypeStruct((n, 1024), qdtype),
                   jax.ShapeDtypeStruct((n, 768), F32),
                   jax.ShapeDtypeStruct((n, 768), BF16),
                   jax.ShapeDtypeStruct((n, 128), F32)),
        grid=(n // tn,),
        in_specs=[row(d), _full((1, d)), _full(w.shape), _full((256, 256)), _full((8, 256)), tab, tab],
        out_specs=(row(1024), row(768), row(768), row(128)),
        compiler_params=_cparams("parallel"),
    )(x, g, w, gm, consts, jnp.asarray(cos), jnp.asarray(sin))


def _nsa_cmp_weights(cmp_pe, cmp_w1, cmp_w2):
    wbig, pe_rows, w2blk = [], [], []
    for i in range(2):
        w1 = cmp_w1[i].reshape(2, 16, HEAD_DIM, NSA_CMP_HIDDEN)
        z = jnp.zeros_like(w1)
        per_g = []
        for g in range(2):
            blk = jnp.stack([w1 if gp == g else z for gp in range(2)], axis=2)
            blk = blk.transpose(1, 2, 3, 0, 4).reshape(2048, 2 * NSA_CMP_HIDDEN)
            per_g.append(blk)
        wbig.append(jnp.concatenate(per_g, axis=1).astype(BF16))
        pe = cmp_pe[i].reshape(2, 16, 1, HEAD_DIM)
        pe = jnp.broadcast_to(pe, (2, 16, 2, HEAD_DIM)).reshape(2, 2048)
        pe_rows.append(jnp.concatenate([pe, jnp.zeros((6, 2048), F32)], axis=0))
        z2 = jnp.zeros_like(cmp_w2[i])
        w2blk.append(jnp.concatenate([jnp.concatenate([cmp_w2[i], z2], axis=1),
                                      jnp.concatenate([z2, cmp_w2[i]], axis=1)], axis=0).astype(BF16))
    return wbig, pe_rows, w2blk


def _cmp_hidden(pre_a, pre_b, pc, b1):
    hs = []
    for g in range(2):
        c = pc[0:1, g * 512:g * 512 + 256] + pc[1:2, g * 512 + 256:(g + 1) * 512] + b1
        hs.append(_gelu_tanh(pre_a[:, g * 512:g * 512 + 256] + pre_b[:, g * 512 + 256:(g + 1) * 512] + c))
    return jnp.concatenate(hs, axis=1)


def _nsa_cmp_prompt_kernel(k_ref, v_ref, wk_ref, wv_ref, pek_ref, pev_ref, b1_ref, w2k_ref, w2v_ref, o_ref):
    nch = o_ref.shape[0]
    for i, (r_ref, w_ref, pe_ref, w2_ref) in enumerate(((k_ref, wk_ref, pek_ref, w2k_ref),
                                                         (v_ref, wv_ref, pev_ref, w2v_ref))):
        x = jnp.concatenate([r_ref[pl.ds(l, nch, stride=NSA_CMP_STRIDE), :] for l in range(NSA_CMP_STRIDE)],
                            axis=1).astype(BF16)
        pre = _dot(x, w_ref[...])
        pc = _dot(pe_ref[...].astype(BF16), w_ref[...])
        nxt = pltpu.roll(pre, nch - 1, 0)
        hid = _cmp_hidden(pre, nxt, pc, b1_ref[i:i + 1, :])
        o_ref[:, i * 128:(i + 1) * 128] = _dot(hid.astype(BF16), w2_ref[...])


def _nsa_cmp_prompt(kv, wbig, pe_rows, b1, w2blk, batch, seq):
    nch = seq // NSA_CMP_STRIDE
    return pl.pallas_call(
        _nsa_cmp_prompt_kernel,
        out_shape=jax.ShapeDtypeStruct((batch * nch, 256), F32),
        grid=(batch,),
        in_specs=[pl.BlockSpec((seq, 128), lambda b: (b, 0)), pl.BlockSpec((seq, 128), lambda b: (b, 1)),
                  _full((2048, 1024)), _full((2048, 1024)), _full((8, 2048)), _full((8, 2048)),
                  _full((8, 256)), _full((512, 128)), _full((512, 128))],
        out_specs=pl.BlockSpec((nch, 256), lambda b: (b, 0)),
        compiler_params=_cparams("parallel"),
    )(kv, kv, wbig[0], wbig[1], pe_rows[0], pe_rows[1], b1, w2blk[0], w2blk[1])


def _flash_step(s, ok, v, m_ref, l_ref, acc_ref):
    s = jnp.where(ok, s, NEG)
    m_old = m_ref[...]
    m_new = jnp.maximum(m_old, jnp.max(s, axis=-1, keepdims=True))
    p = jnp.where(ok, jnp.exp(s - m_new), 0.0)
    alpha = jnp.exp(m_old - m_new)
    m_ref[...] = m_new
    l_ref[...] = alpha * l_ref[...] + jnp.sum(p, axis=-1, keepdims=True)
    acc_ref[...] = alpha * acc_ref[...] + _dot(p.astype(BF16), v)


def _flash_init(m_ref, l_ref, acc_ref):
    m_ref[...] = jnp.full(m_ref.shape, NEG, F32)
    l_ref[...] = jnp.zeros(l_ref.shape, F32)
    acc_ref[...] = jnp.zeros(acc_ref.shape, F32)


def _nsa_prompt_kernel(q_ref, ks_ref, vs_ref, kw_ref, vw_ref, kvc_ref, gate_ref, ovl_ref, e_ref,
                       o_ref, m_ref, l_ref, acc_ref, os_ref):
    i = pl.program_id(1)
    tq = q_ref.shape[0]
    rows = 16 * tq
    tk = 256
    q0 = i * tq
    qa = _slot_rows(q_ref, 8)
    ntok = kvc_ref.shape[0]

    kc = kvc_ref[:, 0:128].astype(BF16)
    vc = kvc_ref[:, 128:256].astype(BF16)
    qpos_c = q0 + (lax.broadcasted_iota(jnp.int32, (rows, ntok), 0) & (tq - 1))
    tok = lax.broadcasted_iota(jnp.int32, (rows, ntok), 1)
    ok = tok * NSA_CMP_STRIDE + (NSA_CMP_LEN - 1) <= qpos_c
    sc = jnp.where(ok, _dot_nt(qa, kc), NEG)
    mc = jnp.max(sc, axis=-1, keepdims=True)
    ec = jnp.where(ok, jnp.exp(sc - mc), 0.0)
    pc = ec / jnp.maximum(jnp.sum(ec, axis=-1, keepdims=True), TINY)
    o_c = _dot(pc.astype(BF16), vc)

    pg = jnp.sum(pc.reshape(2, 8, tq, ntok), axis=1).reshape(2 * tq, ntok)
    phi, plo = _split_bf16(pg)
    imp = _dot(phi, ovl_ref[...]) + _dot(plo, ovl_ref[...])
    qp2 = q0 + (lax.broadcasted_iota(jnp.int32, (2 * tq, LANES), 0) & (tq - 1))
    own = qp2 >> 6
    jb = lax.broadcasted_iota(jnp.int32, (2 * tq, LANES), 1)
    allowed = jb <= own
    forced = (jb == 0) | (jb == own) | (jb == own - 1)
    imp = jnp.where(forced, NSA_FORCE_SCORE, imp)
    imp = jnp.where(allowed, imp, -jnp.inf)
    nsb = ks_ref.shape[0] // NSA_SLC_BLOCK
    sel = _topk_mask(imp, NSA_SLC_TOPN, nsb) & allowed
    selb = jnp.where(sel, 1.0, 0.0).astype(BF16)

    _flash_init(m_ref, l_ref, acc_ref)
    qp_k = q0 + (lax.broadcasted_iota(jnp.int32, (2 * tq, tk), 0) & (tq - 1))
    kl = lax.broadcasted_iota(jnp.int32, (2 * tq, tk), 1)

    def slc_body(j, carry):
        off = pl.multiple_of(j * tk, tk)
        mk = _dot(selb, e_ref[j])
        okg = jnp.where((mk > 0.5) & (off + kl <= qp_k), 1.0, 0.0)
        okr = jnp.broadcast_to(okg.reshape(2, 1, tq, tk), (2, 8, tq, tk)).reshape(rows, tk) > 0.5
        _flash_step(_dot_nt(qa, ks_ref[pl.ds(off, tk), :]), okr, vs_ref[pl.ds(off, tk), :],
                    m_ref, l_ref, acc_ref)
        return carry

    lax.fori_loop(0, (q0 + tq + tk - 1) // tk, slc_body, 0)
    os_ref[...] = acc_ref[...] / l_ref[...]

    _flash_init(m_ref, l_ref, acc_ref)
    qp_r = q0 + (lax.broadcasted_iota(jnp.int32, (rows, tk), 0) & (tq - 1))
    klr = lax.broadcasted_iota(jnp.int32, (rows, tk), 1)

    def win_body(j, carry):
        off = pl.multiple_of(j * tk, tk)
        dist = qp_r - (off + klr)
        okw = (dist >= 0) & (dist < NSA_WINDOW)
        _flash_step(_dot_nt(qa, kw_ref[pl.ds(off, tk), :]), okw, vw_ref[pl.ds(off, tk), :],
                    m_ref, l_ref, acc_ref)
        return carry

    lax.fori_loop(jnp.maximum(q0 - (NSA_WINDOW - 1), 0) // tk, (q0 + tq + tk - 1) // tk, win_body, 0)
    o_w = acc_ref[...] / l_ref[...]
    o_s = os_ref[...]

    gate = gate_ref[...]
    outs = []
    for h in range(16):
        r0 = h * tq
        outs.append(gate[:, 3 * h:3 * h + 1] * o_c[r0:r0 + tq, :]
                    + gate[:, 3 * h + 1:3 * h + 2] * o_s[r0:r0 + tq, :]
                    + gate[:, 3 * h + 2:3 * h + 3] * o_w[r0:r0 + tq, :])
    o_ref[...] = _slot_out(jnp.concatenate(outs, axis=0), 8, tq).astype(o_ref.dtype)


def _nsa_overlap(n_tok, n_blk, tok_shift, rows, cols):
    ovl = np.zeros((rows, cols), np.float32)
    t = np.arange(n_tok)[:, None] * NSA_CMP_STRIDE
    b = np.arange(n_blk)[None, :] * NSA_SLC_BLOCK
    ovl[tok_shift:tok_shift + n_tok, :n_blk] = ((t < b + NSA_SLC_BLOCK) & (t + NSA_CMP_LEN > b))
    return ovl


def _nsa_prompt_attn(q, kvb, kvc, gate, batch, seq):
    tq, tk = 128, 256
    nq = seq // tq
    nch = seq // NSA_CMP_STRIDE
    n_tok = (seq - NSA_CMP_LEN) // NSA_CMP_STRIDE + 1
    nsb = seq // NSA_SLC_BLOCK
    assert nch == LANES and nsb <= LANES
    ovl = jnp.asarray(_nsa_overlap(n_tok, nsb, 0, nch, LANES), BF16)
    e = np.zeros((seq // tk, LANES, tk), np.float32)
    for j in range(seq // tk):
        blk = (j * tk + np.arange(tk)) // NSA_SLC_BLOCK
        e[j, blk, np.arange(tk)] = 1.0
    e = jnp.asarray(e, BF16)
    seqcol = lambda c: pl.BlockSpec((seq, 128), lambda b, i: (b, c))
    return pl.pallas_call(
        _nsa_prompt_kernel,
        out_shape=jax.ShapeDtypeStruct((batch * seq, 1024), BF16),
        grid=(batch, nq),
        in_specs=[pl.BlockSpec((tq, 1024), lambda b, i: (b * nq + i, 0)),
                  seqcol(2), seqcol(3), seqcol(4), seqcol(5),
                  pl.BlockSpec((nch, 256), lambda b, i: (b, 0)),
                  pl.BlockSpec((tq, 128), lambda b, i: (b * nq + i, 0)),
                  _full((nch, LANES)), _full(e.shape)],
        out_specs=pl.BlockSpec((tq, 1024), lambda b, i: (b * nq + i, 0)),
        scratch_shapes=[pltpu.VMEM((16 * tq, 1), F32), pltpu.VMEM((16 * tq, 1), F32),
                        pltpu.VMEM((16 * tq, 128), F32), pltpu.VMEM((16 * tq, 128), F32)],
        compiler_params=_cparams("parallel", "arbitrary"),
    )(q, kvb, kvb, kvb, kvb, kvc, gate, ovl, e)


def _topk_mask_iter(score, k):
    rows, w = score.shape
    lane = lax.broadcasted_iota(jnp.int32, (rows, w), 1)
    taken = jnp.zeros((rows, w), jnp.int32)
    for _ in range(k):
        free = taken == 0
        cur = jnp.where(free, score, -jnp.inf)
        m = jnp.max(cur, axis=-1, keepdims=True)
        idx = jnp.min(jnp.where(free & (cur == m), lane, w), axis=-1, keepdims=True)
        taken = jnp.where(lane == idx, 1, taken)
    return taken > 0


def _sample_q_rows(q):
    lane = lax.broadcasted_iota(jnp.int32, (8, LANES), 1)
    ev = [jnp.where(lane < HEAD_DIM, q[:, s * LANES:(s + 1) * LANES], 0.0) for s in range(8)]
    od = [jnp.where(lane >= HEAD_DIM, q[:, s * LANES:(s + 1) * LANES], 0.0) for s in range(8)]
    return jnp.concatenate(ev + od, axis=0)


def _nsa_sample_cmp_kernel(pt_ref, q_ref, wk_ref, wv_ref, pek_ref, pev_ref, b1_ref, w2k_ref, w2v_ref,
                           ovl_ref, *refs, pps, past, blk_per_step):
    kpages, vpages = refs[:pps], refs[pps:2 * pps]
    oc_ref, sel_ref = refs[2 * pps], refs[2 * pps + 1]
    kvc_ref, carry_ref = refs[2 * pps + 2:]
    s = pl.program_id(1)
    cpp = PAGE // NSA_CMP_STRIDE
    m = pps * cpp

    @pl.when(s == 0)
    def _():
        carry_ref[...] = jnp.zeros_like(carry_ref)

    row0 = lax.broadcasted_iota(jnp.int32, (m, 1024), 0) == 0
    for i, (pages, w_ref, pe_ref, w2_ref) in enumerate(((kpages, wk_ref, pek_ref, w2k_ref),
                                                         (vpages, wv_ref, pev_ref, w2v_ref))):
        x = jnp.concatenate(
            [jnp.concatenate([pg[pl.ds(l, cpp, stride=NSA_CMP_STRIDE), :] for pg in pages], axis=0)
             for l in range(NSA_CMP_STRIDE)], axis=1).astype(BF16)
        pre = _dot(x, w_ref[...])
        pc = _dot(pe_ref[...].astype(BF16), w_ref[...])
        prev = jnp.where(row0, carry_ref[i:i + 1, :], pltpu.roll(pre, 1, 0))
        carry_ref[i:i + 1, :] = pre[m - 1:m, :]
        hid = _cmp_hidden(prev, pre, pc, b1_ref[i:i + 1, :])
        kvc_ref[pl.ds(pl.multiple_of(s * m, m), m), i * 128:(i + 1) * 128] = _dot(hid.astype(BF16), w2_ref[...])

    @pl.when(s == pl.num_programs(1) - 1)
    def _():
        nt = kvc_ref.shape[0]
        qa = _sample_q_rows(q_ref[...]).astype(BF16)
        kc = kvc_ref[:, 0:128].astype(BF16)
        vc = kvc_ref[:, 128:256].astype(BF16)
        r = lax.broadcasted_iota(jnp.int32, (128, nt), 1)
        qpos = past + (lax.broadcasted_iota(jnp.int32, (128, nt), 0) & 7)
        ok = (r >= 1) & ((r - 1) * NSA_CMP_STRIDE + (NSA_CMP_LEN - 1) <= qpos)
        sc = jnp.where(ok, _dot_nt(qa, kc), NEG)
        mc = jnp.max(sc, axis=-1, keepdims=True)
        ec = jnp.where(ok, jnp.exp(sc - mc), 0.0)
        pc = ec / jnp.maximum(jnp.sum(ec, axis=-1, keepdims=True), TINY)
        oc_ref[...] = _dot(pc.astype(BF16), vc)
        pg = jnp.sum(pc.reshape(2, 8, 8, nt), axis=1).reshape(16, nt)
        phi, plo = _split_bf16(pg)
        imp = _dot(phi, ovl_ref[...]) + _dot(plo, ovl_ref[...])
        nl = imp.shape[1]
        own = (past + (lax.broadcasted_iota(jnp.int32, (16, nl), 0) & 7)) >> 6
        jb = lax.broadcasted_iota(jnp.int32, (16, nl), 1)
        allowed = jb <= own
        forced = (jb == 0) | (jb == own) | (jb == own - 1)
        imp = jnp.where(forced, NSA_FORCE_SCORE, imp)
        imp = jnp.where(allowed, imp, -jnp.inf)
        sel = jnp.where(_topk_mask_iter(imp, NSA_SLC_TOPN) & allowed, 1.0, 0.0)
        lane = lax.broadcasted_iota(jnp.int32, (16, LANES), 1)
        for st in range(sel_ref.shape[0]):
            piece = sel if st == 0 else pltpu.roll(sel, nl - st * blk_per_step, 1)
            sel_ref[st] = jnp.where(lane < blk_per_step, piece[:, 0:LANES], 0.0)


def _nsa_sample_cmp(q, cache, page_table, wbig, pe_rows, b1, w2blk, past, pps):
    db, n_pages = page_table.shape
    nsteps = n_pages // pps
    nt = n_pages * (PAGE // NSA_CMP_STRIDE)
    n_tok = (past + 8 - NSA_CMP_LEN) // NSA_CMP_STRIDE + 1
    nsb = -(-(past + 8) // NSA_SLC_BLOCK)
    nl = -(-nsb // LANES) * LANES
    blk_per_step = pps * PAGE // NSA_SLC_BLOCK
    assert n_tok == nt - 1 and blk_per_step <= LANES
    ovl = jnp.asarray(_nsa_overlap(n_tok, nsb, 1, nt, nl), BF16)

    def page_spec(t, c):
        return pl.BlockSpec((None, PAGE, 128), lambda b, s, pt: (pt[b, s * pps + t], 0, c))

    grid_spec = pltpu.PrefetchScalarGridSpec(
        num_scalar_prefetch=1,
        grid=(db, nsteps),
        in_specs=[pl.BlockSpec((8, 1024), lambda b, s, pt: (b, 0)),
                  _full((2048, 1024)), _full((2048, 1024)), _full((8, 2048)), _full((8, 2048)),
                  _full((8, 256)), _full((512, 128)), _full((512, 128)), _full((nt, nl))]
                 + [page_spec(t, 0) for t in range(pps)] + [page_spec(t, 1) for t in range(pps)],
        out_specs=(pl.BlockSpec((128, 128), lambda b, s, pt: (b, 0)),
                   pl.BlockSpec((None, nsteps, 16, LANES), lambda b, s, pt: (b, 0, 0, 0))),
        scratch_shapes=[pltpu.VMEM((nt, 256), F32), pltpu.VMEM((8, 1024), F32)],
    )
    return pl.pallas_call(
        functools.partial(_nsa_sample_cmp_kernel, pps=pps, past=past, blk_per_step=blk_per_step),
        out_shape=(jax.ShapeDtypeStruct((db * 128, 128), F32),
                   jax.ShapeDtypeStruct((db, nsteps, 16, LANES), F32)),
        grid_spec=grid_spec,
        compiler_params=_cparams("parallel", "arbitrary"),
    )(page_table, q, wbig[0], wbig[1], pe_rows[0], pe_rows[1], b1, w2blk[0], w2blk[1], ovl,
      *([cache] * (2 * pps)))


def _nsa_sample_attn_kernel(pt_ref, q_ref, sel_ref, oc_ref, kvn_ref, win_ref, gate_ref, e_ref, *refs,
                            pps, past):
    pages = refs[:pps]
    o_ref = refs[pps]
    qa_ref, m_ref, l_ref, acc_ref = refs[pps + 1:]
    s = pl.program_id(1)
    rows = 128

    @pl.when(s == 0)
    def _():
        qa_ref[...] = _sample_q_rows(q_ref[...]).astype(BF16)
        _flash_init(m_ref, l_ref, acc_ref)

    qa = qa_ref[...]
    kv = jnp.concatenate([pg[...] for pg in pages], axis=0)
    sel = sel_ref[...]
    selrows = jnp.concatenate([sel[0:8, :]] * 8 + [sel[8:16, :]] * 8, axis=0).astype(BF16)
    ok = _dot(selrows, e_ref[...]) > 0.5
    _flash_step(_dot_nt(qa, kv[:, 0:128].astype(BF16)), ok, kv[:, 128:256].astype(BF16),
                m_ref, l_ref, acc_ref)

    @pl.when(s == pl.num_programs(1) - 1)
    def _():
        col = lax.broadcasted_iota(jnp.int32, (rows, LANES), 1)
        qi = lax.broadcasted_iota(jnp.int32, (rows, LANES), 0) & 7
        pad = jnp.zeros((LANES - 8, 128), F32)
        kn = jnp.concatenate([kvn_ref[:, 256:384], pad], axis=0).astype(BF16)
        vn = jnp.concatenate([kvn_ref[:, 384:512], pad], axis=0).astype(BF16)
        _flash_step(_dot_nt(qa, kn), col <= qi, vn, m_ref, l_ref, acc_ref)
        o_s = acc_ref[...] / l_ref[...]
        wb = win_ref.shape[0]
        kw = jnp.concatenate([win_ref[:, 0:128], kvn_ref[:, 512:640], pad], axis=0).astype(BF16)
        vw = jnp.concatenate([win_ref[:, 128:256], kvn_ref[:, 640:768], pad], axis=0).astype(BF16)
        nw = wb + LANES
        c = lax.broadcasted_iota(jnp.int32, (rows, nw), 1)
        qpos = past + (lax.broadcasted_iota(jnp.int32, (rows, nw), 0) & 7)
        wpos = past - wb + c
        dist = qpos - wpos
        okw = (dist >= 0) & (dist < NSA_WINDOW) & (wpos >= 0) & (c < wb + 8)
        sw = jnp.where(okw, _dot_nt(qa, kw), NEG)
        mw = jnp.max(sw, axis=-1, keepdims=True)
        pw = jnp.where(okw, jnp.exp(sw - mw), 0.0)
        o_w = _dot(pw.astype(BF16), vw) / jnp.maximum(jnp.sum(pw, axis=-1, keepdims=True), TINY)
        o_c = oc_ref[...]
        gate = gate_ref[...]
        lane8 = lax.broadcasted_iota(jnp.int32, (8, LANES), 1)
        hs = []
        for h in range(16):
            r0 = h * 8
            hs.append(gate[:, 3 * h:3 * h + 1] * o_c[r0:r0 + 8, :]
                      + gate[:, 3 * h + 1:3 * h + 2] * o_s[r0:r0 + 8, :]
                      + gate[:, 3 * h + 2:3 * h + 3] * o_w[r0:r0 + 8, :])
        o_ref[...] = jnp.concatenate([jnp.where(lane8 < HEAD_DIM, hs[sl], hs[8 + sl]) for sl in range(8)],
                                     axis=1)


def _nsa_sample_attn(q, sel, o_c, kvn, state_win, gate, cache, page_table, past, pps):
    db, n_pages = page_table.shape
    nsteps = n_pages // pps
    wb = state_win.shape[1]
    nk = pps * PAGE
    e = np.zeros((LANES, nk), np.float32)
    e[np.arange(nk) // NSA_SLC_BLOCK, np.arange(nk)] = 1.0

    def page_spec(t):
        return pl.BlockSpec((None, PAGE, 256), lambda b, s, pt: (pt[b, s * pps + t], 0, 1))

    grid_spec = pltpu.PrefetchScalarGridSpec(
        num_scalar_prefetch=1,
        grid=(db, nsteps),
        in_specs=[pl.BlockSpec((8, 1024), lambda b, s, pt: (b, 0)),
                  pl.BlockSpec((None, None, 16, LANES), lambda b, s, pt: (b, s, 0, 0)),
                  pl.BlockSpec((128, 128), lambda b, s, pt: (b, 0)),
                  pl.BlockSpec((8, 768), lambda b, s, pt: (b, 0)),
                  pl.BlockSpec((None, wb, 256), lambda b, s, pt: (b, 0, 0)),
                  pl.BlockSpec((8, 128), lambda b, s, pt: (b, 0)),
                  _full((LANES, nk))]
                 + [page_spec(t) for t in range(pps)],
        out_specs=pl.BlockSpec((8, 1024), lambda b, s, pt: (b, 0)),
        scratch_shapes=[pltpu.VMEM((128, 128), BF16), pltpu.VMEM((128, 1), F32), pltpu.VMEM((128, 1), F32),
                        pltpu.VMEM((128, 128), F32)],
    )
    return pl.pallas_call(
        functools.partial(_nsa_sample_attn_kernel, pps=pps, past=past),
        out_shape=jax.ShapeDtypeStruct((db * 8, 1024), F32),
        grid_spec=grid_spec,
        compiler_params=_cparams("parallel", "arbitrary"),
    )(page_table, q, sel, o_c, kvn, state_win, gate, jnp.asarray(e, BF16), *([cache] * pps))


def _nsa_layer(xp, xs, cache, state_win, page_table, ln_g, w_in, qn, kn, cmp_pe, cmp_w1, cmp_b1, cmp_w2,
               w_o, batch, seq, past_len):
    perm = _slot_perm(NSA_KV_HEADS)
    w_in_p = jnp.concatenate([w_in[:, :1024][:, perm], w_in[:, 1024:],
                              jnp.zeros((w_in.shape[0], 1920 - w_in.shape[1]), F32)], axis=1).astype(BF16)
    w_o_p = w_o[perm, :].astype(BF16)
    g = ln_g.reshape(1, -1)
    wbig, pe_rows, w2blk = _nsa_cmp_weights(cmp_pe, cmp_w1, cmp_w2)
    b1 = jnp.concatenate([cmp_b1, jnp.zeros((6, NSA_CMP_HIDDEN), F32)], axis=0)
    dec = 8
    q, kv, kvb, gate = _nsa_proj(xp, g, w_in_p, qn, kn, np.arange(seq), 512, BF16)
    kvc = _nsa_cmp_prompt(kv, wbig, pe_rows, b1, w2blk, batch, seq)
    o = _nsa_prompt_attn(q, kvb, kvc, gate, batch, seq)
    yp = _outproj(xp, o, w_o_p, 512)
    ns = xs.shape[0]
    db, n_pages = page_table.shape
    pps = min(16, n_pages)
    pos_s = np.tile(past_len + np.arange(dec), ns // dec)
    qs, kvs, _, gs = _nsa_proj(xs, g, w_in_p, qn, kn, pos_s, ns, F32)
    cache3 = cache.reshape(cache.shape[0], PAGE, 512)
    win3 = state_win.reshape(db, state_win.shape[1], 256)
    o_c, sel = _nsa_sample_cmp(qs, cache3, page_table, wbig, pe_rows, b1, w2blk, past_len, pps)
    os_ = _nsa_sample_attn(qs, sel, o_c, kvs, win3, gs, cache3, page_table, past_len, pps)
    ys = _outproj(xs, os_, w_o_p, ns)
    return yp, ys, kv, kvs


MLA_QK = MLA_NOPE + MLA_ROPE


def _mla_proj_kernel(x_ref, g_ref, w_ref, wuq_ref, gm_ref, c_ref, cosq_ref, sinq_ref, cosk_ref, sink_ref,
                     q_ref, lat_ref):
    h = _rms_rows(x_ref[...], g_ref[...]).astype(BF16)
    cq = _rms_rows(_dot(h, w_ref[:, 0:MLA_Q_LORA]), c_ref[0:1, 0:MLA_Q_LORA]).astype(BF16)
    ckv = _rms_rows(_dot(h, w_ref[:, MLA_Q_LORA:MLA_Q_LORA + MLA_KV_LORA]), c_ref[1:2, 0:MLA_KV_LORA])
    a = _dot(h, w_ref[:, 640:768])
    ms = jnp.sum(a * a, axis=-1, keepdims=True) * (1.0 / MLA_ROPE)
    kpe = a * lax.rsqrt(ms + NORM_EPS) * c_ref[2:3, 0:128]
    kpe = _rope(kpe, cosk_ref[...], sink_ref[...], c_ref[3:4, 0:128], MLA_ROPE // 2)
    lat_ref[:, 0:MLA_KV_LORA] = ckv
    lat_ref[:, MLA_KV_LORA:MLA_KV_LORA + MLA_ROPE] = kpe[:, 0:MLA_ROPE]
    gm = gm_ref[...]
    qg, up = c_ref[4:5, 0:256], c_ref[5:6, 0:256]
    cos, sin = cosq_ref[...], sinq_ref[...]
    for c in range(8):
        a = _dot(cq, wuq_ref[:, c * 256:(c + 1) * 256])
        y = _rope(_group_norm(a, gm, qg), cos, sin, up, MLA_ROPE // 2)
        q_ref[:, c * 256:(c + 1) * 256] = (y * MLA_QK ** -0.5).astype(q_ref.dtype)


def _mla_cat_groups(width):
    g = []
    for lo in range(0, width, LANES):
        g += [(lo, lo + MLA_NOPE), (lo + MLA_NOPE, lo + MLA_QK)]
    return g


def _mla_cat_cols():
    idx = -np.ones((16, LANES), np.int64)
    for h in range(16):
        idx[h, :MLA_QK] = h * MLA_QK + np.arange(MLA_QK)
    return idx.reshape(-1)


def _take_cols(w, idx):
    wz = jnp.concatenate([w, jnp.zeros((w.shape[0], 1), w.dtype)], axis=1)
    return wz[:, np.where(idx < 0, w.shape[1], idx)]


def _mla_proj(x, g, w_dqkv, g_q, g_kv, w_uq, qn, kn, pos, tn, qdtype):
    n, d = x.shape
    period = pos.shape[0]
    w = jnp.concatenate([w_dqkv, jnp.zeros((d, 768 - w_dqkv.shape[1]), F32)], axis=1).astype(BF16)
    wuq = _take_cols(w_uq, _mla_cat_cols()).astype(BF16)
    cosq, sinq, upq = _rope_tables(pos, 256, [(MLA_NOPE, MLA_ROPE), (LANES + MLA_NOPE, MLA_ROPE)])
    cosk, sink, upk = _rope_tables(pos, 128, [(0, MLA_ROPE)])
    consts = jnp.zeros((8, 384), F32)
    consts = consts.at[0, :].set(g_q).at[1, 0:256].set(g_kv).at[2, 0:MLA_ROPE].set(kn[MLA_NOPE:])
    consts = consts.at[3, 0:128].set(jnp.asarray(upk[0])).at[5, 0:256].set(jnp.asarray(upq[0]))
    qgain = jnp.concatenate([qn, jnp.zeros((LANES - MLA_QK,), F32)])
    consts = consts.at[4, 0:256].set(jnp.tile(qgain, 2))
    gm = jnp.asarray(_group_mean_matrix(_mla_cat_groups(256), 256), BF16)
    nper = period // tn
    tabq = pl.BlockSpec((tn, 256), lambda i: (i % nper, 0))
    tabk = pl.BlockSpec((tn, 128), lambda i: (i % nper, 0))
    row = lambda wd: pl.BlockSpec((tn, wd), lambda i: (i, 0))
    return pl.pallas_call(
        _mla_proj_kernel,
        out_shape=(jax.ShapeDtypeStruct((n, 2048), qdtype), jax.ShapeDtypeStruct((n, 288), F32)),
        grid=(n // tn,),
        in_specs=[row(d), _full((1, d)), _full(w.shape), _full(wuq.shape), _full((256, 256)), _full((8, 384)),
                  tabq, tabq, tabk, tabk],
        out_specs=(row(2048), row(288)),
        compiler_params=_cparams("parallel"),
    )(x, g, w, wuq, gm, consts, jnp.asarray(cosq), jnp.asarray(sinq), jnp.asarray(cosk), jnp.asarray(sink))


def _mla_expand_kernel(lat_ref, wk_ref, wv_ref, gm_ref, c_ref, place_ref, k_ref, v_ref):
    ckv = lat_ref[:, 0:MLA_KV_LORA].astype(BF16)
    kpe = _dot(lat_ref[:, MLA_KV_LORA:MLA_KV_LORA + MLA_ROPE].astype(BF16), place_ref[...])
    gm = gm_ref[...]
    kg = c_ref[0:1, :]
    for c in range(8):
        e = _dot(ckv, wk_ref[:, c * 256:(c + 1) * 256])
        k_ref[:, c * 256:(c + 1) * 256] = (_group_norm(e, gm, kg) + kpe).astype(BF16)
    v_ref[...] = _dot(ckv, wv_ref[...]).astype(BF16)


def _mla_split_ukv(w_ukv):
    w = w_ukv.reshape(MLA_KV_LORA, 16, MLA_NOPE + MLA_V)
    return w[:, :, :MLA_NOPE].reshape(MLA_KV_LORA, 16 * MLA_NOPE), w[:, :, MLA_NOPE:].reshape(MLA_KV_LORA, 16 * MLA_V)


def _mla_expand(lat, w_ukv, kn, tn):
    n = lat.shape[0]
    wk_nat, wv = _mla_split_ukv(w_ukv)
    idx = -np.ones((16, LANES), np.int64)
    for h in range(16):
        idx[h, :MLA_NOPE] = h * MLA_NOPE + np.arange(MLA_NOPE)
    wk = _take_cols(wk_nat, idx.reshape(-1)).astype(BF16)
    gm = jnp.asarray(_group_mean_matrix([(lo, lo + MLA_NOPE) for lo in (0, LANES)], 256), BF16)
    kgain = jnp.concatenate([kn[:MLA_NOPE], jnp.zeros((LANES - MLA_NOPE,), F32)])
    consts = jnp.zeros((8, 256), F32).at[0].set(jnp.tile(kgain, 2))
    place = np.zeros((MLA_ROPE, 256), np.float32)
    for lo in (MLA_NOPE, LANES + MLA_NOPE):
        place[np.arange(MLA_ROPE), lo + np.arange(MLA_ROPE)] = 1.0
    row = lambda wd: pl.BlockSpec((tn, wd), lambda i: (i, 0))
    return pl.pallas_call(
        _mla_expand_kernel,
        out_shape=(jax.ShapeDtypeStruct((n, 2048), BF16), jax.ShapeDtypeStruct((n, 1024), BF16)),
        grid=(n // tn,),
        in_specs=[row(288), _full(wk.shape), _full((256, 1024)), _full((256, 256)), _full((8, 256)),
                  _full((MLA_ROPE, 256))],
        out_specs=(row(2048), row(1024)),
        compiler_params=_cparams("parallel"),
    )(lat, wk, wv.astype(BF16), gm, consts, jnp.asarray(place, BF16))


def _mla_prompt_kernel(q_ref, k_ref, v_ref, o_ref, m_ref, l_ref, acc_ref):
    i = pl.program_id(2)
    tq = q_ref.shape[0]
    tk = tq
    q0 = q_ref[:, 0:128]
    q1 = q_ref[:, 128:256]
    _flash_init(m_ref, l_ref, acc_ref)
    always = jnp.ones((2 * tq, tk), jnp.bool_)

    def scores(off):
        return jnp.concatenate([_dot_nt(q0, k_ref[pl.ds(off, tk), 0:128]),
                                _dot_nt(q1, k_ref[pl.ds(off, tk), 128:256])], axis=0)

    def body(j, carry):
        off = pl.multiple_of(j * tk, tk)
        _flash_step(scores(off), always, v_ref[pl.ds(off, tk), :], m_ref, l_ref, acc_ref)
        return carry

    lax.fori_loop(0, i, body, 0)
    off = pl.multiple_of(i * tk, tk)
    qi = lax.broadcasted_iota(jnp.int32, (2 * tq, tk), 0) & (tq - 1)
    ki = lax.broadcasted_iota(jnp.int32, (2 * tq, tk), 1)
    _flash_step(scores(off), ki <= qi, v_ref[pl.ds(off, tk), :], m_ref, l_ref, acc_ref)
    o = acc_ref[...] / l_ref[...]
    lane = lax.broadcasted_iota(jnp.int32, (tq, LANES), 1)
    o_ref[...] = jnp.where(lane < MLA_V, o[0:tq, :], o[tq:2 * tq, :]).astype(o_ref.dtype)


def _mla_prompt_attn(q, k, v, batch, seq):
    tq = 256
    nq = seq // tq
    return pl.pallas_call(
        _mla_prompt_kernel,
        out_shape=jax.ShapeDtypeStruct((batch * seq, 1024), BF16),
        grid=(batch, 8, nq),
        in_specs=[pl.BlockSpec((tq, 256), lambda b, p, i: (b * nq + i, p)),
                  pl.BlockSpec((seq, 256), lambda b, p, i: (b, p)),
                  pl.BlockSpec((seq, 128), lambda b, p, i: (b, p))],
        out_specs=pl.BlockSpec((tq, 128), lambda b, p, i: (b * nq + i, p)),
        scratch_shapes=[pltpu.VMEM((2 * tq, 1), F32), pltpu.VMEM((2 * tq, 1), F32),
                        pltpu.VMEM((2 * tq, 128), F32)],
        compiler_params=_cparams("parallel", "parallel", "arbitrary"),
    )(q, k, v)


def _mla_sample_kernel(pt_ref, q_ref, latn_ref, wk_ref, wv_ref, kg_ref, gmean_ref, *refs, pps):
    pages = refs[:pps]
    o_ref = refs[pps]
    qt_ref, qpe_ref, m_ref, l_ref, acc_ref = refs[pps + 1:]
    s = pl.program_id(1)
    lane8 = lax.broadcasted_iota(jnp.int32, (8, LANES), 1)

    @pl.when(s == 0)
    def _():
        z = jnp.zeros((8, LANES), F32)
        qg_rows, pe_rows = [], []
        for h in range(16):
            ch = q_ref[:, h * LANES:(h + 1) * LANES]
            nope = jnp.where(lane8 < MLA_NOPE, ch, 0.0)
            if h % 2:
                nope = pltpu.roll(nope, MLA_NOPE, 1)
            qg_rows.append(jnp.concatenate([z] * (h // 2) + [nope] + [z] * (7 - h // 2), axis=1))
            pe_rows.append(pltpu.roll(ch, LANES - MLA_NOPE, 1)[:, 0:MLA_ROPE])
        qg = (jnp.concatenate(qg_rows, axis=0) * kg_ref[...]).astype(BF16)
        qt_ref[...] = _dot_nt(wk_ref[...], qg).astype(BF16)
        qpe_ref[...] = jnp.concatenate(pe_rows, axis=0).astype(BF16)
        m_ref[...] = jnp.full(m_ref.shape, NEG, F32)
        l_ref[...] = jnp.zeros(l_ref.shape, F32)
        acc_ref[...] = jnp.zeros(acc_ref.shape, F32)

    def update(lat, ok):
        ckv = lat[:, 0:MLA_KV_LORA].astype(BF16)
        kpe = lat[:, MLA_KV_LORA:MLA_KV_LORA + MLA_ROPE].astype(BF16)
        e = _dot(ckv, wk_ref[...])
        rs = lax.rsqrt(_dot((e * e).astype(BF16), gmean_ref[...]) + NORM_EPS)
        sc = _dot(ckv, qt_ref[...]) * rs + _dot_nt(kpe, qpe_ref[...])
        if ok is not None:
            sc = jnp.where(ok, sc, NEG)
        m_old = m_ref[...]
        m_new = jnp.maximum(m_old, jnp.max(sc, axis=0, keepdims=True))
        p = jnp.exp(sc - m_new)
        if ok is not None:
            p = jnp.where(ok, p, 0.0)
        alpha = jnp.exp(m_old - m_new)
        m_ref[...] = m_new
        l_ref[...] = alpha * l_ref[...] + jnp.sum(p, axis=0, keepdims=True)
        acc_ref[...] = alpha * acc_ref[...] + _dot_tn(ckv, p.astype(BF16))

    update(jnp.concatenate([pg[...] for pg in pages], axis=0), None)

    @pl.when(s == pl.num_programs(1) - 1)
    def _():
        latn = jnp.concatenate([latn_ref[...], jnp.zeros((LANES - 8, latn_ref.shape[1]), F32)], axis=0)
        t = lax.broadcasted_iota(jnp.int32, (LANES, LANES), 0)
        rho = lax.broadcasted_iota(jnp.int32, (LANES, LANES), 1)
        update(latn, t <= (rho & 7))
        olat = (acc_ref[...] / l_ref[...]).astype(BF16)
        ofull = _dot_tn(olat, wv_ref[...])
        r_head = lax.broadcasted_iota(jnp.int32, (LANES, 1024), 0) >> 3
        c_head = lax.broadcasted_iota(jnp.int32, (LANES, 1024), 1) >> 6
        ofull = jnp.where(r_head == c_head, ofull, 0.0)
        out = ofull[0:8, :]
        for h in range(1, 16):
            out = out + ofull[h * 8:(h + 1) * 8, :]
        o_ref[...] = out


def _mla_sample_attn(q, latn, cache, page_table, w_ukv, kn, pps):
    db, n_pages = page_table.shape
    nsteps = n_pages // pps
    wk, wv = _mla_split_ukv(w_ukv)
    kg = jnp.tile(kn[:MLA_NOPE], 16).reshape(1, 1024)
    gmean = np.zeros((1024, LANES), np.float32)
    for h in range(16):
        gmean[h * 64:(h + 1) * 64, h * 8:(h + 1) * 8] = 1.0 / MLA_NOPE

    def page_spec(t):
        return pl.BlockSpec((None, PAGE, 288), lambda b, s, pt: (pt[b, s * pps + t], 0, 0))

    grid_spec = pltpu.PrefetchScalarGridSpec(
        num_scalar_prefetch=1,
        grid=(db, nsteps),
        in_specs=[pl.BlockSpec((8, 2048), lambda b, s, pt: (b, 0)),
                  pl.BlockSpec((8, 288), lambda b, s, pt: (b, 0)),
                  _full((256, 1024)), _full((256, 1024)), _full((1, 1024)), _full((1024, LANES))]
                 + [page_spec(t) for t in range(pps)],
        out_specs=pl.BlockSpec((8, 1024), lambda b, s, pt: (b, 0)),
        scratch_shapes=[pltpu.VMEM((256, 128), BF16), pltpu.VMEM((128, MLA_ROPE), BF16),
                        pltpu.VMEM((1, 128), F32), pltpu.VMEM((1, 128), F32), pltpu.VMEM((256, 128), F32)],
    )
    return pl.pallas_call(
        functools.partial(_mla_sample_kernel, pps=pps),
        out_shape=jax.ShapeDtypeStruct((db * 8, 1024), F32),
        grid_spec=grid_spec,
        compiler_params=_cparams("parallel", "arbitrary"),
    )(page_table, q, latn, wk.astype(BF16), wv.astype(BF16), kg, jnp.asarray(gmean, BF16),
      *([cache] * pps))


def _mla_layer(xp, xs, cache, page_table, ln_g, w_dqkv, g_q, g_kv, w_uq, w_ukv, qn, kn, w_o,
               batch, seq, past_len):
    g = ln_g.reshape(1, -1)
    w_o_b = w_o.astype(BF16)
    dec = 8
    q, lat = _mla_proj(xp, g, w_dqkv, g_q, g_kv, w_uq, qn, kn, np.arange(seq), 512, BF16)
    k, v = _mla_expand(lat, w_ukv, kn, 512)
    o = _mla_prompt_attn(q, k, v, batch, seq)
    yp = _outproj(xp, o, w_o_b, 512)
    ns = xs.shape[0]
    pos_s = np.tile(past_len + np.arange(dec), ns // dec)
    qs, lats = _mla_proj(xs, g, w_dqkv, g_q, g_kv, w_uq, qn, kn, pos_s, ns, F32)
    os_ = _mla_sample_attn(qs, lats, cache, page_table, w_ukv, kn, min(8, page_table.shape[1]))
    ys = _outproj(xs, os_, w_o_b, ns)
    return yp, ys, lat, lats


def kernel(x_prompt, x_sample, cache_kv_0, cache_kv_1, state_win_1, cache_lat_2, cache_kv_3, page_table, ln1_g, ln2_g, mlp_w1, mlp_w2, moba_w_in_0, moba_qn_0, moba_kn_0, moba_w_o_0, nsa_w_in_1, nsa_qn_1, nsa_kn_1, nsa_cmp_pe_1, nsa_cmp_w1_1, nsa_cmp_b1_1, nsa_cmp_w2_1, nsa_w_o_1, mla_w_dqkv_2, mla_g_q_2, mla_g_kv_2, mla_w_uq_2, mla_w_ukv_2, mla_qn_2, mla_kn_2, mla_w_o_2, moba_w_in_3, moba_qn_3, moba_kn_3, moba_w_o_3):
    batch, seq, d = x_prompt.shape
    db, dec, _ = x_sample.shape
    past = page_table.shape[1] * PAGE
    assert dec == 8 and seq % 512 == 0
    xp = x_prompt.reshape(batch * seq, d)
    xs = x_sample.reshape(db * dec, d)

    def mlp(i, xp, xs):
        w1 = mlp_w1[i].astype(BF16)
        w2 = mlp_w2[i].astype(BF16)
        g = ln2_g[i].reshape(1, d)
        return _mlp(xp, g, w1, w2, 512), _mlp(xs, g, w1, w2, db * dec)

    xp, xs, kv0_p, kv0_s = _moba_layer(xp, xs, cache_kv_0, page_table, ln1_g[0], moba_w_in_0, moba_qn_0,
                                       moba_kn_0, moba_w_o_0, batch, seq, past)
    xp, xs = mlp(0, xp, xs)
    xp, xs, kv1_p, kv1_s = _nsa_layer(xp, xs, cache_kv_1, state_win_1, page_table, ln1_g[1], nsa_w_in_1,
                                      nsa_qn_1, nsa_kn_1, nsa_cmp_pe_1, nsa_cmp_w1_1, nsa_cmp_b1_1,
                                      nsa_cmp_w2_1, nsa_w_o_1, batch, seq, past)
    xp, xs = mlp(1, xp, xs)
    xp, xs, lat_p, lat_s = _mla_layer(xp, xs, cache_lat_2, page_table, ln1_g[2], mla_w_dqkv_2, mla_g_q_2,
                                      mla_g_kv_2, mla_w_uq_2, mla_w_ukv_2, mla_qn_2, mla_kn_2, mla_w_o_2,
                                      batch, seq, past)
    xp, xs = mlp(2, xp, xs)
    xp, xs, kv3_p, kv3_s = _moba_layer(xp, xs, cache_kv_3, page_table, ln1_g[3], moba_w_in_3, moba_qn_3,
                                       moba_kn_3, moba_w_o_3, batch, seq, past)
    xp, xs = mlp(3, xp, xs)

    wb_p = min(NSA_WINDOW, seq)
    win_p = kv1_p[:, 512:768].reshape(batch, seq, 2, NSA_KV_HEADS, HEAD_DIM)[:, seq - wb_p:]
    win_new = kv1_s[:, 512:768].reshape(db, dec, 2, NSA_KV_HEADS, HEAD_DIM)
    win_s = jnp.concatenate([state_win_1, win_new], axis=1)[:, dec:]
    return (xp.reshape(batch, seq, d), xs.reshape(db, dec, d),
            kv0_p.reshape(batch, seq, 2, MOBA_KV_HEADS, HEAD_DIM),
            kv1_p[:, :512].reshape(batch, seq, 4, NSA_KV_HEADS, HEAD_DIM),
            win_p,
            lat_p.reshape(batch, seq, MLA_KV_LORA + MLA_ROPE),
            kv3_p.reshape(batch, seq, 2, MOBA_KV_HEADS, HEAD_DIM),
            kv0_s.reshape(db, dec, 2, MOBA_KV_HEADS, HEAD_DIM),
            kv1_s[:, :512].reshape(db, dec, 4, NSA_KV_HEADS, HEAD_DIM),
            win_s,
            lat_s.reshape(db, dec, MLA_KV_LORA + MLA_ROPE),
            kv3_s.reshape(db, dec, 2, MOBA_KV_HEADS, HEAD_DIM))
```

```python
import functools

import numpy as np
import jax
import jax.numpy as jnp
from jax import lax
from jax.experimental import pallas as pl
from jax.experimental.pallas import tpu as pltpu

F32 = jnp.float32
BF16 = jnp.bfloat16

HEAD_DIM = 64
ROPE_THETA = 10000.0
NORM_EPS = 1e-6
PAGE = 128
MOBA_KV_HEADS = 4
MOBA_BLOCK = 256
MOBA_TOPK = 3
NSA_KV_HEADS = 2
NSA_CMP_LEN = 32
NSA_CMP_STRIDE = 16
NSA_CMP_HIDDEN = 256
NSA_SLC_BLOCK = 64
NSA_SLC_TOPN = 16
NSA_WINDOW = 512
NSA_FORCE_SCORE = 1e9
MLA_Q_LORA = 384
MLA_KV_LORA = 256
MLA_NOPE = 64
MLA_ROPE = 32
MLA_V = 64

LANES = 128
VMEM_LIMIT_BYTES = 56 * 1024 * 1024
NEG = -1e30
TINY = float(np.finfo(np.float32).tiny)


def _cparams(*sem):
    return pltpu.CompilerParams(dimension_semantics=sem, vmem_limit_bytes=VMEM_LIMIT_BYTES)


def _dot(a, b):
    return jnp.dot(a, b, preferred_element_type=F32)


def _dot_nt(a, b):
    return lax.dot_general(a, b, (((1,), (1,)), ((), ())), preferred_element_type=F32)


def _dot_tn(a, b):
    return lax.dot_general(a, b, (((0,), (0,)), ((), ())), preferred_element_type=F32)


def _split_bf16(x):
    hi = x.astype(BF16)
    lo = (x - hi.astype(F32)).astype(BF16)
    return hi, lo


def _full(shape):
    n = len(shape)
    return pl.BlockSpec(shape, lambda *_: (0,) * n)


def _slot_perm(kv_heads, n_heads=16):
    grp = n_heads // kv_heads
    cols = []
    for s in range(n_heads // 2):
        p, r = divmod(s, grp)
        for h in ((2 * p) * grp + r, (2 * p + 1) * grp + r):
            cols.append(h * HEAD_DIM + np.arange(HEAD_DIM))
    return np.concatenate(cols)


def _slot_heads(kv_heads, n_heads=16):
    grp = n_heads // kv_heads
    out = []
    for s in range(n_heads // 2):
        p, r = divmod(s, grp)
        out.append(((2 * p) * grp + r, (2 * p + 1) * grp + r))
    return out


def _group_mean_matrix(groups, width):
    m = np.zeros((width, width), np.float32)
    for lo, hi in groups:
        m[lo:hi, lo:hi] = 1.0 / (hi - lo)
    return m


def _rope_tables(pos, width, segs):
    pos = np.asarray(pos, np.float64)
    cos = np.ones((pos.shape[0], width), np.float64)
    sin = np.zeros((pos.shape[0], width), np.float64)
    up = np.zeros((1, width), np.float32)
    for lo, dim in segs:
        half = dim // 2
        inv = ROPE_THETA ** (-np.arange(half, dtype=np.float64) / half)
        ang = pos[:, None] * inv[None, :]
        cos[:, lo:lo + half] = np.cos(ang)
        cos[:, lo + half:lo + dim] = np.cos(ang)
        sin[:, lo:lo + half] = -np.sin(ang)
        sin[:, lo + half:lo + dim] = np.sin(ang)
        up[:, lo:lo + half] = 1.0
    return cos.astype(np.float32), sin.astype(np.float32), up


def _head64_segs(width):
    return [(lo, HEAD_DIM) for lo in range(0, width, HEAD_DIM)]


def _rms_rows(x, g):
    ms = jnp.mean(x * x, axis=-1, keepdims=True)
    return x * lax.rsqrt(ms + NORM_EPS) * g


def _group_norm(a, gm, gain):
    ms = _dot((a * a).astype(BF16), gm)
    return a * lax.rsqrt(ms + NORM_EPS) * gain


def _rope(y, cos, sin, up, shift):
    w = y.shape[-1]
    hi = pltpu.roll(y, w - shift, 1)
    lo = pltpu.roll(y, shift, 1)
    partner = jnp.where(up > 0.5, hi, lo)
    return y * cos + partner * sin


def _gelu_tanh(x):
    return 0.5 * x * (1.0 + jnp.tanh(0.7978845608028654 * (x + 0.044715 * (x * x * x))))


def _outproj_kernel(x_ref, o_ref, w_ref, y_ref):
    y_ref[...] = x_ref[...] + _dot(o_ref[...].astype(BF16), w_ref[...])


def _outproj(x, o, w, tn):
    n, d = x.shape
    k = o.shape[1]
    return pl.pallas_call(
        _outproj_kernel,
        out_shape=jax.ShapeDtypeStruct((n, d), F32),
        grid=(n // tn,),
        in_specs=[pl.BlockSpec((tn, d), lambda i: (i, 0)),
                  pl.BlockSpec((tn, k), lambda i: (i, 0)),
                  _full((k, d))],
        out_specs=pl.BlockSpec((tn, d), lambda i: (i, 0)),
        compiler_params=_cparams("parallel"),
    )(x, o, w)


def _mlp_kernel(x_ref, g_ref, w1_ref, w2_ref, y_ref, *, ff_chunk):
    x = x_ref[...]
    h = _rms_rows(x, g_ref[...]).astype(BF16)
    acc = x
    for c in range(w1_ref.shape[1] // ff_chunk):
        u = _dot(h, w1_ref[:, c * ff_chunk:(c + 1) * ff_chunk])
        u = jnp.maximum(u, 0.0)
        acc = acc + _dot((u * u).astype(BF16), w2_ref[c * ff_chunk:(c + 1) * ff_chunk, :])
    y_ref[...] = acc


def _mlp(x, g, w1, w2, tn):
    n, d = x.shape
    ff = w1.shape[1]
    return pl.pallas_call(
        functools.partial(_mlp_kernel, ff_chunk=1024),
        out_shape=jax.ShapeDtypeStruct((n, d), F32),
        grid=(n // tn,),
        in_specs=[pl.BlockSpec((tn, d), lambda i: (i, 0)),
                  _full((1, d)), _full((d, ff)), _full((ff, d))],
        out_specs=pl.BlockSpec((tn, d), lambda i: (i, 0)),
        compiler_params=_cparams("parallel"),
    )(x, g, w1, w2)


def _moba_proj_kernel(x_ref, g_ref, w_ref, gm_ref, c_ref, cos_ref, sin_ref, q_ref, kv_ref, kvb_ref):
    h = _rms_rows(x_ref[...], g_ref[...]).astype(BF16)
    gm = gm_ref[...]
    cos, sin = cos_ref[...], sin_ref[...]
    qg, kg, up = c_ref[0:1, :], c_ref[1:2, :], c_ref[2:3, :]
    for c in range(4):
        a = _dot(h, w_ref[:, c * 256:(c + 1) * 256])
        y = _rope(_group_norm(a, gm, qg), cos, sin, up, HEAD_DIM // 2)
        q_ref[:, c * 256:(c + 1) * 256] = (y * HEAD_DIM ** -0.5).astype(q_ref.dtype)
    a = _dot(h, w_ref[:, 1024:1280])
    k = _rope(_group_norm(a, gm, kg), cos, sin, up, HEAD_DIM // 2)
    v = _dot(h, w_ref[:, 1280:1536])
    kv_ref[:, 0:256] = k
    kv_ref[:, 256:512] = v
    kvb_ref[:, 0:256] = k.astype(BF16)
    kvb_ref[:, 256:512] = v.astype(BF16)


def _moba_proj(x, g, w, qn, kn, pos, tn, qdtype):
    n, d = x.shape
    period = pos.shape[0]
    cos, sin, up = _rope_tables(pos, 256, _head64_segs(256))
    consts = np.zeros((8, 256), np.float32)
    consts[2] = up[0]
    consts = jnp.asarray(consts).at[0].set(jnp.tile(qn, 4)).at[1].set(jnp.tile(kn, 4))
    gm = jnp.asarray(_group_mean_matrix([(lo, lo + 64) for lo in range(0, 256, 64)], 256), BF16)
    nper = period // tn
    tab = pl.BlockSpec((tn, 256), lambda i: (i % nper, 0))
    return pl.pallas_call(
        _moba_proj_kernel,
        out_shape=(jax.ShapeDtypeStruct((n, 1024), qdtype),
                   jax.ShapeDtypeStruct((n, 512), F32),
                   jax.ShapeDtypeStruct((n, 512), BF16)),
        grid=(n // tn,),
        in_specs=[pl.BlockSpec((tn, d), lambda i: (i, 0)), _full((1, d)), _full(w.shape),
                  _full((256, 256)), _full((8, 256)), tab, tab],
        out_specs=(pl.BlockSpec((tn, 1024), lambda i: (i, 0)),
                   pl.BlockSpec((tn, 512), lambda i: (i, 0)),
                   pl.BlockSpec((tn, 512), lambda i: (i, 0))),
        compiler_params=_cparams("parallel"),
    )(x, g, w, gm, consts, jnp.asarray(cos), jnp.asarray(sin))


def _topk_mask(score, k, n_valid):
    rows, w = score.shape
    lane = lax.broadcasted_iota(jnp.int32, (rows, w), 1)
    rank = jnp.zeros((rows, w), jnp.int32)
    for j in range(n_valid):
        sj = score[:, j:j + 1]
        ahead = (sj > score) | ((sj == score) & (lane > j))
        rank = rank + ahead.astype(jnp.int32)
    return rank < k


def _topk_mask_t(score_t, k, n_valid):
    nc, n = score_t.shape
    cand = lax.broadcasted_iota(jnp.int32, (nc, n), 0)
    rank = jnp.zeros((nc, n), jnp.int32)
    for j in range(n_valid):
        sj = score_t[j:j + 1, :]
        ahead = (sj > score_t) | ((sj == score_t) & (cand > j))
        rank = rank + ahead.astype(jnp.int32)
    return rank < k


def _bias_rows(sel_t):
    nc, n = sel_t.shape
    bias_t = jnp.where(sel_t, 0.0, NEG)
    if nc < LANES:
        bias_t = jnp.concatenate([bias_t, jnp.zeros((LANES - nc, n), F32)], axis=0)
    return bias_t.T


def _rep(x, n):
    return x if n == 1 else jnp.concatenate([x] * n, axis=1)


def _flash_first(s, pv, m_ref, l_ref, acc_ref):
    m = jnp.max(s, axis=-1, keepdims=True)
    p = jnp.exp(s - m)
    m_ref[...] = jnp.broadcast_to(m, m_ref.shape)
    l_ref[...] = jnp.broadcast_to(jnp.sum(p, axis=-1, keepdims=True), l_ref.shape)
    acc_ref[...] = pv(p.astype(BF16))


def _flash_masked(s, ok, pv, m_ref, l_ref, acc_ref):
    s = jnp.where(ok, s, NEG)
    m_old = m_ref[...]
    m_new = jnp.maximum(m_old, jnp.max(s, axis=-1, keepdims=True))
    p = jnp.where(ok, jnp.exp(s - _rep(m_new, s.shape[1] // LANES)), 0.0)
    alpha = jnp.exp(m_old - m_new)
    m_ref[...] = m_new
    l_ref[...] = alpha * l_ref[...] + jnp.sum(p, axis=-1, keepdims=True)
    acc_ref[...] = _rep(alpha, acc_ref.shape[1] // LANES) * acc_ref[...] + pv(p.astype(BF16))


def _flash_next(s, pv, m_ref, l_ref, acc_ref):
    m_old = m_ref[...]
    m_new = jnp.maximum(m_old, jnp.max(s, axis=-1, keepdims=True))
    p = jnp.exp(s - _rep(m_new, s.shape[1] // LANES))
    alpha = jnp.exp(m_old - m_new)
    m_ref[...] = m_new
    l_ref[...] = alpha * l_ref[...] + jnp.sum(p, axis=-1, keepdims=True)
    acc_ref[...] = _rep(alpha, acc_ref.shape[1] // LANES) * acc_ref[...] + pv(p.astype(BF16))


def _slot_rows(q_ref, n_slots):
    tq = q_ref.shape[0]
    lane = lax.broadcasted_iota(jnp.int32, (tq, LANES), 1)
    ev, od = [], []
    for s in range(n_slots):
        qs = q_ref[:, s * LANES:(s + 1) * LANES]
        ev.append(jnp.where(lane < HEAD_DIM, qs, jnp.zeros_like(qs)))
        od.append(jnp.where(lane >= HEAD_DIM, qs, jnp.zeros_like(qs)))
    return jnp.concatenate(ev + od, axis=0)


def _slot_out(o, n_slots, tq):
    lane = lax.broadcasted_iota(jnp.int32, (tq, LANES), 1)
    outs = []
    for s in range(n_slots):
        e = o[s * tq:(s + 1) * tq, :]
        d = o[(n_slots + s) * tq:(n_slots + s + 1) * tq, :]
        outs.append(jnp.where(lane < HEAD_DIM, e, d))
    return jnp.concatenate(outs, axis=1)


def _moba_prompt_kernel(q_ref, k_ref, v_ref, bm_ref, o_ref, m_ref, l_ref, acc_ref):
    i = pl.program_id(2)
    tq = q_ref.shape[0]
    nb = k_ref.shape[0] // MOBA_BLOCK
    rows = 8 * tq
    qa = _slot_rows(q_ref, 4)
    bhi, blo = _split_bf16(bm_ref[...])
    gate_t = _dot_nt(bhi, qa) + _dot_nt(blo, qa)
    blk = lax.broadcasted_iota(jnp.int32, (nb, rows), 0)
    gate_t = jnp.where(blk < i, gate_t, -jnp.inf)
    sel_t = _topk_mask_t(gate_t, MOBA_TOPK, nb) & (blk < i)
    qaug = jnp.concatenate([qa, _bias_rows(sel_t).astype(BF16)], axis=1)

    kd = k_ref[pl.ds(pl.multiple_of(i * MOBA_BLOCK, MOBA_BLOCK), MOBA_BLOCK), :]
    vd = v_ref[pl.ds(pl.multiple_of(i * MOBA_BLOCK, MOBA_BLOCK), MOBA_BLOCK), :]
    qi = lax.broadcasted_iota(jnp.int32, (rows, MOBA_BLOCK), 0) & (tq - 1)
    ki = lax.broadcasted_iota(jnp.int32, (rows, MOBA_BLOCK), 1)
    _flash_first(jnp.where(ki <= qi, _dot_nt(qa, kd), NEG), lambda p: _dot(p, vd), m_ref, l_ref, acc_ref)

    lane = lax.broadcasted_iota(jnp.int32, (MOBA_BLOCK, LANES), 1)
    for j in range(nb - 1):
        @pl.when(j < i)
        def _(j=j):
            onehot = jnp.where(lane == j, 1.0, 0.0).astype(BF16)
            kj = jnp.concatenate([k_ref[j * MOBA_BLOCK:(j + 1) * MOBA_BLOCK, :], onehot], axis=1)
            vj = v_ref[j * MOBA_BLOCK:(j + 1) * MOBA_BLOCK, :]
            _flash_next(_dot_nt(qaug, kj), lambda p: _dot(p, vj), m_ref, l_ref, acc_ref)

    o = acc_ref[...] / l_ref[...]
    o_ref[...] = _slot_out(o, 4, tq).astype(o_ref.dtype)


def _moba_prompt_attn(q, kvb, bm, batch, seq):
    tq = MOBA_BLOCK
    nq = seq // tq
    nb = seq // MOBA_BLOCK
    return pl.pallas_call(
        _moba_prompt_kernel,
        out_shape=jax.ShapeDtypeStruct((batch * seq, 1024), BF16),
        grid=(batch, 2, nq),
        in_specs=[pl.BlockSpec((tq, 512), lambda b, p, i: (b * nq + i, p)),
                  pl.BlockSpec((seq, 128), lambda b, p, i: (b, p)),
                  pl.BlockSpec((seq, 128), lambda b, p, i: (b, 2 + p)),
                  pl.BlockSpec((nb, 128), lambda b, p, i: (b, p))],
        out_specs=pl.BlockSpec((tq, 512), lambda b, p, i: (b * nq + i, p)),
        scratch_shapes=[pltpu.VMEM((8 * tq, 128), F32), pltpu.VMEM((8 * tq, 128), F32),
                        pltpu.VMEM((8 * tq, 128), F32)],
        compiler_params=_cparams("parallel", "parallel", "arbitrary"),
    )(q, kvb, kvb, bm)


def _bmean_kernel(k_ref, o_ref):
    nb = o_ref.shape[0]
    k = k_ref[...].reshape(nb, MOBA_BLOCK, k_ref.shape[1])
    o_ref[...] = jnp.sum(k, axis=1) * (1.0 / MOBA_BLOCK)


def _moba_bmean(kv, batch, seq):
    nb = seq // MOBA_BLOCK
    return pl.pallas_call(
        _bmean_kernel,
        out_shape=jax.ShapeDtypeStruct((batch * nb, 256), F32),
        grid=(batch,),
        in_specs=[pl.BlockSpec((seq, 256), lambda b: (b, 0))],
        out_specs=pl.BlockSpec((nb, 256), lambda b: (b, 0)),
        compiler_params=_cparams("parallel"),
    )(kv)


def _moba_sample_kernel(pt_ref, q_ref, kvn_ref, *refs, bps, nb_past):
    pages = refs[:2 * bps]
    o_ref = refs[2 * bps]
    qa_ref, bm_ref, mst_ref, lst_ref, oacc_ref = refs[2 * bps + 1:]
    s = pl.program_id(1)
    rows = 128
    lane128 = lax.broadcasted_iota(jnp.int32, (8, LANES), 1)

    @pl.when(s == 0)
    def _():
        q = q_ref[...]
        z = jnp.zeros((8, LANES), F32)
        ev, od = [], []
        for sl in range(8):
            qs = q[:, sl * LANES:(sl + 1) * LANES]
            e = jnp.where(lane128 < HEAD_DIM, qs, 0.0)
            d = jnp.where(lane128 >= HEAD_DIM, qs, 0.0)
            if sl // 4 == 0:
                ev.append(jnp.concatenate([e, z], axis=1))
                od.append(jnp.concatenate([d, z], axis=1))
            else:
                ev.append(jnp.concatenate([z, e], axis=1))
                od.append(jnp.concatenate([z, d], axis=1))
        qa_ref[...] = jnp.concatenate(ev + od, axis=0).astype(BF16)
        bm_ref[...] = jnp.zeros_like(bm_ref)
        mst_ref[...] = jnp.zeros_like(mst_ref)
        lst_ref[...] = jnp.zeros_like(lst_ref)

    qa = qa_ref[...]
    col = lax.broadcasted_iota(jnp.int32, (rows, LANES), 1)
    colb = lax.broadcasted_iota(jnp.int32, (256, LANES), 1)
    for t in range(bps):
        jg = s * bps + t
        kt = jnp.concatenate([pages[2 * t][0:256, :], pages[2 * t + 1][0:256, :]], axis=1)
        vt = jnp.concatenate([pages[2 * t][256:512, :], pages[2 * t + 1][256:512, :]], axis=1)
        bcol = jnp.sum(kt, axis=1, keepdims=True) * (1.0 / MOBA_BLOCK)
        bm_ref[...] = jnp.where(colb == jg, bcol, bm_ref[...])
        sc = _dot(qa, kt.astype(BF16))
        m = jnp.max(sc, axis=-1, keepdims=True)
        p = jnp.exp(sc - m)
        oacc_ref[jg] = _dot_nt(p.astype(BF16), vt.astype(BF16))
        mst_ref[...] = jnp.where(col == jg, m, mst_ref[...])
        lst_ref[...] = jnp.where(col == jg, jnp.sum(p, axis=-1, keepdims=True), lst_ref[...])

    @pl.when(s == pl.num_programs(1) - 1)
    def _():
        kn = kvn_ref[0:256, :].astype(BF16)
        vn = kvn_ref[256:512, :].astype(BF16)
        sn = _dot(qa, kn)
        qi = lax.broadcasted_iota(jnp.int32, (rows, LANES), 0) & 7
        okn = col <= qi
        sn = jnp.where(okn, sn, NEG)
        mn = jnp.max(sn, axis=-1, keepdims=True)
        pn = jnp.where(okn, jnp.exp(sn - mn), 0.0)
        ln = jnp.sum(pn, axis=-1, keepdims=True)
        on = _dot_nt(pn.astype(BF16), vn)
        bhi, blo = _split_bf16(bm_ref[...])
        gate = _dot(qa, bhi) + _dot(qa, blo)
        past = col < nb_past
        gate = jnp.where(past, gate, -jnp.inf)
        sel = _topk_mask(gate, MOBA_TOPK, nb_past) & past
        mst = mst_ref[...]
        mstar = jnp.maximum(jnp.max(jnp.where(sel, mst, NEG), axis=-1, keepdims=True), mn)
        w = jnp.where(sel, jnp.exp(mst - mstar), 0.0)
        wn = jnp.exp(mn - mstar)
        den = jnp.sum(w * lst_ref[...], axis=-1, keepdims=True) + wn * ln
        acc = wn * on
        for j in range(nb_past):
            acc = acc + w[:, j:j + 1] * oacc_ref[j]
        o = acc / den
        outs = []
        for sl in range(8):
            c = (sl // 4) * LANES
            e = o[sl * 8:(sl + 1) * 8, c:c + LANES]
            d = o[(8 + sl) * 8:(9 + sl) * 8, c:c + LANES]
            outs.append(jnp.where(lane128 < HEAD_DIM, e, d))
        o_ref[...] = jnp.concatenate(outs, axis=1)


def _pages_t(cache):
    return jnp.swapaxes(cache.reshape(cache.shape[0], PAGE, -1), 1, 2)


def _rows_t(x, db):
    xt = jnp.swapaxes(x.reshape(db, x.shape[0] // db, x.shape[1]), 1, 2)
    return jnp.pad(xt, ((0, 0), (0, 0), (0, LANES - xt.shape[2])))


def _moba_sample_attn(q, kvn_t, cache_t, page_table, bps):
    db, n_pages = page_table.shape
    nb_past = n_pages * PAGE // MOBA_BLOCK
    assert n_pages % (2 * bps) == 0 and nb_past < LANES
    nsteps = n_pages // (2 * bps)

    def page_spec(t):
        return pl.BlockSpec((None, 512, PAGE), lambda b, s, pt: (pt[b, s * 2 * bps + t], 0, 0))

    grid_spec = pltpu.PrefetchScalarGridSpec(
        num_scalar_prefetch=1,
        grid=(db, nsteps),
        in_specs=[pl.BlockSpec((8, 1024), lambda b, s, pt: (b, 0)),
                  pl.BlockSpec((None, 512, LANES), lambda b, s, pt: (b, 0, 0))]
                 + [page_spec(t) for t in range(2 * bps)],
        out_specs=pl.BlockSpec((8, 1024), lambda b, s, pt: (b, 0)),
        scratch_shapes=[pltpu.VMEM((128, 256), BF16), pltpu.VMEM((256, LANES), F32),
                        pltpu.VMEM((128, LANES), F32), pltpu.VMEM((128, LANES), F32),
                        pltpu.VMEM((nb_past, 128, 256), F32)],
    )
    return pl.pallas_call(
        functools.partial(_moba_sample_kernel, bps=bps, nb_past=nb_past),
        out_shape=jax.ShapeDtypeStruct((db * 8, 1024), F32),
        grid_spec=grid_spec,
        compiler_params=_cparams("parallel", "arbitrary"),
    )(page_table, q, kvn_t, *([cache_t] * (2 * bps)))


def _moba_layer(xp, xs, cache, page_table, ln_g, w_in, qn, kn, w_o, batch, seq, past_len):
    perm = _slot_perm(MOBA_KV_HEADS)
    w_in_p = jnp.concatenate([w_in[:, :1024][:, perm], w_in[:, 1024:]], axis=1).astype(BF16)
    w_o_p = w_o[perm, :].astype(BF16)
    g = ln_g.reshape(1, -1)
    dec = 8
    q, kv, kvb = _moba_proj(xp, g, w_in_p, qn, kn, np.arange(seq), 512, BF16)
    bm = _moba_bmean(kv, batch, seq)
    o = _moba_prompt_attn(q, kvb, bm, batch, seq)
    yp = _outproj(xp, o, w_o_p, 512)
    ns = xs.shape[0]
    pos_s = np.tile(past_len + np.arange(dec), ns // dec)
    qs, kvs, _ = _moba_proj(xs, g, w_in_p, qn, kn, pos_s, ns, F32)
    db, n_pages = page_table.shape
    bps = 4 if n_pages % 8 == 0 else 2
    os_ = _moba_sample_attn(qs, _rows_t(kvs, db), _pages_t(cache), page_table, bps)
    ys = _outproj(xs, os_, w_o_p, ns)
    return yp, ys, kv, kvs


def _nsa_proj_kernel(x_ref, g_ref, w_ref, gm_ref, c_ref, cos_ref, sin_ref,
                     q_ref, kv_ref, kvb_ref, gate_ref):
    h = _rms_rows(x_ref[...], g_ref[...]).astype(BF16)
    gm = gm_ref[...]
    cos, sin = cos_ref[...], sin_ref[...]
    qg, up = c_ref[0:1, :], c_ref[4:5, :]
    for c in range(4):
        a = _dot(h, w_ref[:, c * 256:(c + 1) * 256])
        y = _rope(_group_norm(a, gm, qg), cos, sin, up, HEAD_DIM // 2)
        q_ref[:, c * 256:(c + 1) * 256] = (y * HEAD_DIM ** -0.5).astype(q_ref.dtype)
    gm1 = gm_ref[0:128, 0:128]
    for br in range(3):
        a = _dot(h, w_ref[:, 1024 + br * 256:1024 + (br + 1) * 256])
        k = _rope(_group_norm(a[:, 0:128], gm1, c_ref[1 + br:2 + br, 0:128]),
                  cos[:, 0:128], sin[:, 0:128], up[:, 0:128], HEAD_DIM // 2)
        v = a[:, 128:256]
        kv_ref[:, br * 256:br * 256 + 128] = k
        kv_ref[:, br * 256 + 128:(br + 1) * 256] = v
        kvb_ref[:, br * 256:br * 256 + 128] = k.astype(BF16)
        kvb_ref[:, br * 256 + 128:(br + 1) * 256] = v.astype(BF16)
    a = _dot(h, w_ref[:, 1792:1920])
    gate_ref[...] = 1.0 / (1.0 + jnp.exp(-a))


def _nsa_proj(x, g, w, qn, kn, pos, tn, qdtype):
    n, d = x.shape
    period = pos.shape[0]
    cos, sin, up = _rope_tables(pos, 256, _head64_segs(256))
    consts = np.zeros((8, 256), np.float32)
    consts[4] = up[0]
    consts = jnp.asarray(consts).at[0].set(jnp.tile(qn, 4))
    for br in range(3):
        consts = consts.at[1 + br].set(jnp.tile(kn[br], 4))
    gm = jnp.asarray(_group_mean_matrix([(lo, lo + 64) for lo in range(0, 256, 64)], 256), BF16)
    nper = period // tn
    tab = pl.BlockSpec((tn, 256), lambda i: (i % nper, 0))
    row = lambda wd: pl.BlockSpec((tn, wd), lambda i: (i, 0))
    return pl.pallas_call(
        _nsa_proj_kernel,
        out_shape=(jax.ShapeDtypeStruct((n, 1024), qdtype),
                   jax.ShapeDtypeStruct((n, 768), F32),
                   jax.ShapeDtypeStruct((n, 768), BF16),
                   jax.ShapeDtypeStruct((n, 128), F32)),
        grid=(n // tn,),
        in_specs=[row(d), _full((1, d)), _full(w.shape), _full((256, 256)), _full((8, 256)), tab, tab],
        out_specs=(row(1024), row(768), row(768), row(128)),
        compiler_params=_cparams("parallel"),
    )(x, g, w, gm, consts, jnp.asarray(cos), jnp.asarray(sin))


def _nsa_cmp_weights(cmp_pe, cmp_w1, cmp_w2):
    wbig, pe_rows, w2blk = [], [], []
    for i in range(2):
        w1 = cmp_w1[i].reshape(2, 16, HEAD_DIM, NSA_CMP_HIDDEN)
        z = jnp.zeros_like(w1)
        per_g = []
        for g in range(2):
            blk = jnp.stack([w1 if gp == g else z for gp in range(2)], axis=2)
            blk = blk.transpose(1, 2, 3, 0, 4).reshape(2048, 2 * NSA_CMP_HIDDEN)
            per_g.append(blk)
        wbig.append(jnp.concatenate(per_g, axis=1).astype(BF16))
        pe = cmp_pe[i].reshape(2, 16, 1, HEAD_DIM)
        pe = jnp.broadcast_to(pe, (2, 16, 2, HEAD_DIM)).reshape(2, 2048)
        pe_rows.append(jnp.concatenate([pe, jnp.zeros((6, 2048), F32)], axis=0))
        z2 = jnp.zeros_like(cmp_w2[i])
        w2blk.append(jnp.concatenate([jnp.concatenate([cmp_w2[i], z2], axis=1),
                                      jnp.concatenate([z2, cmp_w2[i]], axis=1)], axis=0).astype(BF16))
    return wbig, pe_rows, w2blk


def _cmp_hidden(pre_a, pre_b, pc, b1):
    hs = []
    for g in range(2):
        c = pc[0:1, g * 512:g * 512 + 256] + pc[1:2, g * 512 + 256:(g + 1) * 512] + b1
        hs.append(_gelu_tanh(pre_a[:, g * 512:g * 512 + 256] + pre_b[:, g * 512 + 256:(g + 1) * 512] + c))
    return jnp.concatenate(hs, axis=1)


def _nsa_cmp_prompt_kernel(k_ref, v_ref, wk_ref, wv_ref, pek_ref, pev_ref, b1_ref, w2k_ref, w2v_ref, o_ref):
    nch = o_ref.shape[0]
    for i, (r_ref, w_ref, pe_ref, w2_ref) in enumerate(((k_ref, wk_ref, pek_ref, w2k_ref),
                                                         (v_ref, wv_ref, pev_ref, w2v_ref))):
        x = jnp.concatenate([r_ref[pl.ds(l, nch, stride=NSA_CMP_STRIDE), :] for l in range(NSA_CMP_STRIDE)],
                            axis=1).astype(BF16)
        pre = _dot(x, w_ref[...])
        pc = _dot(pe_ref[...].astype(BF16), w_ref[...])
        nxt = pltpu.roll(pre, nch - 1, 0)
        hid = _cmp_hidden(pre, nxt, pc, b1_ref[i:i + 1, :])
        o_ref[:, i * 128:(i + 1) * 128] = _dot(hid.astype(BF16), w2_ref[...])


def _nsa_cmp_prompt(kv, wbig, pe_rows, b1, w2blk, batch, seq):
    nch = seq // NSA_CMP_STRIDE
    return pl.pallas_call(
        _nsa_cmp_prompt_kernel,
        out_shape=jax.ShapeDtypeStruct((batch * nch, 256), F32),
        grid=(batch,),
        in_specs=[pl.BlockSpec((seq, 128), lambda b: (b, 0)), pl.BlockSpec((seq, 128), lambda b: (b, 1)),
                  _full((2048, 1024)), _full((2048, 1024)), _full((8, 2048)), _full((8, 2048)),
                  _full((8, 256)), _full((512, 128)), _full((512, 128))],
        out_specs=pl.BlockSpec((nch, 256), lambda b: (b, 0)),
        compiler_params=_cparams("parallel"),
    )(kv, kv, wbig[0], wbig[1], pe_rows[0], pe_rows[1], b1, w2blk[0], w2blk[1])


def _flash_step(s, ok, v, m_ref, l_ref, acc_ref):
    s = jnp.where(ok, s, NEG)
    m_old = m_ref[...]
    m_new = jnp.maximum(m_old, jnp.max(s, axis=-1, keepdims=True))
    p = jnp.where(ok, jnp.exp(s - m_new), 0.0)
    alpha = jnp.exp(m_old - m_new)
    m_ref[...] = m_new
    l_ref[...] = alpha * l_ref[...] + jnp.sum(p, axis=-1, keepdims=True)
    acc_ref[...] = alpha * acc_ref[...] + _dot(p.astype(BF16), v)


def _flash_init(m_ref, l_ref, acc_ref):
    m_ref[...] = jnp.full(m_ref.shape, NEG, F32)
    l_ref[...] = jnp.zeros(l_ref.shape, F32)
    acc_ref[...] = jnp.zeros(acc_ref.shape, F32)


def _nsa_prompt_kernel(q_ref, ks_ref, vs_ref, kw_ref, vw_ref, kvc_ref, gate_ref, ovl_ref, e_ref,
                       o_ref, m_ref, l_ref, acc_ref, os_ref):
    i = pl.program_id(1)
    tq = q_ref.shape[0]
    rows = 16 * tq
    tk = 256
    q0 = i * tq
    qa = _slot_rows(q_ref, 8)
    ntok = kvc_ref.shape[0]

    kc = kvc_ref[:, 0:128].astype(BF16)
    vc = kvc_ref[:, 128:256].astype(BF16)
    qpos_c = q0 + (lax.broadcasted_iota(jnp.int32, (rows, ntok), 0) & (tq - 1))
    tok = lax.broadcasted_iota(jnp.int32, (rows, ntok), 1)
    ok = tok * NSA_CMP_STRIDE + (NSA_CMP_LEN - 1) <= qpos_c
    sc = jnp.where(ok, _dot_nt(qa, kc), NEG)
    mc = jnp.max(sc, axis=-1, keepdims=True)
    ec = jnp.where(ok, jnp.exp(sc - mc), 0.0)
    pc = ec / jnp.maximum(jnp.sum(ec, axis=-1, keepdims=True), TINY)
    o_c = _dot(pc.astype(BF16), vc)

    pg = jnp.sum(pc.reshape(2, 8, tq, ntok), axis=1).reshape(2 * tq, ntok)
    phi, plo = _split_bf16(pg)
    nsb = ks_ref.shape[0] // NSA_SLC_BLOCK
    imp_t = (_dot_nt(ovl_ref[...], phi) + _dot_nt(ovl_ref[...], plo))[0:nsb, :]
    qp2 = q0 + (lax.broadcasted_iota(jnp.int32, (nsb, 2 * tq), 1) & (tq - 1))
    own = qp2 >> 6
    jb = lax.broadcasted_iota(jnp.int32, (nsb, 2 * tq), 0)
    allowed = jb <= own
    forced = (jb == 0) | (jb == own) | (jb == own - 1)
    imp_t = jnp.where(forced, NSA_FORCE_SCORE, imp_t)
    imp_t = jnp.where(allowed, imp_t, -jnp.inf)
    sel_t = _topk_mask_t(imp_t, NSA_SLC_TOPN, nsb) & allowed
    bias = _bias_rows(sel_t).astype(BF16)
    qaug = jnp.concatenate([qa, jnp.concatenate([bias[0:tq]] * 8 + [bias[tq:2 * tq]] * 8, axis=0)], axis=1)

    qp_r = q0 + (lax.broadcasted_iota(jnp.int32, (rows, tk), 0) & (tq - 1))
    klr = lax.broadcasted_iota(jnp.int32, (rows, tk), 1)
    jd = (q0 + tq - 1) // tk
    offd = pl.multiple_of(jd * tk, tk)

    def slc_scores(off):
        kj = jnp.concatenate([ks_ref[pl.ds(off, tk), :], e_ref[pl.ds(off, tk), :]], axis=1)
        return _dot_nt(qaug, kj)

    _flash_first(jnp.where(offd + klr <= qp_r, slc_scores(offd), NEG),
                 lambda p: _dot(p, vs_ref[pl.ds(offd, tk), :]), m_ref, l_ref, acc_ref)

    def slc_body(j, carry):
        off = pl.multiple_of(j * tk, tk)
        _flash_next(slc_scores(off), lambda p: _dot(p, vs_ref[pl.ds(off, tk), :]), m_ref, l_ref, acc_ref)
        return carry

    lax.fori_loop(0, jd, slc_body, 0)
    os_ref[...] = acc_ref[...] / l_ref[...]

    def win_scores(off):
        dist = qp_r - (off + klr)
        return jnp.where((dist >= 0) & (dist < NSA_WINDOW), _dot_nt(qa, kw_ref[pl.ds(off, tk), :]), NEG)

    _flash_first(win_scores(offd), lambda p: _dot(p, vw_ref[pl.ds(offd, tk), :]), m_ref, l_ref, acc_ref)

    def win_body(j, carry):
        off = pl.multiple_of(j * tk, tk)
        _flash_next(win_scores(off), lambda p: _dot(p, vw_ref[pl.ds(off, tk), :]), m_ref, l_ref, acc_ref)
        return carry

    lax.fori_loop(jnp.maximum(q0 - (NSA_WINDOW - 1), 0) // tk, jd, win_body, 0)
    o_w = acc_ref[...] / l_ref[...]
    o_s = os_ref[...]

    gate = gate_ref[...]
    outs = []
    for h in range(16):
        r0 = h * tq
        outs.append(gate[:, 3 * h:3 * h + 1] * o_c[r0:r0 + tq, :]
                    + gate[:, 3 * h + 1:3 * h + 2] * o_s[r0:r0 + tq, :]
                    + gate[:, 3 * h + 2:3 * h + 3] * o_w[r0:r0 + tq, :])
    o_ref[...] = _slot_out(jnp.concatenate(outs, axis=0), 8, tq).astype(o_ref.dtype)


def _nsa_overlap(n_tok, n_blk, tok_shift, rows, cols):
    ovl = np.zeros((rows, cols), np.float32)
    t = np.arange(n_tok)[:, None] * NSA_CMP_STRIDE
    b = np.arange(n_blk)[None, :] * NSA_SLC_BLOCK
    ovl[tok_shift:tok_shift + n_tok, :n_blk] = ((t < b + NSA_SLC_BLOCK) & (t + NSA_CMP_LEN > b))
    return ovl


def _nsa_prompt_attn(q, kvb, kvc, gate, batch, seq):
    tq, tk = 128, 256
    nq = seq // tq
    nch = seq // NSA_CMP_STRIDE
    n_tok = (seq - NSA_CMP_LEN) // NSA_CMP_STRIDE + 1
    nsb = seq // NSA_SLC_BLOCK
    assert nch == LANES and nsb <= LANES
    ovl = jnp.asarray(_nsa_overlap(n_tok, nsb, 0, nch, LANES).T, BF16)
    e = np.zeros((seq, LANES), np.float32)
    e[np.arange(seq), np.arange(seq) // NSA_SLC_BLOCK] = 1.0
    e = jnp.asarray(e, BF16)
    seqcol = lambda c: pl.BlockSpec((seq, 128), lambda b, i: (b, c))
    return pl.pallas_call(
        _nsa_prompt_kernel,
        out_shape=jax.ShapeDtypeStruct((batch * seq, 1024), BF16),
        grid=(batch, nq),
        in_specs=[pl.BlockSpec((tq, 1024), lambda b, i: (b * nq + i, 0)),
                  seqcol(2), seqcol(3), seqcol(4), seqcol(5),
                  pl.BlockSpec((nch, 256), lambda b, i: (b, 0)),
                  pl.BlockSpec((tq, 128), lambda b, i: (b * nq + i, 0)),
                  _full((nch, LANES)), _full(e.shape)],
        out_specs=pl.BlockSpec((tq, 1024), lambda b, i: (b * nq + i, 0)),
        scratch_shapes=[pltpu.VMEM((16 * tq, 128), F32), pltpu.VMEM((16 * tq, 128), F32),
                        pltpu.VMEM((16 * tq, 128), F32), pltpu.VMEM((16 * tq, 128), F32)],
        compiler_params=_cparams("parallel", "arbitrary"),
    )(q, kvb, kvb, kvb, kvb, kvc, gate, ovl, e)


def _topk_mask_iter(score, k):
    rows, w = score.shape
    lane = lax.broadcasted_iota(jnp.int32, (rows, w), 1)
    taken = jnp.zeros((rows, w), jnp.int32)
    for _ in range(k):
        free = taken == 0
        cur = jnp.where(free, score, -jnp.inf)
        m = jnp.max(cur, axis=-1, keepdims=True)
        idx = jnp.min(jnp.where(free & (cur == m), lane, w), axis=-1, keepdims=True)
        taken = jnp.where(lane == idx, 1, taken)
    return taken > 0


def _sample_q_rows(q):
    lane = lax.broadcasted_iota(jnp.int32, (8, LANES), 1)
    ev = [jnp.where(lane < HEAD_DIM, q[:, s * LANES:(s + 1) * LANES], 0.0) for s in range(8)]
    od = [jnp.where(lane >= HEAD_DIM, q[:, s * LANES:(s + 1) * LANES], 0.0) for s in range(8)]
    return jnp.concatenate(ev + od, axis=0)


def _nsa_sample_cmp_kernel(pt_ref, q_ref, wk_ref, wv_ref, pek_ref, pev_ref, b1_ref, w2k_ref, w2v_ref,
                           ovl_ref, *refs, pps, past, blk_per_step):
    kpages, vpages = refs[:pps], refs[pps:2 * pps]
    oc_ref, sel_ref = refs[2 * pps], refs[2 * pps + 1]
    kvc_ref, carry_ref, stage_ref = refs[2 * pps + 2:]
    s = pl.program_id(1)
    m = pps * (PAGE // NSA_CMP_STRIDE)

    @pl.when(s == 0)
    def _():
        carry_ref[...] = jnp.zeros_like(carry_ref)

    row0 = lax.broadcasted_iota(jnp.int32, (m, 1024), 0) == 0
    for i, (pages, w_ref, pe_ref, w2_ref) in enumerate(((kpages, wk_ref, pek_ref, w2k_ref),
                                                         (vpages, wv_ref, pev_ref, w2v_ref))):
        for t, pg in enumerate(pages):
            stage_ref[t * PAGE:(t + 1) * PAGE, :] = pg[...].T
        x = jnp.concatenate([stage_ref[pl.ds(l, m, stride=NSA_CMP_STRIDE), :] for l in range(NSA_CMP_STRIDE)],
                            axis=1).astype(BF16)
        pre = _dot(x, w_ref[...])
        pc = _dot(pe_ref[...].astype(BF16), w_ref[...])
        prev = jnp.where(row0, carry_ref[i:i + 1, :], pltpu.roll(pre, 1, 0))
        carry_ref[i:i + 1, :] = pre[m - 1:m, :]
        hid = _cmp_hidden(prev, pre, pc, b1_ref[i:i + 1, :])
        kvc_ref[pl.ds(pl.multiple_of(s * m, m), m), i * 128:(i + 1) * 128] = _dot(hid.astype(BF16), w2_ref[...])

    @pl.when(s == pl.num_programs(1) - 1)
    def _():
        nt = kvc_ref.shape[0]
        qa = _sample_q_rows(q_ref[...]).astype(BF16)
        kc = kvc_ref[:, 0:128].astype(BF16)
        vc = kvc_ref[:, 128:256].astype(BF16)
        r = lax.broadcasted_iota(jnp.int32, (128, nt), 1)
        qpos = past + (lax.broadcasted_iota(jnp.int32, (128, nt), 0) & 7)
        ok = (r >= 1) & ((r - 1) * NSA_CMP_STRIDE + (NSA_CMP_LEN - 1) <= qpos)
        sc = jnp.where(ok, _dot_nt(qa, kc), NEG)
        mc = jnp.max(sc, axis=-1, keepdims=True)
        ec = jnp.where(ok, jnp.exp(sc - mc), 0.0)
        pc = ec / jnp.maximum(jnp.sum(ec, axis=-1, keepdims=True), TINY)
        oc_ref[...] = _dot(pc.astype(BF16), vc)
        pg = jnp.sum(pc.reshape(2, 8, 8, nt), axis=1).reshape(16, nt)
        phi, plo = _split_bf16(pg)
        imp = _dot(phi, ovl_ref[...]) + _dot(plo, ovl_ref[...])
        nl = imp.shape[1]
        own = (past + (lax.broadcasted_iota(jnp.int32, (16, nl), 0) & 7)) >> 6
        jb = lax.broadcasted_iota(jnp.int32, (16, nl), 1)
        allowed = jb <= own
        forced = (jb == 0) | (jb == own) | (jb == own - 1)
        imp = jnp.where(forced, NSA_FORCE_SCORE, imp)
        imp = jnp.where(allowed, imp, -jnp.inf)
        sel = jnp.where(_topk_mask_iter(imp, NSA_SLC_TOPN) & allowed, 1.0, 0.0)
        lane = lax.broadcasted_iota(jnp.int32, (16, LANES), 1)
        for st in range(sel_ref.shape[0]):
            piece = sel if st == 0 else pltpu.roll(sel, nl - st * blk_per_step, 1)
            sel_ref[st] = jnp.where(lane < blk_per_step, piece[:, 0:LANES], 0.0)


def _nsa_sample_cmp(q, cache, page_table, wbig, pe_rows, b1, w2blk, past, pps):
    db, n_pages = page_table.shape
    nsteps = n_pages // pps
    nt = n_pages * (PAGE // NSA_CMP_STRIDE)
    n_tok = (past + 8 - NSA_CMP_LEN) // NSA_CMP_STRIDE + 1
    nsb = -(-(past + 8) // NSA_SLC_BLOCK)
    nl = -(-nsb // LANES) * LANES
    blk_per_step = pps * PAGE // NSA_SLC_BLOCK
    assert n_tok == nt - 1 and blk_per_step <= LANES
    ovl = jnp.asarray(_nsa_overlap(n_tok, nsb, 1, nt, nl), BF16)

    def page_spec(t, c):
        return pl.BlockSpec((None, 128, PAGE), lambda b, s, pt: (pt[b, s * pps + t], c, 0))

    grid_spec = pltpu.PrefetchScalarGridSpec(
        num_scalar_prefetch=1,
        grid=(db, nsteps),
        in_specs=[pl.BlockSpec((8, 1024), lambda b, s, pt: (b, 0)),
                  _full((2048, 1024)), _full((2048, 1024)), _full((8, 2048)), _full((8, 2048)),
                  _full((8, 256)), _full((512, 128)), _full((512, 128)), _full((nt, nl))]
                 + [page_spec(t, 0) for t in range(pps)] + [page_spec(t, 1) for t in range(pps)],
        out_specs=(pl.BlockSpec((128, 128), lambda b, s, pt: (b, 0)),
                   pl.BlockSpec((None, nsteps, 16, LANES), lambda b, s, pt: (b, 0, 0, 0))),
        scratch_shapes=[pltpu.VMEM((nt, 256), F32), pltpu.VMEM((8, 1024), F32),
                        pltpu.VMEM((pps * PAGE, 128), F32)],
    )
    return pl.pallas_call(
        functools.partial(_nsa_sample_cmp_kernel, pps=pps, past=past, blk_per_step=blk_per_step),
        out_shape=(jax.ShapeDtypeStruct((db * 128, 128), F32),
                   jax.ShapeDtypeStruct((db, nsteps, 16, LANES), F32)),
        grid_spec=grid_spec,
        compiler_params=_cparams("parallel", "arbitrary"),
    )(page_table, q, wbig[0], wbig[1], pe_rows[0], pe_rows[1], b1, w2blk[0], w2blk[1], ovl,
      *([cache] * (2 * pps)))


def _nsa_sample_attn_kernel(pt_ref, q_ref, sel_ref, oc_ref, kvn_ref, win_ref, gate_ref, e_ref, *refs,
                            pps, past):
    pages = refs[:pps]
    o_ref = refs[pps]
    qa_ref, m_ref, l_ref, acc_ref = refs[pps + 1:]
    s = pl.program_id(1)
    rows = 128

    @pl.when(s == 0)
    def _():
        qa_ref[...] = _sample_q_rows(q_ref[...]).astype(BF16)
        _flash_init(m_ref, l_ref, acc_ref)

    qa = qa_ref[...]
    kt = jnp.concatenate([pg[0:128, :] for pg in pages], axis=1).astype(BF16)
    vt = jnp.concatenate([pg[128:256, :] for pg in pages], axis=1).astype(BF16)
    sel = sel_ref[...]
    selrows = jnp.concatenate([sel[0:8, :]] * 8 + [sel[8:16, :]] * 8, axis=0).astype(BF16)
    ok = _dot(selrows, e_ref[...]) > 0.5
    _flash_masked(_dot(qa, kt), ok, lambda p: _dot_nt(p, vt), m_ref, l_ref, acc_ref)

    @pl.when(s == pl.num_programs(1) - 1)
    def _():
        col = lax.broadcasted_iota(jnp.int32, (rows, LANES), 1)
        qi = lax.broadcasted_iota(jnp.int32, (rows, LANES), 0) & 7
        kn = kvn_ref[256:384, :].astype(BF16)
        vn = kvn_ref[384:512, :].astype(BF16)
        _flash_masked(_dot(qa, kn), col <= qi, lambda p: _dot_nt(p, vn), m_ref, l_ref, acc_ref)
        o_s = acc_ref[...] / l_ref[...]
        wb = win_ref.shape[1]
        kw = jnp.concatenate([win_ref[0:128, :], kvn_ref[512:640, :]], axis=1).astype(BF16)
        vw = jnp.concatenate([win_ref[128:256, :], kvn_ref[640:768, :]], axis=1).astype(BF16)
        nw = wb + LANES
        c = lax.broadcasted_iota(jnp.int32, (rows, nw), 1)
        qpos = past + (lax.broadcasted_iota(jnp.int32, (rows, nw), 0) & 7)
        wpos = past - wb + c
        dist = qpos - wpos
        okw = (dist >= 0) & (dist < NSA_WINDOW) & (wpos >= 0) & (c < wb + 8)
        sw = jnp.where(okw, _dot(qa, kw), NEG)
        mw = jnp.max(sw, axis=-1, keepdims=True)
        pw = jnp.where(okw, jnp.exp(sw - mw), 0.0)
        o_w = _dot_nt(pw.astype(BF16), vw) / jnp.maximum(jnp.sum(pw, axis=-1, keepdims=True), TINY)
        o_c = oc_ref[...]
        gate = gate_ref[...]
        lane8 = lax.broadcasted_iota(jnp.int32, (8, LANES), 1)
        hs = []
        for h in range(16):
            r0 = h * 8
            hs.append(gate[:, 3 * h:3 * h + 1] * o_c[r0:r0 + 8, :]
                      + gate[:, 3 * h + 1:3 * h + 2] * o_s[r0:r0 + 8, :]
                      + gate[:, 3 * h + 2:3 * h + 3] * o_w[r0:r0 + 8, :])
        o_ref[...] = jnp.concatenate([jnp.where(lane8 < HEAD_DIM, hs[sl], hs[8 + sl]) for sl in range(8)],
                                     axis=1)


def _nsa_sample_attn(q, sel, o_c, kvn, state_win, gate, cache, page_table, past, pps):
    db, n_pages = page_table.shape
    nsteps = n_pages // pps
    wb = state_win.shape[2]
    nk = pps * PAGE
    e = np.zeros((LANES, nk), np.float32)
    e[np.arange(nk) // NSA_SLC_BLOCK, np.arange(nk)] = 1.0

    def page_spec(t):
        return pl.BlockSpec((None, 256, PAGE), lambda b, s, pt: (pt[b, s * pps + t], 1, 0))

    grid_spec = pltpu.PrefetchScalarGridSpec(
        num_scalar_prefetch=1,
        grid=(db, nsteps),
        in_specs=[pl.BlockSpec((8, 1024), lambda b, s, pt: (b, 0)),
                  pl.BlockSpec((None, None, 16, LANES), lambda b, s, pt: (b, s, 0, 0)),
                  pl.BlockSpec((128, 128), lambda b, s, pt: (b, 0)),
                  pl.BlockSpec((None, 768, LANES), lambda b, s, pt: (b, 0, 0)),
                  pl.BlockSpec((None, 256, wb), lambda b, s, pt: (b, 0, 0)),
                  pl.BlockSpec((8, 128), lambda b, s, pt: (b, 0)),
                  _full((LANES, nk))]
                 + [page_spec(t) for t in range(pps)],
        out_specs=pl.BlockSpec((8, 1024), lambda b, s, pt: (b, 0)),
        scratch_shapes=[pltpu.VMEM((128, 128), BF16), pltpu.VMEM((128, 128), F32), pltpu.VMEM((128, 128), F32),
                        pltpu.VMEM((128, 128), F32)],
    )
    return pl.pallas_call(
        functools.partial(_nsa_sample_attn_kernel, pps=pps, past=past),
        out_shape=jax.ShapeDtypeStruct((db * 8, 1024), F32),
        grid_spec=grid_spec,
        compiler_params=_cparams("parallel", "arbitrary"),
    )(page_table, q, sel, o_c, kvn, state_win, gate, jnp.asarray(e, BF16), *([cache] * pps))


def _nsa_layer(xp, xs, cache, state_win, page_table, ln_g, w_in, qn, kn, cmp_pe, cmp_w1, cmp_b1, cmp_w2,
               w_o, batch, seq, past_len):
    perm = _slot_perm(NSA_KV_HEADS)
    w_in_p = jnp.concatenate([w_in[:, :1024][:, perm], w_in[:, 1024:],
                              jnp.zeros((w_in.shape[0], 1920 - w_in.shape[1]), F32)], axis=1).astype(BF16)
    w_o_p = w_o[perm, :].astype(BF16)
    g = ln_g.reshape(1, -1)
    wbig, pe_rows, w2blk = _nsa_cmp_weights(cmp_pe, cmp_w1, cmp_w2)
    b1 = jnp.concatenate([cmp_b1, jnp.zeros((6, NSA_CMP_HIDDEN), F32)], axis=0)
    dec = 8
    q, kv, kvb, gate = _nsa_proj(xp, g, w_in_p, qn, kn, np.arange(seq), 512, BF16)
    kvc = _nsa_cmp_prompt(kv, wbig, pe_rows, b1, w2blk, batch, seq)
    o = _nsa_prompt_attn(q, kvb, kvc, gate, batch, seq)
    yp = _outproj(xp, o, w_o_p, 512)
    ns = xs.shape[0]
    db, n_pages = page_table.shape
    pps = min(16, n_pages)
    pos_s = np.tile(past_len + np.arange(dec), ns // dec)
    qs, kvs, _, gs = _nsa_proj(xs, g, w_in_p, qn, kn, pos_s, ns, F32)
    cache_t = _pages_t(cache)
    win_t = jnp.swapaxes(state_win.reshape(db, state_win.shape[1], 256), 1, 2)
    o_c, sel = _nsa_sample_cmp(qs, cache_t, page_table, wbig, pe_rows, b1, w2blk, past_len, pps)
    os_ = _nsa_sample_attn(qs, sel, o_c, _rows_t(kvs, db), win_t, gs, cache_t, page_table, past_len, pps)
    ys = _outproj(xs, os_, w_o_p, ns)
    return yp, ys, kv, kvs


MLA_QK = MLA_NOPE + MLA_ROPE


def _mla_proj_kernel(x_ref, g_ref, w_ref, wuq_ref, gm_ref, c_ref, cosq_ref, sinq_ref, cosk_ref, sink_ref,
                     q_ref, lat_ref):
    h = _rms_rows(x_ref[...], g_ref[...]).astype(BF16)
    cq = _rms_rows(_dot(h, w_ref[:, 0:MLA_Q_LORA]), c_ref[0:1, 0:MLA_Q_LORA]).astype(BF16)
    ckv = _rms_rows(_dot(h, w_ref[:, MLA_Q_LORA:MLA_Q_LORA + MLA_KV_LORA]), c_ref[1:2, 0:MLA_KV_LORA])
    a = _dot(h, w_ref[:, 640:768])
    ms = jnp.sum(a * a, axis=-1, keepdims=True) * (1.0 / MLA_ROPE)
    kpe = a * lax.rsqrt(ms + NORM_EPS) * c_ref[2:3, 0:128]
    kpe = _rope(kpe, cosk_ref[...], sink_ref[...], c_ref[3:4, 0:128], MLA_ROPE // 2)
    lat_ref[:, 0:MLA_KV_LORA] = ckv
    lat_ref[:, MLA_KV_LORA:MLA_KV_LORA + MLA_ROPE] = kpe[:, 0:MLA_ROPE]
    gm = gm_ref[...]
    qg, up = c_ref[4:5, 0:256], c_ref[5:6, 0:256]
    cos, sin = cosq_ref[...], sinq_ref[...]
    for c in range(8):
        a = _dot(cq, wuq_ref[:, c * 256:(c + 1) * 256])
        y = _rope(_group_norm(a, gm, qg), cos, sin, up, MLA_ROPE // 2)
        q_ref[:, c * 256:(c + 1) * 256] = (y * MLA_QK ** -0.5).astype(q_ref.dtype)


def _mla_cat_groups(width):
    g = []
    for lo in range(0, width, LANES):
        g += [(lo, lo + MLA_NOPE), (lo + MLA_NOPE, lo + MLA_QK)]
    return g


def _mla_cat_cols():
    idx = -np.ones((16, LANES), np.int64)
    for h in range(16):
        idx[h, :MLA_QK] = h * MLA_QK + np.arange(MLA_QK)
    return idx.reshape(-1)


def _take_cols(w, idx):
    wz = jnp.concatenate([w, jnp.zeros((w.shape[0], 1), w.dtype)], axis=1)
    return wz[:, np.where(idx < 0, w.shape[1], idx)]


def _mla_proj(x, g, w_dqkv, g_q, g_kv, w_uq, qn, kn, pos, tn, qdtype):
    n, d = x.shape
    period = pos.shape[0]
    w = jnp.concatenate([w_dqkv, jnp.zeros((d, 768 - w_dqkv.shape[1]), F32)], axis=1).astype(BF16)
    wuq = _take_cols(w_uq, _mla_cat_cols()).astype(BF16)
    cosq, sinq, upq = _rope_tables(pos, 256, [(MLA_NOPE, MLA_ROPE), (LANES + MLA_NOPE, MLA_ROPE)])
    cosk, sink, upk = _rope_tables(pos, 128, [(0, MLA_ROPE)])
    consts = jnp.zeros((8, 384), F32)
    consts = consts.at[0, :].set(g_q).at[1, 0:256].set(g_kv).at[2, 0:MLA_ROPE].set(kn[MLA_NOPE:])
    consts = consts.at[3, 0:128].set(jnp.asarray(upk[0])).at[5, 0:256].set(jnp.asarray(upq[0]))
    qgain = jnp.concatenate([qn, jnp.zeros((LANES - MLA_QK,), F32)])
    consts = consts.at[4, 0:256].set(jnp.tile(qgain, 2))
    gm = jnp.asarray(_group_mean_matrix(_mla_cat_groups(256), 256), BF16)
    nper = period // tn
    tabq = pl.BlockSpec((tn, 256), lambda i: (i % nper, 0))
    tabk = pl.BlockSpec((tn, 128), lambda i: (i % nper, 0))
    row = lambda wd: pl.BlockSpec((tn, wd), lambda i: (i, 0))
    return pl.pallas_call(
        _mla_proj_kernel,
        out_shape=(jax.ShapeDtypeStruct((n, 2048), qdtype), jax.ShapeDtypeStruct((n, 288), F32)),
        grid=(n // tn,),
        in_specs=[row(d), _full((1, d)), _full(w.shape), _full(wuq.shape), _full((256, 256)), _full((8, 384)),
                  tabq, tabq, tabk, tabk],
        out_specs=(row(2048), row(288)),
        compiler_params=_cparams("parallel"),
    )(x, g, w, wuq, gm, consts, jnp.asarray(cosq), jnp.asarray(sinq), jnp.asarray(cosk), jnp.asarray(sink))


def _mla_expand_kernel(lat_ref, wk_ref, wv_ref, gm_ref, c_ref, place_ref, k_ref, v_ref):
    ckv = lat_ref[:, 0:MLA_KV_LORA].astype(BF16)
    kpe = _dot(lat_ref[:, MLA_KV_LORA:MLA_KV_LORA + MLA_ROPE].astype(BF16), place_ref[...])
    gm = gm_ref[...]
    kg = c_ref[0:1, :]
    for c in range(8):
        e = _dot(ckv, wk_ref[:, c * 256:(c + 1) * 256])
        k_ref[:, c * 256:(c + 1) * 256] = (_group_norm(e, gm, kg) + kpe).astype(BF16)
    v_ref[...] = _dot(ckv, wv_ref[...]).astype(BF16)


def _mla_split_ukv(w_ukv):
    w = w_ukv.reshape(MLA_KV_LORA, 16, MLA_NOPE + MLA_V)
    return w[:, :, :MLA_NOPE].reshape(MLA_KV_LORA, 16 * MLA_NOPE), w[:, :, MLA_NOPE:].reshape(MLA_KV_LORA, 16 * MLA_V)


def _mla_expand(lat, w_ukv, kn, tn):
    n = lat.shape[0]
    wk_nat, wv = _mla_split_ukv(w_ukv)
    idx = -np.ones((16, LANES), np.int64)
    for h in range(16):
        idx[h, :MLA_NOPE] = h * MLA_NOPE + np.arange(MLA_NOPE)
    wk = _take_cols(wk_nat, idx.reshape(-1)).astype(BF16)
    gm = jnp.asarray(_group_mean_matrix([(lo, lo + MLA_NOPE) for lo in (0, LANES)], 256), BF16)
    kgain = jnp.concatenate([kn[:MLA_NOPE], jnp.zeros((LANES - MLA_NOPE,), F32)])
    consts = jnp.zeros((8, 256), F32).at[0].set(jnp.tile(kgain, 2))
    place = np.zeros((MLA_ROPE, 256), np.float32)
    for lo in (MLA_NOPE, LANES + MLA_NOPE):
        place[np.arange(MLA_ROPE), lo + np.arange(MLA_ROPE)] = 1.0
    row = lambda wd: pl.BlockSpec((tn, wd), lambda i: (i, 0))
    return pl.pallas_call(
        _mla_expand_kernel,
        out_shape=(jax.ShapeDtypeStruct((n, 2048), BF16), jax.ShapeDtypeStruct((n, 1024), BF16)),
        grid=(n // tn,),
        in_specs=[row(288), _full(wk.shape), _full((256, 1024)), _full((256, 256)), _full((8, 256)),
                  _full((MLA_ROPE, 256))],
        out_specs=(row(2048), row(1024)),
        compiler_params=_cparams("parallel"),
    )(lat, wk, wv.astype(BF16), gm, consts, jnp.asarray(place, BF16))


def _mla_prompt_kernel(q_ref, k_ref, v_ref, o_ref, m_ref, l_ref, acc_ref):
    i = pl.program_id(2)
    tq = q_ref.shape[0]
    tk = tq
    q0 = q_ref[:, 0:128]
    q1 = q_ref[:, 128:256]

    def scores(off):
        return jnp.concatenate([_dot_nt(q0, k_ref[pl.ds(off, tk), 0:128]),
                                _dot_nt(q1, k_ref[pl.ds(off, tk), 128:256])], axis=0)

    offd = pl.multiple_of(i * tk, tk)
    qi = lax.broadcasted_iota(jnp.int32, (2 * tq, tk), 0) & (tq - 1)
    ki = lax.broadcasted_iota(jnp.int32, (2 * tq, tk), 1)
    _flash_first(jnp.where(ki <= qi, scores(offd), NEG), lambda p: _dot(p, v_ref[pl.ds(offd, tk), :]),
                 m_ref, l_ref, acc_ref)

    def body(j, carry):
        off = pl.multiple_of(j * tk, tk)
        _flash_next(scores(off), lambda p: _dot(p, v_ref[pl.ds(off, tk), :]), m_ref, l_ref, acc_ref)
        return carry

    lax.fori_loop(0, i, body, 0)
    o = acc_ref[...] / l_ref[...]
    lane = lax.broadcasted_iota(jnp.int32, (tq, LANES), 1)
    o_ref[...] = jnp.where(lane < MLA_V, o[0:tq, :], o[tq:2 * tq, :]).astype(o_ref.dtype)


def _mla_prompt_attn(q, k, v, batch, seq):
    tq = 512
    nq = seq // tq
    return pl.pallas_call(
        _mla_prompt_kernel,
        out_shape=jax.ShapeDtypeStruct((batch * seq, 1024), BF16),
        grid=(batch, 8, nq),
        in_specs=[pl.BlockSpec((tq, 256), lambda b, p, i: (b * nq + i, p)),
                  pl.BlockSpec((seq, 256), lambda b, p, i: (b, p)),
                  pl.BlockSpec((seq, 128), lambda b, p, i: (b, p))],
        out_specs=pl.BlockSpec((tq, 128), lambda b, p, i: (b * nq + i, p)),
        scratch_shapes=[pltpu.VMEM((2 * tq, 128), F32), pltpu.VMEM((2 * tq, 128), F32),
                        pltpu.VMEM((2 * tq, 128), F32)],
        compiler_params=_cparams("parallel", "parallel", "arbitrary"),
    )(q, k, v)


def _mla_sample_kernel(pt_ref, q_ref, latn_ref, wk_ref, wkt_ref, wv_ref, kg_ref, gmean_ref, *refs, pps):
    pages = refs[:pps]
    o_ref = refs[pps]
    qt_ref, qpe_ref, m_ref, l_ref, acc_ref = refs[pps + 1:]
    s = pl.program_id(1)
    lane8 = lax.broadcasted_iota(jnp.int32, (8, LANES), 1)

    @pl.when(s == 0)
    def _():
        z = jnp.zeros((8, LANES), F32)
        qg_rows, pe_rows = [], []
        for h in range(16):
            ch = q_ref[:, h * LANES:(h + 1) * LANES]
            nope = jnp.where(lane8 < MLA_NOPE, ch, 0.0)
            if h % 2:
                nope = pltpu.roll(nope, MLA_NOPE, 1)
            qg_rows.append(jnp.concatenate([z] * (h // 2) + [nope] + [z] * (7 - h // 2), axis=1))
            pe_rows.append(pltpu.roll(ch, LANES - MLA_NOPE, 1)[:, 0:MLA_ROPE])
        qg = (jnp.concatenate(qg_rows, axis=0) * kg_ref[...]).astype(BF16)
        qt_ref[...] = _dot_nt(qg, wk_ref[...]).astype(BF16)
        qpe_ref[...] = jnp.concatenate(pe_rows, axis=0).astype(BF16)
        _flash_init(m_ref, l_ref, acc_ref)

    def update(lat_t, ok):
        ckv = lat_t[0:MLA_KV_LORA, :].astype(BF16)
        kpe = lat_t[MLA_KV_LORA:MLA_KV_LORA + MLA_ROPE, :].astype(BF16)
        e = _dot(wkt_ref[...], ckv)
        rs = lax.rsqrt(_dot(gmean_ref[...], (e * e).astype(BF16)) + NORM_EPS)
        sc = _dot(qt_ref[...], ckv) * rs + _dot(qpe_ref[...], kpe)
        pv = lambda p: _dot_nt(p, ckv)
        if ok is None:
            _flash_next(sc, pv, m_ref, l_ref, acc_ref)
        else:
            _flash_masked(sc, ok, pv, m_ref, l_ref, acc_ref)

    update(jnp.concatenate([pg[...] for pg in pages], axis=1), None)

    @pl.when(s == pl.num_programs(1) - 1)
    def _():
        rho = lax.broadcasted_iota(jnp.int32, (LANES, LANES), 0)
        t = lax.broadcasted_iota(jnp.int32, (LANES, LANES), 1)
        update(latn_ref[...], t <= (rho & 7))
        olat = (acc_ref[...] / _rep(l_ref[...], 2)).astype(BF16)
        ofull = _dot(olat, wv_ref[...])
        r_head = lax.broadcasted_iota(jnp.int32, (LANES, 1024), 0) >> 3
        c_head = lax.broadcasted_iota(jnp.int32, (LANES, 1024), 1) >> 6
        ofull = jnp.where(r_head == c_head, ofull, 0.0)
        out = ofull[0:8, :]
        for h in range(1, 16):
            out = out + ofull[h * 8:(h + 1) * 8, :]
        o_ref[...] = out


def _mla_sample_attn(q, latn_t, cache_t, page_table, w_ukv, kn, pps):
    db, n_pages = page_table.shape
    nsteps = n_pages // pps
    wk, wv = _mla_split_ukv(w_ukv)
    wk = wk.astype(BF16)
    kg = jnp.tile(kn[:MLA_NOPE], 16).reshape(1, 1024)
    gmean = np.zeros((LANES, 1024), np.float32)
    for h in range(16):
        gmean[h * 8:(h + 1) * 8, h * 64:(h + 1) * 64] = 1.0 / MLA_NOPE

    def page_spec(t):
        return pl.BlockSpec((None, 288, PAGE), lambda b, s, pt: (pt[b, s * pps + t], 0, 0))

    grid_spec = pltpu.PrefetchScalarGridSpec(
        num_scalar_prefetch=1,
        grid=(db, nsteps),
        in_specs=[pl.BlockSpec((8, 2048), lambda b, s, pt: (b, 0)),
                  pl.BlockSpec((None, 288, LANES), lambda b, s, pt: (b, 0, 0)),
                  _full((256, 1024)), _full((1024, 256)), _full((256, 1024)), _full((1, 1024)),
                  _full((LANES, 1024))]
                 + [page_spec(t) for t in range(pps)],
        out_specs=pl.BlockSpec((8, 1024), lambda b, s, pt: (b, 0)),
        scratch_shapes=[pltpu.VMEM((128, 256), BF16), pltpu.VMEM((128, MLA_ROPE), BF16),
                        pltpu.VMEM((128, 128), F32), pltpu.VMEM((128, 128), F32), pltpu.VMEM((128, 256), F32)],
    )
    return pl.pallas_call(
        functools.partial(_mla_sample_kernel, pps=pps),
        out_shape=jax.ShapeDtypeStruct((db * 8, 1024), F32),
        grid_spec=grid_spec,
        compiler_params=_cparams("parallel", "arbitrary"),
    )(page_table, q, latn_t, wk, wk.T, wv.astype(BF16), kg, jnp.asarray(gmean, BF16),
      *([cache_t] * pps))


def _mla_layer(xp, xs, cache, page_table, ln_g, w_dqkv, g_q, g_kv, w_uq, w_ukv, qn, kn, w_o,
               batch, seq, past_len):
    g = ln_g.reshape(1, -1)
    w_o_b = w_o.astype(BF16)
    dec = 8
    q, lat = _mla_proj(xp, g, w_dqkv, g_q, g_kv, w_uq, qn, kn, np.arange(seq), 512, BF16)
    k, v = _mla_expand(lat, w_ukv, kn, 512)
    o = _mla_prompt_attn(q, k, v, batch, seq)
    yp = _outproj(xp, o, w_o_b, 512)
    ns = xs.shape[0]
    pos_s = np.tile(past_len + np.arange(dec), ns // dec)
    qs, lats = _mla_proj(xs, g, w_dqkv, g_q, g_kv, w_uq, qn, kn, pos_s, ns, F32)
    os_ = _mla_sample_attn(qs, _rows_t(lats, page_table.shape[0]), _pages_t(cache), page_table, w_ukv, kn,
                           min(8, page_table.shape[1]))
    ys = _outproj(xs, os_, w_o_b, ns)
    return yp, ys, lat, lats


def kernel(x_prompt, x_sample, cache_kv_0, cache_kv_1, state_win_1, cache_lat_2, cache_kv_3, page_table, ln1_g, ln2_g, mlp_w1, mlp_w2, moba_w_in_0, moba_qn_0, moba_kn_0, moba_w_o_0, nsa_w_in_1, nsa_qn_1, nsa_kn_1, nsa_cmp_pe_1, nsa_cmp_w1_1, nsa_cmp_b1_1, nsa_cmp_w2_1, nsa_w_o_1, mla_w_dqkv_2, mla_g_q_2, mla_g_kv_2, mla_w_uq_2, mla_w_ukv_2, mla_qn_2, mla_kn_2, mla_w_o_2, moba_w_in_3, moba_qn_3, moba_kn_3, moba_w_o_3):
    batch, seq, d = x_prompt.shape
    db, dec, _ = x_sample.shape
    past = page_table.shape[1] * PAGE
    assert dec == 8 and seq % 512 == 0
    xp = x_prompt.reshape(batch * seq, d)
    xs = x_sample.reshape(db * dec, d)

    def mlp(i, xp, xs):
        w1 = mlp_w1[i].astype(BF16)
        w2 = mlp_w2[i].astype(BF16)
        g = ln2_g[i].reshape(1, d)
        return _mlp(xp, g, w1, w2, 512), _mlp(xs, g, w1, w2, db * dec)

    xp, xs, kv0_p, kv0_s = _moba_layer(xp, xs, cache_kv_0, page_table, ln1_g[0], moba_w_in_0, moba_qn_0,
                                       moba_kn_0, moba_w_o_0, batch, seq, past)
    xp, xs = mlp(0, xp, xs)
    xp, xs, kv1_p, kv1_s = _nsa_layer(xp, xs, cache_kv_1, state_win_1, page_table, ln1_g[1], nsa_w_in_1,
                                      nsa_qn_1, nsa_kn_1, nsa_cmp_pe_1, nsa_cmp_w1_1, nsa_cmp_b1_1,
                                      nsa_cmp_w2_1, nsa_w_o_1, batch, seq, past)
    xp, xs = mlp(1, xp, xs)
    xp, xs, lat_p, lat_s = _mla_layer(xp, xs, cache_lat_2, page_table, ln1_g[2], mla_w_dqkv_2, mla_g_q_2,
                                      mla_g_kv_2, mla_w_uq_2, mla_w_ukv_2, mla_qn_2, mla_kn_2, mla_w_o_2,
                                      batch, seq, past)
    xp, xs = mlp(2, xp, xs)
    xp, xs, kv3_p, kv3_s = _moba_layer(xp, xs, cache_kv_3, page_table, ln1_g[3], moba_w_in_3, moba_qn_3,
                                       moba_kn_3, moba_w_o_3, batch, seq, past)
    xp, xs = mlp(3, xp, xs)

    wb_p = min(NSA_WINDOW, seq)
    win_p = kv1_p[:, 512:768].reshape(batch, seq, 2, NSA_KV_HEADS, HEAD_DIM)[:, seq - wb_p:]
    win_new = kv1_s[:, 512:768].reshape(db, dec, 2, NSA_KV_HEADS, HEAD_DIM)
    win_s = jnp.concatenate([state_win_1, win_new], axis=1)[:, dec:]
    return (xp.reshape(batch, seq, d), xs.reshape(db, dec, d),
            kv0_p.reshape(batch, seq, 2, MOBA_KV_HEADS, HEAD_DIM),
            kv1_p[:, :512].reshape(batch, seq, 4, NSA_KV_HEADS, HEAD_DIM),
            win_p,
            lat_p.reshape(batch, seq, MLA_KV_LORA + MLA_ROPE),
            kv3_p.reshape(batch, seq, 2, MOBA_KV_HEADS, HEAD_DIM),
            kv0_s.reshape(db, dec, 2, MOBA_KV_HEADS, HEAD_DIM),
            kv1_s[:, :512].reshape(db, dec, 4, NSA_KV_HEADS, HEAD_DIM),
            win_s,
            lat_s.reshape(db, dec, MLA_KV_LORA + MLA_ROPE),
            kv3_s.reshape(db, dec, 2, MOBA_KV_HEADS, HEAD_DIM))
```

```python
import functools

import numpy as np
import jax
import jax.numpy as jnp
from jax import lax
from jax.experimental import pallas as pl
from jax.experimental.pallas import tpu as pltpu

F32 = jnp.float32
BF16 = jnp.bfloat16

HEAD_DIM = 64
ROPE_THETA = 10000.0
NORM_EPS = 1e-6
PAGE = 128
MOBA_KV_HEADS = 4
MOBA_BLOCK = 256
MOBA_TOPK = 3
NSA_KV_HEADS = 2
NSA_CMP_LEN = 32
NSA_CMP_STRIDE = 16
NSA_CMP_HIDDEN = 256
NSA_SLC_BLOCK = 64
NSA_SLC_TOPN = 16
NSA_WINDOW = 512
NSA_FORCE_SCORE = 1e9
MLA_Q_LORA = 384
MLA_KV_LORA = 256
MLA_NOPE = 64
MLA_ROPE = 32
MLA_V = 64

LANES = 128
VMEM_LIMIT_BYTES = 56 * 1024 * 1024
NEG = -1e30
TINY = float(np.finfo(np.float32).tiny)


def _cparams(*sem):
    return pltpu.CompilerParams(dimension_semantics=sem, vmem_limit_bytes=VMEM_LIMIT_BYTES)


def _dot(a, b):
    return jnp.dot(a, b, preferred_element_type=F32)


def _dot_nt(a, b):
    return lax.dot_general(a, b, (((1,), (1,)), ((), ())), preferred_element_type=F32)


def _dot_tn(a, b):
    return lax.dot_general(a, b, (((0,), (0,)), ((), ())), preferred_element_type=F32)


def _split_bf16(x):
    hi = x.astype(BF16)
    lo = (x - hi.astype(F32)).astype(BF16)
    return hi, lo


def _full(shape):
    n = len(shape)
    return pl.BlockSpec(shape, lambda *_: (0,) * n)


def _slot_perm(kv_heads, n_heads=16):
    grp = n_heads // kv_heads
    cols = []
    for s in range(n_heads // 2):
        p, r = divmod(s, grp)
        for h in ((2 * p) * grp + r, (2 * p + 1) * grp + r):
            cols.append(h * HEAD_DIM + np.arange(HEAD_DIM))
    return np.concatenate(cols)


def _slot_heads(kv_heads, n_heads=16):
    grp = n_heads // kv_heads
    out = []
    for s in range(n_heads // 2):
        p, r = divmod(s, grp)
        out.append(((2 * p) * grp + r, (2 * p + 1) * grp + r))
    return out


def _group_mean_matrix(groups, width):
    m = np.zeros((width, width), np.float32)
    for lo, hi in groups:
        m[lo:hi, lo:hi] = 1.0 / (hi - lo)
    return m


def _rope_tables(pos, width, segs):
    pos = np.asarray(pos, np.float64)
    cos = np.ones((pos.shape[0], width), np.float64)
    sin = np.zeros((pos.shape[0], width), np.float64)
    up = np.zeros((1, width), np.float32)
    for lo, dim in segs:
        half = dim // 2
        inv = ROPE_THETA ** (-np.arange(half, dtype=np.float64) / half)
        ang = pos[:, None] * inv[None, :]
        cos[:, lo:lo + half] = np.cos(ang)
        cos[:, lo + half:lo + dim] = np.cos(ang)
        sin[:, lo:lo + half] = -np.sin(ang)
        sin[:, lo + half:lo + dim] = np.sin(ang)
        up[:, lo:lo + half] = 1.0
    return cos.astype(np.float32), sin.astype(np.float32), up


def _head64_segs(width):
    return [(lo, HEAD_DIM) for lo in range(0, width, HEAD_DIM)]


def _rms_rows(x, g):
    ms = jnp.mean(x * x, axis=-1, keepdims=True)
    return x * lax.rsqrt(ms + NORM_EPS) * g


def _group_norm(a, gm, gain):
    ms = _dot((a * a).astype(BF16), gm)
    return a * lax.rsqrt(ms + NORM_EPS) * gain


def _rope(y, cos, sin, up, shift):
    w = y.shape[-1]
    hi = pltpu.roll(y, w - shift, 1)
    lo = pltpu.roll(y, shift, 1)
    partner = jnp.where(up > 0.5, hi, lo)
    return y * cos + partner * sin


def _gelu_tanh(x):
    return 0.5 * x * (1.0 + jnp.tanh(0.7978845608028654 * (x + 0.044715 * (x * x * x))))


def _outproj_kernel(x_ref, o_ref, w_ref, y_ref):
    y_ref[...] = x_ref[...] + _dot(o_ref[...].astype(BF16), w_ref[...])


def _outproj(x, o, w, tn):
    n, d = x.shape
    k = o.shape[1]
    return pl.pallas_call(
        _outproj_kernel,
        out_shape=jax.ShapeDtypeStruct((n, d), F32),
        grid=(n // tn,),
        in_specs=[pl.BlockSpec((tn, d), lambda i: (i, 0)),
                  pl.BlockSpec((tn, k), lambda i: (i, 0)),
                  _full((k, d))],
        out_specs=pl.BlockSpec((tn, d), lambda i: (i, 0)),
        compiler_params=_cparams("parallel"),
    )(x, o, w)


def _mlp_kernel(x_ref, g_ref, w1_ref, w2_ref, y_ref, *, ff_chunk):
    x = x_ref[...]
    h = _rms_rows(x, g_ref[...]).astype(BF16)
    acc = x
    for c in range(w1_ref.shape[1] // ff_chunk):
        u = _dot(h, w1_ref[:, c * ff_chunk:(c + 1) * ff_chunk])
        u = jnp.maximum(u, 0.0)
        acc = acc + _dot((u * u).astype(BF16), w2_ref[c * ff_chunk:(c + 1) * ff_chunk, :])
    y_ref[...] = acc


def _mlp(x, g, w1, w2, tn):
    n, d = x.shape
    ff = w1.shape[1]
    return pl.pallas_call(
        functools.partial(_mlp_kernel, ff_chunk=1024),
        out_shape=jax.ShapeDtypeStruct((n, d), F32),
        grid=(n // tn,),
        in_specs=[pl.BlockSpec((tn, d), lambda i: (i, 0)),
                  _full((1, d)), _full((d, ff)), _full((ff, d))],
        out_specs=pl.BlockSpec((tn, d), lambda i: (i, 0)),
        compiler_params=_cparams("parallel"),
    )(x, g, w1, w2)


def _moba_proj_kernel(x_ref, g_ref, w_ref, gm_ref, c_ref, cos_ref, sin_ref, q_ref, kv_ref, kvb_ref, kvt_ref):
    h = _rms_rows(x_ref[...], g_ref[...]).astype(BF16)
    gm = gm_ref[...]
    cos, sin = cos_ref[...], sin_ref[...]
    qg, kg, up = c_ref[0:1, :], c_ref[1:2, :], c_ref[2:3, :]
    for c in range(4):
        a = _dot(h, w_ref[:, c * 256:(c + 1) * 256])
        y = _rope(_group_norm(a, gm, qg), cos, sin, up, HEAD_DIM // 2)
        q_ref[:, c * 256:(c + 1) * 256] = (y * HEAD_DIM ** -0.5).astype(q_ref.dtype)
    a = _dot(h, w_ref[:, 1024:1280])
    k = _rope(_group_norm(a, gm, kg), cos, sin, up, HEAD_DIM // 2)
    v = _dot(h, w_ref[:, 1280:1536])
    kv_ref[:, 0:256] = k
    kv_ref[:, 256:512] = v
    kvb_ref[:, 0:256] = k.astype(BF16)
    kvb_ref[:, 256:512] = v.astype(BF16)
    kvt_ref[0:256, :] = k.T
    kvt_ref[256:512, :] = v.T


def _t_spec(feat, tn, nper):
    return pl.BlockSpec((None, feat, tn), lambda i: (i // nper, 0, i % nper))


def _moba_proj(x, g, w, qn, kn, pos, tn, qdtype):
    n, d = x.shape
    period = pos.shape[0]
    cos, sin, up = _rope_tables(pos, 256, _head64_segs(256))
    consts = np.zeros((8, 256), np.float32)
    consts[2] = up[0]
    consts = jnp.asarray(consts).at[0].set(jnp.tile(qn, 4)).at[1].set(jnp.tile(kn, 4))
    gm = jnp.asarray(_group_mean_matrix([(lo, lo + 64) for lo in range(0, 256, 64)], 256), BF16)
    nper = period // tn
    tab = pl.BlockSpec((tn, 256), lambda i: (i % nper, 0))
    return pl.pallas_call(
        _moba_proj_kernel,
        out_shape=(jax.ShapeDtypeStruct((n, 1024), qdtype),
                   jax.ShapeDtypeStruct((n, 512), F32),
                   jax.ShapeDtypeStruct((n, 512), BF16),
                   jax.ShapeDtypeStruct((n // period, 512, period), F32)),
        grid=(n // tn,),
        in_specs=[pl.BlockSpec((tn, d), lambda i: (i, 0)), _full((1, d)), _full(w.shape),
                  _full((256, 256)), _full((8, 256)), tab, tab],
        out_specs=(pl.BlockSpec((tn, 1024), lambda i: (i, 0)),
                   pl.BlockSpec((tn, 512), lambda i: (i, 0)),
                   pl.BlockSpec((tn, 512), lambda i: (i, 0)),
                   _t_spec(512, tn, nper)),
        compiler_params=_cparams("parallel"),
    )(x, g, w, gm, consts, jnp.asarray(cos), jnp.asarray(sin))


def _topk_mask(score, k, n_valid):
    rows, w = score.shape
    lane = lax.broadcasted_iota(jnp.int32, (rows, w), 1)
    rank = jnp.zeros((rows, w), jnp.int32)
    for j in range(n_valid):
        sj = score[:, j:j + 1]
        ahead = (sj > score) | ((sj == score) & (lane > j))
        rank = rank + ahead.astype(jnp.int32)
    return rank < k


def _topk_mask_t(score_t, k, n_valid):
    nc, n = score_t.shape
    cand = lax.broadcasted_iota(jnp.int32, (nc, n), 0)
    rank = jnp.zeros((nc, n), jnp.int32)
    for j in range(n_valid):
        sj = score_t[j:j + 1, :]
        ahead = (sj > score_t) | ((sj == score_t) & (cand > j))
        rank = rank + ahead.astype(jnp.int32)
    return rank < k


def _bias_rows(sel_t):
    nc, n = sel_t.shape
    bias_t = jnp.where(sel_t, 0.0, NEG)
    if nc < LANES:
        bias_t = jnp.concatenate([bias_t, jnp.zeros((LANES - nc, n), F32)], axis=0)
    return bias_t.T


def _rep(x, n):
    return x if n == 1 else jnp.concatenate([x] * n, axis=1)


def _flash_first(s, pv, m_ref, l_ref, acc_ref):
    m = jnp.max(s, axis=-1, keepdims=True)
    p = jnp.exp(s - m)
    m_ref[...] = jnp.broadcast_to(m, m_ref.shape)
    l_ref[...] = jnp.broadcast_to(jnp.sum(p, axis=-1, keepdims=True), l_ref.shape)
    acc_ref[...] = pv(p.astype(BF16))


def _flash_masked(s, ok, pv, m_ref, l_ref, acc_ref):
    s = jnp.where(ok, s, NEG)
    m_old = m_ref[...]
    m_new = jnp.maximum(m_old, jnp.max(s, axis=-1, keepdims=True))
    p = jnp.where(ok, jnp.exp(s - _rep(m_new, s.shape[1] // LANES)), 0.0)
    alpha = jnp.exp(m_old - m_new)
    m_ref[...] = m_new
    l_ref[...] = alpha * l_ref[...] + jnp.sum(p, axis=-1, keepdims=True)
    acc_ref[...] = _rep(alpha, acc_ref.shape[1] // LANES) * acc_ref[...] + pv(p.astype(BF16))


def _flash_next(s, pv, m_ref, l_ref, acc_ref):
    m_old = m_ref[...]
    m_new = jnp.maximum(m_old, jnp.max(s, axis=-1, keepdims=True))
    p = jnp.exp(s - _rep(m_new, s.shape[1] // LANES))
    alpha = jnp.exp(m_old - m_new)
    m_ref[...] = m_new
    l_ref[...] = alpha * l_ref[...] + jnp.sum(p, axis=-1, keepdims=True)
    acc_ref[...] = _rep(alpha, acc_ref.shape[1] // LANES) * acc_ref[...] + pv(p.astype(BF16))


def _slot_rows(q_ref, n_slots):
    tq = q_ref.shape[0]
    lane = lax.broadcasted_iota(jnp.int32, (tq, LANES), 1)
    ev, od = [], []
    for s in range(n_slots):
        qs = q_ref[:, s * LANES:(s + 1) * LANES]
        ev.append(jnp.where(lane < HEAD_DIM, qs, jnp.zeros_like(qs)))
        od.append(jnp.where(lane >= HEAD_DIM, qs, jnp.zeros_like(qs)))
    return jnp.concatenate(ev + od, axis=0)


def _slot_out(o, n_slots, tq):
    lane = lax.broadcasted_iota(jnp.int32, (tq, LANES), 1)
    outs = []
    for s in range(n_slots):
        e = o[s * tq:(s + 1) * tq, :]
        d = o[(n_slots + s) * tq:(n_slots + s + 1) * tq, :]
        outs.append(jnp.where(lane < HEAD_DIM, e, d))
    return jnp.concatenate(outs, axis=1)


def _moba_prompt_kernel(q_ref, k_ref, v_ref, bm_ref, o_ref, m_ref, l_ref, acc_ref):
    i = pl.program_id(2)
    tq = q_ref.shape[0]
    nb = k_ref.shape[0] // MOBA_BLOCK
    rows = 8 * tq
    qa = _slot_rows(q_ref, 4)
    bhi, blo = _split_bf16(bm_ref[...])
    gate_t = _dot_nt(bhi, qa) + _dot_nt(blo, qa)
    blk = lax.broadcasted_iota(jnp.int32, (nb, rows), 0)
    gate_t = jnp.where(blk < i, gate_t, -jnp.inf)
    sel_t = _topk_mask_t(gate_t, MOBA_TOPK, nb) & (blk < i)
    qaug = jnp.concatenate([qa, _bias_rows(sel_t).astype(BF16)], axis=1)

    kd = k_ref[pl.ds(pl.multiple_of(i * MOBA_BLOCK, MOBA_BLOCK), MOBA_BLOCK), :]
    vd = v_ref[pl.ds(pl.multiple_of(i * MOBA_BLOCK, MOBA_BLOCK), MOBA_BLOCK), :]
    qi = lax.broadcasted_iota(jnp.int32, (rows, MOBA_BLOCK), 0) & (tq - 1)
    ki = lax.broadcasted_iota(jnp.int32, (rows, MOBA_BLOCK), 1)
    _flash_first(jnp.where(ki <= qi, _dot_nt(qa, kd), NEG), lambda p: _dot(p, vd), m_ref, l_ref, acc_ref)

    lane = lax.broadcasted_iota(jnp.int32, (MOBA_BLOCK, LANES), 1)
    for j in range(nb - 1):
        @pl.when(j < i)
        def _(j=j):
            onehot = jnp.where(lane == j, 1.0, 0.0).astype(BF16)
            kj = jnp.concatenate([k_ref[j * MOBA_BLOCK:(j + 1) * MOBA_BLOCK, :], onehot], axis=1)
            vj = v_ref[j * MOBA_BLOCK:(j + 1) * MOBA_BLOCK, :]
            _flash_next(_dot_nt(qaug, kj), lambda p: _dot(p, vj), m_ref, l_ref, acc_ref)

    o = acc_ref[...] / l_ref[...]
    o_ref[...] = _slot_out(o, 4, tq).astype(o_ref.dtype)


def _moba_prompt_attn(q, kvb, bm, batch, seq):
    tq = MOBA_BLOCK
    nq = seq // tq
    nb = seq // MOBA_BLOCK
    return pl.pallas_call(
        _moba_prompt_kernel,
        out_shape=jax.ShapeDtypeStruct((batch * seq, 1024), BF16),
        grid=(batch, 2, nq),
        in_specs=[pl.BlockSpec((tq, 512), lambda b, p, i: (b * nq + i, p)),
                  pl.BlockSpec((seq, 128), lambda b, p, i: (b, p)),
                  pl.BlockSpec((seq, 128), lambda b, p, i: (b, 2 + p)),
                  pl.BlockSpec((nb, 128), lambda b, p, i: (b, p))],
        out_specs=pl.BlockSpec((tq, 512), lambda b, p, i: (b * nq + i, p)),
        scratch_shapes=[pltpu.VMEM((8 * tq, 128), F32), pltpu.VMEM((8 * tq, 128), F32),
                        pltpu.VMEM((8 * tq, 128), F32)],
        compiler_params=_cparams("parallel", "parallel", "arbitrary"),
    )(q, kvb, kvb, bm)


def _bmean_kernel(k_ref, o_ref):
    nb = o_ref.shape[0]
    k = k_ref[...].reshape(nb, MOBA_BLOCK, k_ref.shape[1])
    o_ref[...] = jnp.sum(k, axis=1) * (1.0 / MOBA_BLOCK)


def _moba_bmean(kv, batch, seq):
    nb = seq // MOBA_BLOCK
    return pl.pallas_call(
        _bmean_kernel,
        out_shape=jax.ShapeDtypeStruct((batch * nb, 256), F32),
        grid=(batch,),
        in_specs=[pl.BlockSpec((seq, 256), lambda b: (b, 0))],
        out_specs=pl.BlockSpec((nb, 256), lambda b: (b, 0)),
        compiler_params=_cparams("parallel"),
    )(kv)


def _moba_sample_kernel(pt_ref, q_ref, kvn_ref, *refs, bps, nb_past):
    pages = refs[:2 * bps]
    o_ref = refs[2 * bps]
    qa_ref, bm_ref, mst_ref, lst_ref, oacc_ref = refs[2 * bps + 1:]
    s = pl.program_id(1)
    rows = 128
    lane128 = lax.broadcasted_iota(jnp.int32, (8, LANES), 1)

    @pl.when(s == 0)
    def _():
        q = q_ref[...]
        z = jnp.zeros((8, LANES), F32)
        ev, od = [], []
        for sl in range(8):
            qs = q[:, sl * LANES:(sl + 1) * LANES]
            e = jnp.where(lane128 < HEAD_DIM, qs, 0.0)
            d = jnp.where(lane128 >= HEAD_DIM, qs, 0.0)
            if sl // 4 == 0:
                ev.append(jnp.concatenate([e, z], axis=1))
                od.append(jnp.concatenate([d, z], axis=1))
            else:
                ev.append(jnp.concatenate([z, e], axis=1))
                od.append(jnp.concatenate([z, d], axis=1))
        qa_ref[...] = jnp.concatenate(ev + od, axis=0).astype(BF16)
        bm_ref[...] = jnp.zeros_like(bm_ref)
        mst_ref[...] = jnp.zeros_like(mst_ref)
        lst_ref[...] = jnp.zeros_like(lst_ref)

    qa = qa_ref[...]
    col = lax.broadcasted_iota(jnp.int32, (rows, LANES), 1)
    for t in range(bps):
        jg = s * bps + t
        kt = jnp.concatenate([pages[2 * t][0:256, :], pages[2 * t + 1][0:256, :]], axis=1)
        vt = jnp.concatenate([pages[2 * t][256:512, :], pages[2 * t + 1][256:512, :]], axis=1)
        sc = _dot(qa, kt.astype(BF16))
        bm_ref[...] = jnp.where(col == jg, jnp.sum(sc, axis=-1, keepdims=True), bm_ref[...])
        m = jnp.max(sc, axis=-1, keepdims=True)
        p = jnp.exp(sc - m)
        oacc_ref[jg] = _dot_nt(p.astype(BF16), vt.astype(BF16))
        mst_ref[...] = jnp.where(col == jg, m, mst_ref[...])
        lst_ref[...] = jnp.where(col == jg, jnp.sum(p, axis=-1, keepdims=True), lst_ref[...])

    @pl.when(s == pl.num_programs(1) - 1)
    def _():
        kn = kvn_ref[0:256, :].astype(BF16)
        vn = kvn_ref[256:512, :].astype(BF16)
        sn = _dot(qa, kn)
        qi = lax.broadcasted_iota(jnp.int32, (rows, LANES), 0) & 7
        okn = col <= qi
        sn = jnp.where(okn, sn, NEG)
        mn = jnp.max(sn, axis=-1, keepdims=True)
        pn = jnp.where(okn, jnp.exp(sn - mn), 0.0)
        ln = jnp.sum(pn, axis=-1, keepdims=True)
        on = _dot_nt(pn.astype(BF16), vn)
        gate = bm_ref[...]
        past = col < nb_past
        gate = jnp.where(past, gate, -jnp.inf)
        sel = _topk_mask(gate, MOBA_TOPK, nb_past) & past
        mst = mst_ref[...]
        mstar = jnp.maximum(jnp.max(jnp.where(sel, mst, NEG), axis=-1, keepdims=True), mn)
        w = jnp.where(sel, jnp.exp(mst - mstar), 0.0)
        wn = jnp.exp(mn - mstar)
        den = jnp.sum(w * lst_ref[...], axis=-1, keepdims=True) + wn * ln
        acc = wn * on
        for j in range(nb_past):
            acc = acc + w[:, j:j + 1] * oacc_ref[j]
        o = acc / den
        outs = []
        for sl in range(8):
            c = (sl // 4) * LANES
            e = o[sl * 8:(sl + 1) * 8, c:c + LANES]
            d = o[(8 + sl) * 8:(9 + sl) * 8, c:c + LANES]
            outs.append(jnp.where(lane128 < HEAD_DIM, e, d))
        o_ref[...] = jnp.concatenate(outs, axis=1)


def _pages_t(cache):
    return jnp.swapaxes(cache.reshape(cache.shape[0], PAGE, -1), 1, 2)


def _rows_t(x, db):
    xt = jnp.swapaxes(x.reshape(db, x.shape[0] // db, x.shape[1]), 1, 2)
    return jnp.pad(xt, ((0, 0), (0, 0), (0, LANES - xt.shape[2])))


def _moba_sample_attn(q, kvn_t, cache_t, page_table, bps):
    db, n_pages = page_table.shape
    nb_past = n_pages * PAGE // MOBA_BLOCK
    assert n_pages % (2 * bps) == 0 and nb_past < LANES
    nsteps = n_pages // (2 * bps)

    def page_spec(t):
        return pl.BlockSpec((None, 512, PAGE), lambda b, s, pt: (pt[b, s * 2 * bps + t], 0, 0))

    grid_spec = pltpu.PrefetchScalarGridSpec(
        num_scalar_prefetch=1,
        grid=(db, nsteps),
        in_specs=[pl.BlockSpec((8, 1024), lambda b, s, pt: (b, 0)),
                  pl.BlockSpec((None, 512, LANES), lambda b, s, pt: (b, 0, 0))]
                 + [page_spec(t) for t in range(2 * bps)],
        out_specs=pl.BlockSpec((8, 1024), lambda b, s, pt: (b, 0)),
        scratch_shapes=[pltpu.VMEM((128, 256), BF16), pltpu.VMEM((128, LANES), F32),
                        pltpu.VMEM((128, LANES), F32), pltpu.VMEM((128, LANES), F32),
                        pltpu.VMEM((nb_past, 128, 256), F32)],
    )
    return pl.pallas_call(
        functools.partial(_moba_sample_kernel, bps=bps, nb_past=nb_past),
        out_shape=jax.ShapeDtypeStruct((db * 8, 1024), F32),
        grid_spec=grid_spec,
        compiler_params=_cparams("parallel", "arbitrary"),
    )(page_table, q, kvn_t, *([cache_t] * (2 * bps)))


def _moba_layer(xp, xs, cache, page_table, ln_g, w_in, qn, kn, w_o, batch, seq, past_len):
    perm = _slot_perm(MOBA_KV_HEADS)
    w_in_p = jnp.concatenate([w_in[:, :1024][:, perm], w_in[:, 1024:]], axis=1).astype(BF16)
    w_o_p = w_o[perm, :].astype(BF16)
    g = ln_g.reshape(1, -1)
    dec = 8
    q, kv, kvb, kvt = _moba_proj(xp, g, w_in_p, qn, kn, np.arange(seq), 512, BF16)
    bm = _moba_bmean(kv, batch, seq)
    o = _moba_prompt_attn(q, kvb, bm, batch, seq)
    yp = _outproj(xp, o, w_o_p, 512)
    ns = xs.shape[0]
    pos_s = np.tile(past_len + np.arange(dec), ns // dec)
    qs, kvs, _, _ = _moba_proj(xs, g, w_in_p, qn, kn, pos_s, ns, F32)
    db, n_pages = page_table.shape
    bps = max(b for b in (1, 2, 4, 8) if n_pages % (2 * b) == 0)
    os_ = _moba_sample_attn(qs, _rows_t(kvs, db), _pages_t(cache), page_table, bps)
    ys = _outproj(xs, os_, w_o_p, ns)
    return yp, ys, kvt, kvs


def _nsa_proj_kernel(x_ref, g_ref, w_ref, gm_ref, c_ref, cos_ref, sin_ref,
                     q_ref, kv_ref, kvb_ref, gate_ref, kvt_ref):
    h = _rms_rows(x_ref[...], g_ref[...]).astype(BF16)
    gm = gm_ref[...]
    cos, sin = cos_ref[...], sin_ref[...]
    qg, up = c_ref[0:1, :], c_ref[4:5, :]
    for c in range(4):
        a = _dot(h, w_ref[:, c * 256:(c + 1) * 256])
        y = _rope(_group_norm(a, gm, qg), cos, sin, up, HEAD_DIM // 2)
        q_ref[:, c * 256:(c + 1) * 256] = (y * HEAD_DIM ** -0.5).astype(q_ref.dtype)
    gm1 = gm_ref[0:128, 0:128]
    for br in range(3):
        a = _dot(h, w_ref[:, 1024 + br * 256:1024 + (br + 1) * 256])
        k = _rope(_group_norm(a[:, 0:128], gm1, c_ref[1 + br:2 + br, 0:128]),
                  cos[:, 0:128], sin[:, 0:128], up[:, 0:128], HEAD_DIM // 2)
        v = a[:, 128:256]
        kv_ref[:, br * 256:br * 256 + 128] = k
        kv_ref[:, br * 256 + 128:(br + 1) * 256] = v
        kvb_ref[:, br * 256:br * 256 + 128] = k.astype(BF16)
        kvb_ref[:, br * 256 + 128:(br + 1) * 256] = v.astype(BF16)
        kvt_ref[br * 256:br * 256 + 128, :] = k.T
        kvt_ref[br * 256 + 128:(br + 1) * 256, :] = v.T
    a = _dot(h, w_ref[:, 1792:1920])
    gate_ref[...] = 1.0 / (1.0 + jnp.exp(-a))


def _nsa_proj(x, g, w, qn, kn, pos, tn, qdtype):
    n, d = x.shape
    period = pos.shape[0]
    cos, sin, up = _rope_tables(pos, 256, _head64_segs(256))
    consts = np.zeros((8, 256), np.float32)
    consts[4] = up[0]
    consts = jnp.asarray(consts).at[0].set(jnp.tile(qn, 4))
    for br in range(3):
        consts = consts.at[1 + br].set(jnp.tile(kn[br], 4))
    gm = jnp.asarray(_group_mean_matrix([(lo, lo + 64) for lo in range(0, 256, 64)], 256), BF16)
    nper = period // tn
    tab = pl.BlockSpec((tn, 256), lambda i: (i % nper, 0))
    row = lambda wd: pl.BlockSpec((tn, wd), lambda i: (i, 0))
    return pl.pallas_call(
        _nsa_proj_kernel,
        out_shape=(jax.ShapeDtypeStruct((n, 1024), qdtype),
                   jax.ShapeDtypeStruct((n, 768), F32),
                   jax.ShapeDtypeStruct((n, 768), BF16),
                   jax.ShapeDtypeStruct((n, 128), F32),
                   jax.ShapeDtypeStruct((n // period, 768, period), F32)),
        grid=(n // tn,),
        in_specs=[row(d), _full((1, d)), _full(w.shape), _full((256, 256)), _full((8, 256)), tab, tab],
        out_specs=(row(1024), row(768), row(768), row(128), _t_spec(768, tn, nper)),
        compiler_params=_cparams("parallel"),
    )(x, g, w, gm, consts, jnp.asarray(cos), jnp.asarray(sin))


def _nsa_cmp_weights(cmp_pe, cmp_w1, cmp_w2):
    w1h, pe_rows, w2blk = [], [], []
    for i in range(2):
        w1h.append(jnp.concatenate([cmp_w1[i][:1024], cmp_w1[i][1024:]], axis=1).astype(BF16))
        pe_rows.append(jnp.concatenate([cmp_pe[i].reshape(2, 1024), jnp.zeros((6, 1024), F32)], axis=0))
        z2 = jnp.zeros_like(cmp_w2[i])
        w2blk.append(jnp.concatenate([jnp.concatenate([cmp_w2[i], z2], axis=1),
                                      jnp.concatenate([z2, cmp_w2[i]], axis=1)], axis=0).astype(BF16))
    return w1h, pe_rows, w2blk


def _cmp_chunks(row_ref, m):
    lane = lax.broadcasted_iota(jnp.int32, (m, LANES), 1)
    g0, g1 = [], []
    for a in range(NSA_CMP_STRIDE // 2):
        xe = row_ref[pl.ds(2 * a, m, stride=NSA_CMP_STRIDE), :]
        xo = row_ref[pl.ds(2 * a + 1, m, stride=NSA_CMP_STRIDE), :]
        g0.append(jnp.where(lane < HEAD_DIM, xe, pltpu.roll(xo, HEAD_DIM, 1)))
        g1.append(jnp.where(lane < HEAD_DIM, pltpu.roll(xe, HEAD_DIM, 1), xo))
    return jnp.concatenate([jnp.concatenate(g0, axis=1), jnp.concatenate(g1, axis=1)], axis=0).astype(BF16)


def _cmp_tokens(pre_a, pre_b, pe_ref, w_ref, b1, w2_ref, m):
    pc = _dot(pe_ref[...].astype(BF16), w_ref[...])
    c = pc[0:1, 0:256] + pc[1:2, 256:512] + b1
    hid = _gelu_tanh(pre_a[:, 0:256] + pre_b[:, 256:512] + c)
    hid = jnp.concatenate([hid[0:m], hid[m:2 * m]], axis=1).astype(BF16)
    return _dot(hid, w2_ref[...])


def _nsa_cmp_prompt_kernel(k_ref, v_ref, wk_ref, wv_ref, pek_ref, pev_ref, b1_ref, w2k_ref, w2v_ref, o_ref):
    nch = o_ref.shape[0]
    for i, (r_ref, w_ref, pe_ref, w2_ref) in enumerate(((k_ref, wk_ref, pek_ref, w2k_ref),
                                                         (v_ref, wv_ref, pev_ref, w2v_ref))):
        pre = _dot(_cmp_chunks(r_ref, nch), w_ref[...])
        nxt = pltpu.roll(pre, 2 * nch - 1, 0)
        o_ref[:, i * 128:(i + 1) * 128] = _cmp_tokens(pre, nxt, pe_ref, w_ref, b1_ref[i:i + 1, :], w2_ref, nch)


def _nsa_cmp_prompt(kv, wbig, pe_rows, b1, w2blk, batch, seq):
    nch = seq // NSA_CMP_STRIDE
    return pl.pallas_call(
        _nsa_cmp_prompt_kernel,
        out_shape=jax.ShapeDtypeStruct((batch * nch, 256), F32),
        grid=(batch,),
        in_specs=[pl.BlockSpec((seq, 128), lambda b: (b, 0)), pl.BlockSpec((seq, 128), lambda b: (b, 1)),
                  _full((1024, 512)), _full((1024, 512)), _full((8, 1024)), _full((8, 1024)),
                  _full((8, 256)), _full((512, 128)), _full((512, 128))],
        out_specs=pl.BlockSpec((nch, 256), lambda b: (b, 0)),
        compiler_params=_cparams("parallel"),
    )(kv, kv, wbig[0], wbig[1], pe_rows[0], pe_rows[1], b1, w2blk[0], w2blk[1])


def _flash_step(s, ok, v, m_ref, l_ref, acc_ref):
    s = jnp.where(ok, s, NEG)
    m_old = m_ref[...]
    m_new = jnp.maximum(m_old, jnp.max(s, axis=-1, keepdims=True))
    p = jnp.where(ok, jnp.exp(s - m_new), 0.0)
    alpha = jnp.exp(m_old - m_new)
    m_ref[...] = m_new
    l_ref[...] = alpha * l_ref[...] + jnp.sum(p, axis=-1, keepdims=True)
    acc_ref[...] = alpha * acc_ref[...] + _dot(p.astype(BF16), v)


def _flash_init(m_ref, l_ref, acc_ref):
    m_ref[...] = jnp.full(m_ref.shape, NEG, F32)
    l_ref[...] = jnp.zeros(l_ref.shape, F32)
    acc_ref[...] = jnp.zeros(acc_ref.shape, F32)


def _nsa_prompt_kernel(q_ref, ks_ref, vs_ref, kw_ref, vw_ref, kvc_ref, gate_ref, ovl_ref, e_ref,
                       o_ref, m_ref, l_ref, acc_ref, os_ref):
    i = pl.program_id(1)
    tq = q_ref.shape[0]
    rows = 16 * tq
    tk = 256
    q0 = i * tq
    qa = _slot_rows(q_ref, 8)
    ntok = kvc_ref.shape[0]

    kc = kvc_ref[:, 0:128].astype(BF16)
    vc = kvc_ref[:, 128:256].astype(BF16)
    qpos_c = q0 + (lax.broadcasted_iota(jnp.int32, (rows, ntok), 0) & (tq - 1))
    tok = lax.broadcasted_iota(jnp.int32, (rows, ntok), 1)
    ok = tok * NSA_CMP_STRIDE + (NSA_CMP_LEN - 1) <= qpos_c
    sc = jnp.where(ok, _dot_nt(qa, kc), NEG)
    mc = jnp.max(sc, axis=-1, keepdims=True)
    ec = jnp.where(ok, jnp.exp(sc - mc), 0.0)
    pc = ec / jnp.maximum(jnp.sum(ec, axis=-1, keepdims=True), TINY)
    o_c = _dot(pc.astype(BF16), vc)

    pg = jnp.sum(pc.reshape(2, 8, tq, ntok), axis=1).reshape(2 * tq, ntok)
    phi, plo = _split_bf16(pg)
    nsb = ks_ref.shape[0] // NSA_SLC_BLOCK
    imp_t = (_dot_nt(ovl_ref[...], phi) + _dot_nt(ovl_ref[...], plo))[0:nsb, :]
    qp2 = q0 + (lax.broadcasted_iota(jnp.int32, (nsb, 2 * tq), 1) & (tq - 1))
    own = qp2 >> 6
    jb = lax.broadcasted_iota(jnp.int32, (nsb, 2 * tq), 0)
    allowed = jb <= own
    forced = (jb == 0) | (jb == own) | (jb == own - 1)
    imp_t = jnp.where(forced, NSA_FORCE_SCORE, imp_t)
    imp_t = jnp.where(allowed, imp_t, -jnp.inf)
    sel_t = _topk_mask_t(imp_t, NSA_SLC_TOPN, nsb) & allowed
    bias = _bias_rows(sel_t).astype(BF16)
    qaug = jnp.concatenate([qa, jnp.concatenate([bias[0:tq]] * 8 + [bias[tq:2 * tq]] * 8, axis=0)], axis=1)

    qp_r = q0 + (lax.broadcasted_iota(jnp.int32, (rows, tk), 0) & (tq - 1))
    klr = lax.broadcasted_iota(jnp.int32, (rows, tk), 1)
    jd = (q0 + tq - 1) // tk
    offd = pl.multiple_of(jd * tk, tk)

    def slc_scores(off):
        kj = jnp.concatenate([ks_ref[pl.ds(off, tk), :], e_ref[pl.ds(off, tk), :]], axis=1)
        return _dot_nt(qaug, kj)

    _flash_first(jnp.where(offd + klr <= qp_r, slc_scores(offd), NEG),
                 lambda p: _dot(p, vs_ref[pl.ds(offd, tk), :]), m_ref, l_ref, acc_ref)

    def slc_body(j, carry):
        off = pl.multiple_of(j * tk, tk)
        _flash_next(slc_scores(off), lambda p: _dot(p, vs_ref[pl.ds(off, tk), :]), m_ref, l_ref, acc_ref)
        return carry

    lax.fori_loop(0, jd, slc_body, 0)
    os_ref[...] = acc_ref[...] / l_ref[...]

    def win_scores(off):
        dist = qp_r - (off + klr)
        return jnp.where((dist >= 0) & (dist < NSA_WINDOW), _dot_nt(qa, kw_ref[pl.ds(off, tk), :]), NEG)

    _flash_first(win_scores(offd), lambda p: _dot(p, vw_ref[pl.ds(offd, tk), :]), m_ref, l_ref, acc_ref)

    def win_body(j, carry):
        off = pl.multiple_of(j * tk, tk)
        _flash_next(win_scores(off), lambda p: _dot(p, vw_ref[pl.ds(off, tk), :]), m_ref, l_ref, acc_ref)
        return carry

    lax.fori_loop(jnp.maximum(q0 - (NSA_WINDOW - 1), 0) // tk, jd, win_body, 0)
    o_w = acc_ref[...] / l_ref[...]
    o_s = os_ref[...]

    gate = gate_ref[...]
    outs = []
    for h in range(16):
        r0 = h * tq
        outs.append(gate[:, 3 * h:3 * h + 1] * o_c[r0:r0 + tq, :]
                    + gate[:, 3 * h + 1:3 * h + 2] * o_s[r0:r0 + tq, :]
                    + gate[:, 3 * h + 2:3 * h + 3] * o_w[r0:r0 + tq, :])
    o_ref[...] = _slot_out(jnp.concatenate(outs, axis=0), 8, tq).astype(o_ref.dtype)


def _nsa_overlap(n_tok, n_blk, tok_shift, rows, cols):
    ovl = np.zeros((rows, cols), np.float32)
    t = np.arange(n_tok)[:, None] * NSA_CMP_STRIDE
    b = np.arange(n_blk)[None, :] * NSA_SLC_BLOCK
    ovl[tok_shift:tok_shift + n_tok, :n_blk] = ((t < b + NSA_SLC_BLOCK) & (t + NSA_CMP_LEN > b))
    return ovl


def _nsa_prompt_attn(q, kvb, kvc, gate, batch, seq):
    tq, tk = 128, 256
    nq = seq // tq
    nch = seq // NSA_CMP_STRIDE
    n_tok = (seq - NSA_CMP_LEN) // NSA_CMP_STRIDE + 1
    nsb = seq // NSA_SLC_BLOCK
    assert nch == LANES and nsb <= LANES
    ovl = jnp.asarray(_nsa_overlap(n_tok, nsb, 0, nch, LANES).T, BF16)
    e = np.zeros((seq, LANES), np.float32)
    e[np.arange(seq), np.arange(seq) // NSA_SLC_BLOCK] = 1.0
    e = jnp.asarray(e, BF16)
    seqcol = lambda c: pl.BlockSpec((seq, 128), lambda b, i: (b, c))
    return pl.pallas_call(
        _nsa_prompt_kernel,
        out_shape=jax.ShapeDtypeStruct((batch * seq, 1024), BF16),
        grid=(batch, nq),
        in_specs=[pl.BlockSpec((tq, 1024), lambda b, i: (b * nq + i, 0)),
                  seqcol(2), seqcol(3), seqcol(4), seqcol(5),
                  pl.BlockSpec((nch, 256), lambda b, i: (b, 0)),
                  pl.BlockSpec((tq, 128), lambda b, i: (b * nq + i, 0)),
                  _full((nch, LANES)), _full(e.shape)],
        out_specs=pl.BlockSpec((tq, 1024), lambda b, i: (b * nq + i, 0)),
        scratch_shapes=[pltpu.VMEM((16 * tq, 128), F32), pltpu.VMEM((16 * tq, 128), F32),
                        pltpu.VMEM((16 * tq, 128), F32), pltpu.VMEM((16 * tq, 128), F32)],
        compiler_params=_cparams("parallel", "arbitrary"),
    )(q, kvb, kvb, kvb, kvb, kvc, gate, ovl, e)


def _topk_mask_iter(score, k):
    rows, w = score.shape
    lane = lax.broadcasted_iota(jnp.int32, (rows, w), 1)
    taken = jnp.zeros((rows, w), jnp.int32)
    for _ in range(k):
        free = taken == 0
        cur = jnp.where(free, score, -jnp.inf)
        m = jnp.max(cur, axis=-1, keepdims=True)
        idx = jnp.min(jnp.where(free & (cur == m), lane, w), axis=-1, keepdims=True)
        taken = jnp.where(lane == idx, 1, taken)
    return taken > 0


def _sample_q_rows(q):
    lane = lax.broadcasted_iota(jnp.int32, (8, LANES), 1)
    ev = [jnp.where(lane < HEAD_DIM, q[:, s * LANES:(s + 1) * LANES], 0.0) for s in range(8)]
    od = [jnp.where(lane >= HEAD_DIM, q[:, s * LANES:(s + 1) * LANES], 0.0) for s in range(8)]
    return jnp.concatenate(ev + od, axis=0)


def _nsa_sample_cmp_kernel(pt_ref, q_ref, wk_ref, wv_ref, pek_ref, pev_ref, b1_ref, w2k_ref, w2v_ref,
                           ovl_ref, *refs, pps, past, blk_per_step):
    pages = refs[:pps]
    oc_ref, sel_ref = refs[pps], refs[pps + 1]
    kvc_ref, carry_ref, stage_ref = refs[pps + 2:]
    s = pl.program_id(1)
    m = pps * (PAGE // NSA_CMP_STRIDE)

    @pl.when(s == 0)
    def _():
        carry_ref[...] = jnp.zeros_like(carry_ref)

    row = lax.broadcasted_iota(jnp.int32, (2 * m, 512), 0)
    for i, (w_ref, pe_ref, w2_ref) in enumerate(((wk_ref, pek_ref, w2k_ref), (wv_ref, pev_ref, w2v_ref))):
        for t, pg in enumerate(pages):
            stage_ref[t * PAGE:(t + 1) * PAGE, :] = pg[i * 128:(i + 1) * 128, :].T
        pre = _dot(_cmp_chunks(stage_ref, m), w_ref[...])
        prev = pltpu.roll(pre, 1, 0)
        prev = jnp.where(row == 0, carry_ref[2 * i:2 * i + 1, :], prev)
        prev = jnp.where(row == m, carry_ref[2 * i + 1:2 * i + 2, :], prev)
        carry_ref[2 * i:2 * i + 1, :] = pre[m - 1:m, :]
        carry_ref[2 * i + 1:2 * i + 2, :] = pre[2 * m - 1:2 * m, :]
        kvc_ref[pl.ds(pl.multiple_of(s * m, m), m), i * 128:(i + 1) * 128] = _cmp_tokens(
            prev, pre, pe_ref, w_ref, b1_ref[i:i + 1, :], w2_ref, m)

    @pl.when(s == pl.num_programs(1) - 1)
    def _():
        nt = kvc_ref.shape[0]
        qa = _sample_q_rows(q_ref[...]).astype(BF16)
        kc = kvc_ref[:, 0:128].astype(BF16)
        vc = kvc_ref[:, 128:256].astype(BF16)
        r = lax.broadcasted_iota(jnp.int32, (128, nt), 1)
        qpos = past + (lax.broadcasted_iota(jnp.int32, (128, nt), 0) & 7)
        ok = (r >= 1) & ((r - 1) * NSA_CMP_STRIDE + (NSA_CMP_LEN - 1) <= qpos)
        sc = jnp.where(ok, _dot_nt(qa, kc), NEG)
        mc = jnp.max(sc, axis=-1, keepdims=True)
        ec = jnp.where(ok, jnp.exp(sc - mc), 0.0)
        pc = ec / jnp.maximum(jnp.sum(ec, axis=-1, keepdims=True), TINY)
        oc_ref[...] = _dot(pc.astype(BF16), vc)
        pg = jnp.sum(pc.reshape(2, 8, 8, nt), axis=1).reshape(16, nt)
        phi, plo = _split_bf16(pg)
        imp = _dot(phi, ovl_ref[...]) + _dot(plo, ovl_ref[...])
        nl = imp.shape[1]
        own = (past + (lax.broadcasted_iota(jnp.int32, (16, nl), 0) & 7)) >> 6
        jb = lax.broadcasted_iota(jnp.int32, (16, nl), 1)
        allowed = jb <= own
        forced = (jb == 0) | (jb == own) | (jb == own - 1)
        imp = jnp.where(forced, NSA_FORCE_SCORE, imp)
        imp = jnp.where(allowed, imp, -jnp.inf)
        sel = jnp.where(_topk_mask_iter(imp, NSA_SLC_TOPN) & allowed, 1.0, 0.0)
        lane = lax.broadcasted_iota(jnp.int32, (16, LANES), 1)
        for st in range(sel_ref.shape[0]):
            piece = sel if st == 0 else pltpu.roll(sel, nl - st * blk_per_step, 1)
            sel_ref[st] = jnp.where(lane < blk_per_step, piece[:, 0:LANES], 0.0)


def _nsa_sample_cmp(q, cache, page_table, w1h, pe_rows, b1, w2blk, past, pps, sel_pps):
    db, n_pages = page_table.shape
    nsteps = n_pages // pps
    nsel = n_pages // sel_pps
    nt = n_pages * (PAGE // NSA_CMP_STRIDE)
    n_tok = (past + 8 - NSA_CMP_LEN) // NSA_CMP_STRIDE + 1
    nsb = -(-(past + 8) // NSA_SLC_BLOCK)
    nl = -(-nsb // LANES) * LANES
    blk_per_step = sel_pps * PAGE // NSA_SLC_BLOCK
    assert n_tok == nt - 1 and blk_per_step <= LANES
    ovl = jnp.asarray(_nsa_overlap(n_tok, nsb, 1, nt, nl), BF16)

    def page_spec(t):
        return pl.BlockSpec((None, 256, PAGE), lambda b, s, pt: (pt[b, s * pps + t], 0, 0))

    grid_spec = pltpu.PrefetchScalarGridSpec(
        num_scalar_prefetch=1,
        grid=(db, nsteps),
        in_specs=[pl.BlockSpec((8, 1024), lambda b, s, pt: (b, 0)),
                  _full((1024, 512)), _full((1024, 512)), _full((8, 1024)), _full((8, 1024)),
                  _full((8, 256)), _full((512, 128)), _full((512, 128)), _full((nt, nl))]
                 + [page_spec(t) for t in range(pps)],
        out_specs=(pl.BlockSpec((128, 128), lambda b, s, pt: (b, 0)),
                   pl.BlockSpec((None, nsel, 16, LANES), lambda b, s, pt: (b, 0, 0, 0))),
        scratch_shapes=[pltpu.VMEM((nt, 256), F32), pltpu.VMEM((8, 512), F32),
                        pltpu.VMEM((pps * PAGE, 128), F32)],
    )
    return pl.pallas_call(
        functools.partial(_nsa_sample_cmp_kernel, pps=pps, past=past, blk_per_step=blk_per_step),
        out_shape=(jax.ShapeDtypeStruct((db * 128, 128), F32),
                   jax.ShapeDtypeStruct((db, nsel, 16, LANES), F32)),
        grid_spec=grid_spec,
        compiler_params=_cparams("parallel", "arbitrary"),
    )(page_table, q, w1h[0], w1h[1], pe_rows[0], pe_rows[1], b1, w2blk[0], w2blk[1], ovl,
      *([cache] * pps))


def _nsa_sample_attn_kernel(pt_ref, q_ref, sel_ref, oc_ref, kvn_ref, win_ref, gate_ref, e_ref, *refs,
                            pps, past):
    pages = refs[:pps]
    o_ref = refs[pps]
    qa_ref, m_ref, l_ref, acc_ref = refs[pps + 1:]
    s = pl.program_id(1)
    rows = 128

    @pl.when(s == 0)
    def _():
        qa_ref[...] = _sample_q_rows(q_ref[...]).astype(BF16)
        _flash_init(m_ref, l_ref, acc_ref)

    qa = qa_ref[...]
    kt = jnp.concatenate([pg[0:128, :] for pg in pages], axis=1).astype(BF16)
    vt = jnp.concatenate([pg[128:256, :] for pg in pages], axis=1).astype(BF16)
    sel = sel_ref[...]
    selrows = jnp.concatenate([sel[0:8, :]] * 8 + [sel[8:16, :]] * 8, axis=0).astype(BF16)
    ok = _dot(selrows, e_ref[...]) > 0.5
    _flash_masked(_dot(qa, kt), ok, lambda p: _dot_nt(p, vt), m_ref, l_ref, acc_ref)

    @pl.when(s == pl.num_programs(1) - 1)
    def _():
        col = lax.broadcasted_iota(jnp.int32, (rows, LANES), 1)
        qi = lax.broadcasted_iota(jnp.int32, (rows, LANES), 0) & 7
        kn = kvn_ref[256:384, :].astype(BF16)
        vn = kvn_ref[384:512, :].astype(BF16)
        _flash_masked(_dot(qa, kn), col <= qi, lambda p: _dot_nt(p, vn), m_ref, l_ref, acc_ref)
        o_s = acc_ref[...] / l_ref[...]
        wb = win_ref.shape[1]
        kw = jnp.concatenate([win_ref[0:128, :], kvn_ref[512:640, :]], axis=1).astype(BF16)
        vw = jnp.concatenate([win_ref[128:256, :], kvn_ref[640:768, :]], axis=1).astype(BF16)
        nw = wb + LANES
        c = lax.broadcasted_iota(jnp.int32, (rows, nw), 1)
        qpos = past + (lax.broadcasted_iota(jnp.int32, (rows, nw), 0) & 7)
        wpos = past - wb + c
        dist = qpos - wpos
        okw = (dist >= 0) & (dist < NSA_WINDOW) & (wpos >= 0) & (c < wb + 8)
        sw = jnp.where(okw, _dot(qa, kw), NEG)
        mw = jnp.max(sw, axis=-1, keepdims=True)
        pw = jnp.where(okw, jnp.exp(sw - mw), 0.0)
        o_w = _dot_nt(pw.astype(BF16), vw) / jnp.maximum(jnp.sum(pw, axis=-1, keepdims=True), TINY)
        o_c = oc_ref[...]
        gate = gate_ref[...]
        lane8 = lax.broadcasted_iota(jnp.int32, (8, LANES), 1)
        hs = []
        for h in range(16):
            r0 = h * 8
            hs.append(gate[:, 3 * h:3 * h + 1] * o_c[r0:r0 + 8, :]
                      + gate[:, 3 * h + 1:3 * h + 2] * o_s[r0:r0 + 8, :]
                      + gate[:, 3 * h + 2:3 * h + 3] * o_w[r0:r0 + 8, :])
        o_ref[...] = jnp.concatenate([jnp.where(lane8 < HEAD_DIM, hs[sl], hs[8 + sl]) for sl in range(8)],
                                     axis=1)


def _nsa_sample_attn(q, sel, o_c, kvn, state_win, gate, cache, page_table, past, pps):
    db, n_pages = page_table.shape
    nsteps = n_pages // pps
    wb = state_win.shape[2]
    nk = pps * PAGE
    e = np.zeros((LANES, nk), np.float32)
    e[np.arange(nk) // NSA_SLC_BLOCK, np.arange(nk)] = 1.0

    def page_spec(t):
        return pl.BlockSpec((None, 256, PAGE), lambda b, s, pt: (pt[b, s * pps + t], 1, 0))

    grid_spec = pltpu.PrefetchScalarGridSpec(
        num_scalar_prefetch=1,
        grid=(db, nsteps),
        in_specs=[pl.BlockSpec((8, 1024), lambda b, s, pt: (b, 0)),
                  pl.BlockSpec((None, None, 16, LANES), lambda b, s, pt: (b, s, 0, 0)),
                  pl.BlockSpec((128, 128), lambda b, s, pt: (b, 0)),
                  pl.BlockSpec((None, 768, LANES), lambda b, s, pt: (b, 0, 0)),
                  pl.BlockSpec((None, 256, wb), lambda b, s, pt: (b, 0, 0)),
                  pl.BlockSpec((8, 128), lambda b, s, pt: (b, 0)),
                  _full((LANES, nk))]
                 + [page_spec(t) for t in range(pps)],
        out_specs=pl.BlockSpec((8, 1024), lambda b, s, pt: (b, 0)),
        scratch_shapes=[pltpu.VMEM((128, 128), BF16), pltpu.VMEM((128, 128), F32), pltpu.VMEM((128, 128), F32),
                        pltpu.VMEM((128, 128), F32)],
    )
    return pl.pallas_call(
        functools.partial(_nsa_sample_attn_kernel, pps=pps, past=past),
        out_shape=jax.ShapeDtypeStruct((db * 8, 1024), F32),
        grid_spec=grid_spec,
        compiler_params=_cparams("parallel", "arbitrary"),
    )(page_table, q, sel, o_c, kvn, state_win, gate, jnp.asarray(e, BF16), *([cache] * pps))


def _nsa_layer(xp, xs, cache, state_win, page_table, ln_g, w_in, qn, kn, cmp_pe, cmp_w1, cmp_b1, cmp_w2,
               w_o, batch, seq, past_len):
    perm = _slot_perm(NSA_KV_HEADS)
    w_in_p = jnp.concatenate([w_in[:, :1024][:, perm], w_in[:, 1024:],
                              jnp.zeros((w_in.shape[0], 1920 - w_in.shape[1]), F32)], axis=1).astype(BF16)
    w_o_p = w_o[perm, :].astype(BF16)
    g = ln_g.reshape(1, -1)
    wbig, pe_rows, w2blk = _nsa_cmp_weights(cmp_pe, cmp_w1, cmp_w2)
    b1 = jnp.concatenate([cmp_b1, jnp.zeros((6, NSA_CMP_HIDDEN), F32)], axis=0)
    dec = 8
    q, kv, kvb, gate, kvt = _nsa_proj(xp, g, w_in_p, qn, kn, np.arange(seq), 512, BF16)
    kvc = _nsa_cmp_prompt(kv, wbig, pe_rows, b1, w2blk, batch, seq)
    o = _nsa_prompt_attn(q, kvb, kvc, gate, batch, seq)
    yp = _outproj(xp, o, w_o_p, 512)
    ns = xs.shape[0]
    db, n_pages = page_table.shape
    pps = min(16, n_pages)
    pos_s = np.tile(past_len + np.arange(dec), ns // dec)
    qs, kvs, _, gs, _ = _nsa_proj(xs, g, w_in_p, qn, kn, pos_s, ns, F32)
    cache_t = _pages_t(cache)
    win_t = jnp.swapaxes(state_win.reshape(db, state_win.shape[1], 256), 1, 2)
    o_c, sel = _nsa_sample_cmp(qs, cache_t, page_table, wbig, pe_rows, b1, w2blk, past_len,
                               min(32, n_pages), pps)
    os_ = _nsa_sample_attn(qs, sel, o_c, _rows_t(kvs, db), win_t, gs, cache_t, page_table, past_len, pps)
    ys = _outproj(xs, os_, w_o_p, ns)
    return yp, ys, kvt, kvs


MLA_QK = MLA_NOPE + MLA_ROPE


def _mla_proj_kernel(x_ref, g_ref, w_ref, wuq_ref, gm_ref, c_ref, cosq_ref, sinq_ref, cosk_ref, sink_ref,
                     q_ref, lat_ref, latt_ref):
    h = _rms_rows(x_ref[...], g_ref[...]).astype(BF16)
    cq = _rms_rows(_dot(h, w_ref[:, 0:MLA_Q_LORA]), c_ref[0:1, 0:MLA_Q_LORA]).astype(BF16)
    ckv = _rms_rows(_dot(h, w_ref[:, MLA_Q_LORA:MLA_Q_LORA + MLA_KV_LORA]), c_ref[1:2, 0:MLA_KV_LORA])
    a = _dot(h, w_ref[:, 640:768])
    ms = jnp.sum(a * a, axis=-1, keepdims=True) * (1.0 / MLA_ROPE)
    kpe = a * lax.rsqrt(ms + NORM_EPS) * c_ref[2:3, 0:128]
    kpe = _rope(kpe, cosk_ref[...], sink_ref[...], c_ref[3:4, 0:128], MLA_ROPE // 2)
    lat_ref[:, 0:MLA_KV_LORA] = ckv
    lat_ref[:, MLA_KV_LORA:MLA_KV_LORA + MLA_ROPE] = kpe[:, 0:MLA_ROPE]
    latt_ref[0:MLA_KV_LORA, :] = ckv.T
    latt_ref[MLA_KV_LORA:MLA_KV_LORA + MLA_ROPE, :] = kpe.T[0:MLA_ROPE, :]
    gm = gm_ref[...]
    qg, up = c_ref[4:5, 0:256], c_ref[5:6, 0:256]
    cos, sin = cosq_ref[...], sinq_ref[...]
    for c in range(8):
        a = _dot(cq, wuq_ref[:, c * 256:(c + 1) * 256])
        y = _rope(_group_norm(a, gm, qg), cos, sin, up, MLA_ROPE // 2)
        q_ref[:, c * 256:(c + 1) * 256] = (y * MLA_QK ** -0.5).astype(q_ref.dtype)


def _mla_cat_groups(width):
    g = []
    for lo in range(0, width, LANES):
        g += [(lo, lo + MLA_NOPE), (lo + MLA_NOPE, lo + MLA_QK)]
    return g


def _mla_cat_cols():
    idx = -np.ones((16, LANES), np.int64)
    for h in range(16):
        idx[h, :MLA_QK] = h * MLA_QK + np.arange(MLA_QK)
    return idx.reshape(-1)


def _take_cols(w, idx):
    wz = jnp.concatenate([w, jnp.zeros((w.shape[0], 1), w.dtype)], axis=1)
    return wz[:, np.where(idx < 0, w.shape[1], idx)]


def _mla_proj(x, g, w_dqkv, g_q, g_kv, w_uq, qn, kn, pos, tn, qdtype):
    n, d = x.shape
    period = pos.shape[0]
    w = jnp.concatenate([w_dqkv, jnp.zeros((d, 768 - w_dqkv.shape[1]), F32)], axis=1).astype(BF16)
    wuq = _take_cols(w_uq, _mla_cat_cols()).astype(BF16)
    cosq, sinq, upq = _rope_tables(pos, 256, [(MLA_NOPE, MLA_ROPE), (LANES + MLA_NOPE, MLA_ROPE)])
    cosk, sink, upk = _rope_tables(pos, 128, [(0, MLA_ROPE)])
    consts = jnp.zeros((8, 384), F32)
    consts = consts.at[0, :].set(g_q).at[1, 0:256].set(g_kv).at[2, 0:MLA_ROPE].set(kn[MLA_NOPE:])
    consts = consts.at[3, 0:128].set(jnp.asarray(upk[0])).at[5, 0:256].set(jnp.asarray(upq[0]))
    qgain = jnp.concatenate([qn, jnp.zeros((LANES - MLA_QK,), F32)])
    consts = consts.at[4, 0:256].set(jnp.tile(qgain, 2))
    gm = jnp.asarray(_group_mean_matrix(_mla_cat_groups(256), 256), BF16)
    nper = period // tn
    tabq = pl.BlockSpec((tn, 256), lambda i: (i % nper, 0))
    tabk = pl.BlockSpec((tn, 128), lambda i: (i % nper, 0))
    row = lambda wd: pl.BlockSpec((tn, wd), lambda i: (i, 0))
    return pl.pallas_call(
        _mla_proj_kernel,
        out_shape=(jax.ShapeDtypeStruct((n, 2048), qdtype), jax.ShapeDtypeStruct((n, 288), F32),
                   jax.ShapeDtypeStruct((n // period, 288, period), F32)),
        grid=(n // tn,),
        in_specs=[row(d), _full((1, d)), _full(w.shape), _full(wuq.shape), _full((256, 256)), _full((8, 384)),
                  tabq, tabq, tabk, tabk],
        out_specs=(row(2048), row(288), _t_spec(288, tn, nper)),
        compiler_params=_cparams("parallel"),
    )(x, g, w, wuq, gm, consts, jnp.asarray(cosq), jnp.asarray(sinq), jnp.asarray(cosk), jnp.asarray(sink))


def _mla_expand_kernel(lat_ref, wk_ref, wv_ref, gm_ref, c_ref, place_ref, k_ref, v_ref):
    ckv = lat_ref[:, 0:MLA_KV_LORA].astype(BF16)
    kpe = _dot(lat_ref[:, MLA_KV_LORA:MLA_KV_LORA + MLA_ROPE].astype(BF16), place_ref[...])
    gm = gm_ref[...]
    kg = c_ref[0:1, :]
    for c in range(8):
        e = _dot(ckv, wk_ref[:, c * 256:(c + 1) * 256])
        k_ref[:, c * 256:(c + 1) * 256] = (_group_norm(e, gm, kg) + kpe).astype(BF16)
    v_ref[...] = _dot(ckv, wv_ref[...]).astype(BF16)


def _mla_split_ukv(w_ukv):
    w = w_ukv.reshape(MLA_KV_LORA, 16, MLA_NOPE + MLA_V)
    return w[:, :, :MLA_NOPE].reshape(MLA_KV_LORA, 16 * MLA_NOPE), w[:, :, MLA_NOPE:].reshape(MLA_KV_LORA, 16 * MLA_V)


def _mla_expand(lat, w_ukv, kn, tn):
    n = lat.shape[0]
    wk_nat, wv = _mla_split_ukv(w_ukv)
    idx = -np.ones((16, LANES), np.int64)
    for h in range(16):
        idx[h, :MLA_NOPE] = h * MLA_NOPE + np.arange(MLA_NOPE)
    wk = _take_cols(wk_nat, idx.reshape(-1)).astype(BF16)
    gm = jnp.asarray(_group_mean_matrix([(lo, lo + MLA_NOPE) for lo in (0, LANES)], 256), BF16)
    kgain = jnp.concatenate([kn[:MLA_NOPE], jnp.zeros((LANES - MLA_NOPE,), F32)])
    consts = jnp.zeros((8, 256), F32).at[0].set(jnp.tile(kgain, 2))
    place = np.zeros((MLA_ROPE, 256), np.float32)
    for lo in (MLA_NOPE, LANES + MLA_NOPE):
        place[np.arange(MLA_ROPE), lo + np.arange(MLA_ROPE)] = 1.0
    row = lambda wd: pl.BlockSpec((tn, wd), lambda i: (i, 0))
    return pl.pallas_call(
        _mla_expand_kernel,
        out_shape=(jax.ShapeDtypeStruct((n, 2048), BF16), jax.ShapeDtypeStruct((n, 1024), BF16)),
        grid=(n // tn,),
        in_specs=[row(288), _full(wk.shape), _full((256, 1024)), _full((256, 256)), _full((8, 256)),
                  _full((MLA_ROPE, 256))],
        out_specs=(row(2048), row(1024)),
        compiler_params=_cparams("parallel"),
    )(lat, wk, wv.astype(BF16), gm, consts, jnp.asarray(place, BF16))


def _mla_prompt_kernel(q_ref, k_ref, v_ref, o_ref, m_ref, l_ref, acc_ref):
    i = pl.program_id(2)
    tq = q_ref.shape[0]
    tk = tq
    q0 = q_ref[:, 0:128]
    q1 = q_ref[:, 128:256]

    def scores(off):
        return jnp.concatenate([_dot_nt(q0, k_ref[pl.ds(off, tk), 0:128]),
                                _dot_nt(q1, k_ref[pl.ds(off, tk), 128:256])], axis=0)

    offd = pl.multiple_of(i * tk, tk)
    qi = lax.broadcasted_iota(jnp.int32, (2 * tq, tk), 0) & (tq - 1)
    ki = lax.broadcasted_iota(jnp.int32, (2 * tq, tk), 1)
    _flash_first(jnp.where(ki <= qi, scores(offd), NEG), lambda p: _dot(p, v_ref[pl.ds(offd, tk), :]),
                 m_ref, l_ref, acc_ref)

    def body(j, carry):
        off = pl.multiple_of(j * tk, tk)
        _flash_next(scores(off), lambda p: _dot(p, v_ref[pl.ds(off, tk), :]), m_ref, l_ref, acc_ref)
        return carry

    lax.fori_loop(0, i, body, 0)
    o = acc_ref[...] / l_ref[...]
    lane = lax.broadcasted_iota(jnp.int32, (tq, LANES), 1)
    o_ref[...] = jnp.where(lane < MLA_V, o[0:tq, :], o[tq:2 * tq, :]).astype(o_ref.dtype)


def _mla_prompt_attn(q, k, v, batch, seq):
    tq = 512
    nq = seq // tq
    return pl.pallas_call(
        _mla_prompt_kernel,
        out_shape=jax.ShapeDtypeStruct((batch * seq, 1024), BF16),
        grid=(batch, 8, nq),
        in_specs=[pl.BlockSpec((tq, 256), lambda b, p, i: (b * nq + i, p)),
                  pl.BlockSpec((seq, 256), lambda b, p, i: (b, p)),
                  pl.BlockSpec((seq, 128), lambda b, p, i: (b, p))],
        out_specs=pl.BlockSpec((tq, 128), lambda b, p, i: (b * nq + i, p)),
        scratch_shapes=[pltpu.VMEM((2 * tq, 128), F32), pltpu.VMEM((2 * tq, 128), F32),
                        pltpu.VMEM((2 * tq, 128), F32)],
        compiler_params=_cparams("parallel", "parallel", "arbitrary"),
    )(q, k, v)


def _mla_sample_kernel(pt_ref, q_ref, latn_ref, wk_ref, wkt_ref, wv_ref, kg_ref, *refs, pps):
    pages = refs[:pps]
    o_ref = refs[pps]
    qt_ref, qpe_ref, m_ref, l_ref, acc_ref = refs[pps + 1:]
    s = pl.program_id(1)
    lane8 = lax.broadcasted_iota(jnp.int32, (8, LANES), 1)

    @pl.when(s == 0)
    def _():
        z = jnp.zeros((8, LANES), F32)
        qg_rows, pe_rows = [], []
        for h in range(16):
            ch = q_ref[:, h * LANES:(h + 1) * LANES]
            nope = jnp.where(lane8 < MLA_NOPE, ch, 0.0)
            if h % 2:
                nope = pltpu.roll(nope, MLA_NOPE, 1)
            qg_rows.append(jnp.concatenate([z] * (h // 2) + [nope] + [z] * (7 - h // 2), axis=1))
            pe_rows.append(pltpu.roll(ch, LANES - MLA_NOPE, 1)[:, 0:MLA_ROPE])
        qg = (jnp.concatenate(qg_rows, axis=0) * kg_ref[...]).astype(BF16)
        qt_ref[...] = _dot_nt(qg, wk_ref[...]).astype(BF16)
        qpe_ref[...] = jnp.concatenate(pe_rows, axis=0).astype(BF16)
        _flash_init(m_ref, l_ref, acc_ref)

    def update(lat_t, ok):
        ckv = lat_t[0:MLA_KV_LORA, :].astype(BF16)
        kpe = lat_t[MLA_KV_LORA:MLA_KV_LORA + MLA_ROPE, :].astype(BF16)
        n = lat_t.shape[1]
        e = _dot(wkt_ref[...], ckv)
        ms = jnp.sum((e * e).reshape(16, MLA_NOPE, n), axis=1) * (1.0 / MLA_NOPE)
        rs = lax.rsqrt(ms + NORM_EPS)
        rs = jnp.broadcast_to(rs[:, None, :], (16, 8, n)).reshape(LANES, n)
        sc = _dot(qt_ref[...], ckv) * rs + _dot(qpe_ref[...], kpe)
        pv = lambda p: _dot_nt(p, ckv)
        if ok is None:
            _flash_next(sc, pv, m_ref, l_ref, acc_ref)
        else:
            _flash_masked(sc, ok, pv, m_ref, l_ref, acc_ref)

    update(jnp.concatenate([pg[...] for pg in pages], axis=1), None)

    @pl.when(s == pl.num_programs(1) - 1)
    def _():
        rho = lax.broadcasted_iota(jnp.int32, (LANES, LANES), 0)
        t = lax.broadcasted_iota(jnp.int32, (LANES, LANES), 1)
        update(latn_ref[...], t <= (rho & 7))
        olat = (acc_ref[...] / _rep(l_ref[...], 2)).astype(BF16)
        ofull = _dot(olat, wv_ref[...])
        r_head = lax.broadcasted_iota(jnp.int32, (LANES, 1024), 0) >> 3
        c_head = lax.broadcasted_iota(jnp.int32, (LANES, 1024), 1) >> 6
        ofull = jnp.where(r_head == c_head, ofull, 0.0)
        out = ofull[0:8, :]
        for h in range(1, 16):
            out = out + ofull[h * 8:(h + 1) * 8, :]
        o_ref[...] = out


def _mla_sample_attn(q, latn_t, cache_t, page_table, w_ukv, kn, pps):
    db, n_pages = page_table.shape
    nsteps = n_pages // pps
    wk, wv = _mla_split_ukv(w_ukv)
    wk = wk.astype(BF16)
    kg = jnp.tile(kn[:MLA_NOPE], 16).reshape(1, 1024)

    def page_spec(t):
        return pl.BlockSpec((None, 288, PAGE), lambda b, s, pt: (pt[b, s * pps + t], 0, 0))

    grid_spec = pltpu.PrefetchScalarGridSpec(
        num_scalar_prefetch=1,
        grid=(db, nsteps),
        in_specs=[pl.BlockSpec((8, 2048), lambda b, s, pt: (b, 0)),
                  pl.BlockSpec((None, 288, LANES), lambda b, s, pt: (b, 0, 0)),
                  _full((256, 1024)), _full((1024, 256)), _full((256, 1024)), _full((1, 1024))]
                 + [page_spec(t) for t in range(pps)],
        out_specs=pl.BlockSpec((8, 1024), lambda b, s, pt: (b, 0)),
        scratch_shapes=[pltpu.VMEM((128, 256), BF16), pltpu.VMEM((128, MLA_ROPE), BF16),
                        pltpu.VMEM((128, 128), F32), pltpu.VMEM((128, 128), F32), pltpu.VMEM((128, 256), F32)],
    )
    return pl.pallas_call(
        functools.partial(_mla_sample_kernel, pps=pps),
        out_shape=jax.ShapeDtypeStruct((db * 8, 1024), F32),
        grid_spec=grid_spec,
        compiler_params=_cparams("parallel", "arbitrary"),
    )(page_table, q, latn_t, wk, wk.T, wv.astype(BF16), kg, *([cache_t] * pps))


def _mla_layer(xp, xs, cache, page_table, ln_g, w_dqkv, g_q, g_kv, w_uq, w_ukv, qn, kn, w_o,
               batch, seq, past_len):
    g = ln_g.reshape(1, -1)
    w_o_b = w_o.astype(BF16)
    dec = 8
    q, lat, latt = _mla_proj(xp, g, w_dqkv, g_q, g_kv, w_uq, qn, kn, np.arange(seq), 512, BF16)
    k, v = _mla_expand(lat, w_ukv, kn, 512)
    o = _mla_prompt_attn(q, k, v, batch, seq)
    yp = _outproj(xp, o, w_o_b, 512)
    ns = xs.shape[0]
    pos_s = np.tile(past_len + np.arange(dec), ns // dec)
    qs, lats, _ = _mla_proj(xs, g, w_dqkv, g_q, g_kv, w_uq, qn, kn, pos_s, ns, F32)
    os_ = _mla_sample_attn(qs, _rows_t(lats, page_table.shape[0]), _pages_t(cache), page_table, w_ukv, kn,
                           min(8, page_table.shape[1]))
    ys = _outproj(xs, os_, w_o_b, ns)
    return yp, ys, latt, lats


def kernel(x_prompt, x_sample, cache_kv_0, cache_kv_1, state_win_1, cache_lat_2, cache_kv_3, page_table, ln1_g, ln2_g, mlp_w1, mlp_w2, moba_w_in_0, moba_qn_0, moba_kn_0, moba_w_o_0, nsa_w_in_1, nsa_qn_1, nsa_kn_1, nsa_cmp_pe_1, nsa_cmp_w1_1, nsa_cmp_b1_1, nsa_cmp_w2_1, nsa_w_o_1, mla_w_dqkv_2, mla_g_q_2, mla_g_kv_2, mla_w_uq_2, mla_w_ukv_2, mla_qn_2, mla_kn_2, mla_w_o_2, moba_w_in_3, moba_qn_3, moba_kn_3, moba_w_o_3):
    batch, seq, d = x_prompt.shape
    db, dec, _ = x_sample.shape
    past = page_table.shape[1] * PAGE
    assert dec == 8 and seq % 512 == 0
    xp = x_prompt.reshape(batch * seq, d)
    xs = x_sample.reshape(db * dec, d)

    def mlp(i, xp, xs):
        w1 = mlp_w1[i].astype(BF16)
        w2 = mlp_w2[i].astype(BF16)
        g = ln2_g[i].reshape(1, d)
        return _mlp(xp, g, w1, w2, 512), _mlp(xs, g, w1, w2, db * dec)

    xp, xs, kv0_p, kv0_s = _moba_layer(xp, xs, cache_kv_0, page_table, ln1_g[0], moba_w_in_0, moba_qn_0,
                                       moba_kn_0, moba_w_o_0, batch, seq, past)
    xp, xs = mlp(0, xp, xs)
    xp, xs, kv1_p, kv1_s = _nsa_layer(xp, xs, cache_kv_1, state_win_1, page_table, ln1_g[1], nsa_w_in_1,
                                      nsa_qn_1, nsa_kn_1, nsa_cmp_pe_1, nsa_cmp_w1_1, nsa_cmp_b1_1,
                                      nsa_cmp_w2_1, nsa_w_o_1, batch, seq, past)
    xp, xs = mlp(1, xp, xs)
    xp, xs, lat_p, lat_s = _mla_layer(xp, xs, cache_lat_2, page_table, ln1_g[2], mla_w_dqkv_2, mla_g_q_2,
                                      mla_g_kv_2, mla_w_uq_2, mla_w_ukv_2, mla_qn_2, mla_kn_2, mla_w_o_2,
                                      batch, seq, past)
    xp, xs = mlp(2, xp, xs)
    xp, xs, kv3_p, kv3_s = _moba_layer(xp, xs, cache_kv_3, page_table, ln1_g[3], moba_w_in_3, moba_qn_3,
                                       moba_kn_3, moba_w_o_3, batch, seq, past)
    xp, xs = mlp(3, xp, xs)

    def rows_major(x_t, *feat):
        return jnp.moveaxis(x_t.reshape((batch,) + feat + (x_t.shape[-1],)), -1, 1)

    wb_p = min(NSA_WINDOW, seq)
    win_new = kv1_s[:, 512:768].reshape(db, dec, 2, NSA_KV_HEADS, HEAD_DIM)
    win_s = jnp.concatenate([state_win_1, win_new], axis=1)[:, dec:]
    return (xp.reshape(batch, seq, d), xs.reshape(db, dec, d),
            rows_major(kv0_p, 2, MOBA_KV_HEADS, HEAD_DIM),
            rows_major(kv1_p[:, :512], 4, NSA_KV_HEADS, HEAD_DIM),
            rows_major(kv1_p[:, 512:768, seq - wb_p:], 2, NSA_KV_HEADS, HEAD_DIM),
            rows_major(lat_p, MLA_KV_LORA + MLA_ROPE),
            rows_major(kv3_p, 2, MOBA_KV_HEADS, HEAD_DIM),
            kv0_s.reshape(db, dec, 2, MOBA_KV_HEADS, HEAD_DIM),
            kv1_s[:, :512].reshape(db, dec, 4, NSA_KV_HEADS, HEAD_DIM),
            win_s,
            lat_s.reshape(db, dec, MLA_KV_LORA + MLA_ROPE),
            kv3_s.reshape(db, dec, 2, MOBA_KV_HEADS, HEAD_DIM))
```

```python
import functools

import numpy as np
import jax
import jax.numpy as jnp
from jax import lax
from jax.experimental import pallas as pl
from jax.experimental.pallas import tpu as pltpu

F32 = jnp.float32
BF16 = jnp.bfloat16

HEAD_DIM = 64
ROPE_THETA = 10000.0
NORM_EPS = 1e-6
PAGE = 128
MOBA_KV_HEADS = 4
MOBA_BLOCK = 256
MOBA_TOPK = 3
NSA_KV_HEADS = 2
NSA_CMP_LEN = 32
NSA_CMP_STRIDE = 16
NSA_CMP_HIDDEN = 256
NSA_SLC_BLOCK = 64
NSA_SLC_TOPN = 16
NSA_WINDOW = 512
NSA_FORCE_SCORE = 1e9
MLA_Q_LORA = 384
MLA_KV_LORA = 256
MLA_NOPE = 64
MLA_ROPE = 32
MLA_V = 64

LANES = 128
VMEM_LIMIT_BYTES = 56 * 1024 * 1024
NEG = -1e30
TINY = float(np.finfo(np.float32).tiny)


def _cparams(*sem):
    return pltpu.CompilerParams(dimension_semantics=sem, vmem_limit_bytes=VMEM_LIMIT_BYTES)


def _dot(a, b):
    return jnp.dot(a, b, preferred_element_type=F32)


def _dot_nt(a, b):
    return lax.dot_general(a, b, (((1,), (1,)), ((), ())), preferred_element_type=F32)


def _dot_tn(a, b):
    return lax.dot_general(a, b, (((0,), (0,)), ((), ())), preferred_element_type=F32)


def _split_bf16(x):
    hi = x.astype(BF16)
    lo = (x - hi.astype(F32)).astype(BF16)
    return hi, lo


def _full(shape):
    n = len(shape)
    return pl.BlockSpec(shape, lambda *_: (0,) * n)


def _slot_perm(kv_heads, n_heads=16):
    grp = n_heads // kv_heads
    cols = []
    for s in range(n_heads // 2):
        p, r = divmod(s, grp)
        for h in ((2 * p) * grp + r, (2 * p + 1) * grp + r):
            cols.append(h * HEAD_DIM + np.arange(HEAD_DIM))
    return np.concatenate(cols)


def _slot_heads(kv_heads, n_heads=16):
    grp = n_heads // kv_heads
    out = []
    for s in range(n_heads // 2):
        p, r = divmod(s, grp)
        out.append(((2 * p) * grp + r, (2 * p + 1) * grp + r))
    return out


def _group_mean_matrix(groups, width):
    m = np.zeros((width, width), np.float32)
    for lo, hi in groups:
        m[lo:hi, lo:hi] = 1.0 / (hi - lo)
    return m


def _rope_tables(pos, width, segs):
    pos = np.asarray(pos, np.float64)
    cos = np.ones((pos.shape[0], width), np.float64)
    sin = np.zeros((pos.shape[0], width), np.float64)
    up = np.zeros((1, width), np.float32)
    for lo, dim in segs:
        half = dim // 2
        inv = ROPE_THETA ** (-np.arange(half, dtype=np.float64) / half)
        ang = pos[:, None] * inv[None, :]
        cos[:, lo:lo + half] = np.cos(ang)
        cos[:, lo + half:lo + dim] = np.cos(ang)
        sin[:, lo:lo + half] = -np.sin(ang)
        sin[:, lo + half:lo + dim] = np.sin(ang)
        up[:, lo:lo + half] = 1.0
    return cos.astype(np.float32), sin.astype(np.float32), up


def _head64_segs(width):
    return [(lo, HEAD_DIM) for lo in range(0, width, HEAD_DIM)]


def _rms_rows(x, g):
    ms = jnp.mean(x * x, axis=-1, keepdims=True)
    return x * lax.rsqrt(ms + NORM_EPS) * g


def _group_norm(a, gm, gain):
    ms = _dot((a * a).astype(BF16), gm)
    return a * lax.rsqrt(ms + NORM_EPS) * gain


def _rope(y, cos, sin, up, shift):
    w = y.shape[-1]
    hi = pltpu.roll(y, w - shift, 1)
    lo = pltpu.roll(y, shift, 1)
    partner = jnp.where(up > 0.5, hi, lo)
    return y * cos + partner * sin


def _gelu_tanh(x):
    return 0.5 * x * (1.0 + jnp.tanh(0.7978845608028654 * (x + 0.044715 * (x * x * x))))


def _outproj_kernel(x_ref, o_ref, w_ref, y_ref):
    y_ref[...] = x_ref[...] + _dot(o_ref[...].astype(BF16), w_ref[...])


def _outproj(x, o, w, tn):
    n, d = x.shape
    k = o.shape[1]
    return pl.pallas_call(
        _outproj_kernel,
        out_shape=jax.ShapeDtypeStruct((n, d), F32),
        grid=(n // tn,),
        in_specs=[pl.BlockSpec((tn, d), lambda i: (i, 0)),
                  pl.BlockSpec((tn, k), lambda i: (i, 0)),
                  _full((k, d))],
        out_specs=pl.BlockSpec((tn, d), lambda i: (i, 0)),
        compiler_params=_cparams("parallel"),
    )(x, o, w)


def _mlp_kernel(x_ref, g_ref, w1_ref, w2_ref, y_ref, *, ff_chunk):
    x = x_ref[...]
    h = _rms_rows(x, g_ref[...]).astype(BF16)
    acc = x
    for c in range(w1_ref.shape[1] // ff_chunk):
        u = _dot(h, w1_ref[:, c * ff_chunk:(c + 1) * ff_chunk])
        u = jnp.maximum(u, 0.0)
        acc = acc + _dot((u * u).astype(BF16), w2_ref[c * ff_chunk:(c + 1) * ff_chunk, :])
    y_ref[...] = acc


def _mlp(x, g, w1, w2, tn):
    n, d = x.shape
    ff = w1.shape[1]
    return pl.pallas_call(
        functools.partial(_mlp_kernel, ff_chunk=1024),
        out_shape=jax.ShapeDtypeStruct((n, d), F32),
        grid=(n // tn,),
        in_specs=[pl.BlockSpec((tn, d), lambda i: (i, 0)),
                  _full((1, d)), _full((d, ff)), _full((ff, d))],
        out_specs=pl.BlockSpec((tn, d), lambda i: (i, 0)),
        compiler_params=_cparams("parallel"),
    )(x, g, w1, w2)


def _moba_proj_kernel(x_ref, g_ref, w_ref, gm_ref, c_ref, cos_ref, sin_ref, q_ref, kv_ref, kvb_ref, kvt_ref):
    h = _rms_rows(x_ref[...], g_ref[...]).astype(BF16)
    gm = gm_ref[...]
    cos, sin = cos_ref[...], sin_ref[...]
    qg, kg, up = c_ref[0:1, :], c_ref[1:2, :], c_ref[2:3, :]
    for c in range(4):
        a = _dot(h, w_ref[:, c * 256:(c + 1) * 256])
        y = _rope(_group_norm(a, gm, qg), cos, sin, up, HEAD_DIM // 2)
        q_ref[:, c * 256:(c + 1) * 256] = (y * HEAD_DIM ** -0.5).astype(q_ref.dtype)
    a = _dot(h, w_ref[:, 1024:1280])
    k = _rope(_group_norm(a, gm, kg), cos, sin, up, HEAD_DIM // 2)
    v = _dot(h, w_ref[:, 1280:1536])
    kv_ref[:, 0:256] = k
    kv_ref[:, 256:512] = v
    kvb_ref[:, 0:256] = k.astype(BF16)
    kvb_ref[:, 256:512] = v.astype(BF16)
    kvt_ref[0:256, :] = k.T
    kvt_ref[256:512, :] = v.T


def _t_spec(feat, tn, nper):
    return pl.BlockSpec((None, feat, tn), lambda i: (i // nper, 0, i % nper))


def _moba_proj(x, g, w, qn, kn, pos, tn, qdtype):
    n, d = x.shape
    period = pos.shape[0]
    cos, sin, up = _rope_tables(pos, 256, _head64_segs(256))
    consts = np.zeros((8, 256), np.float32)
    consts[2] = up[0]
    consts = jnp.asarray(consts).at[0].set(jnp.tile(qn, 4)).at[1].set(jnp.tile(kn, 4))
    gm = jnp.asarray(_group_mean_matrix([(lo, lo + 64) for lo in range(0, 256, 64)], 256), BF16)
    nper = period // tn
    tab = pl.BlockSpec((tn, 256), lambda i: (i % nper, 0))
    return pl.pallas_call(
        _moba_proj_kernel,
        out_shape=(jax.ShapeDtypeStruct((n, 1024), qdtype),
                   jax.ShapeDtypeStruct((n, 512), F32),
                   jax.ShapeDtypeStruct((n, 512), BF16),
                   jax.ShapeDtypeStruct((n // period, 512, period), F32)),
        grid=(n // tn,),
        in_specs=[pl.BlockSpec((tn, d), lambda i: (i, 0)), _full((1, d)), _full(w.shape),
                  _full((256, 256)), _full((8, 256)), tab, tab],
        out_specs=(pl.BlockSpec((tn, 1024), lambda i: (i, 0)),
                   pl.BlockSpec((tn, 512), lambda i: (i, 0)),
                   pl.BlockSpec((tn, 512), lambda i: (i, 0)),
                   _t_spec(512, tn, nper)),
        compiler_params=_cparams("parallel"),
    )(x, g, w, gm, consts, jnp.asarray(cos), jnp.asarray(sin))


def _topk_mask(score, k, n_valid):
    rows, w = score.shape
    lane = lax.broadcasted_iota(jnp.int32, (rows, w), 1)
    rank = jnp.zeros((rows, w), jnp.int32)
    for j in range(n_valid):
        sj = score[:, j:j + 1]
        ahead = (sj > score) | ((sj == score) & (lane > j))
        rank = rank + ahead.astype(jnp.int32)
    return rank < k


def _topk_mask_t(score_t, k, n_valid):
    nc, n = score_t.shape
    cand = lax.broadcasted_iota(jnp.int32, (nc, n), 0)
    rank = jnp.zeros((nc, n), jnp.int32)
    for j in range(n_valid):
        sj = score_t[j:j + 1, :]
        ahead = (sj > score_t) | ((sj == score_t) & (cand > j))
        rank = rank + ahead.astype(jnp.int32)
    return rank < k


def _bias_rows(sel_t):
    nc, n = sel_t.shape
    bias_t = jnp.where(sel_t, 0.0, NEG)
    if nc < LANES:
        bias_t = jnp.concatenate([bias_t, jnp.zeros((LANES - nc, n), F32)], axis=0)
    return bias_t.T


def _rep(x, n):
    return x if n == 1 else jnp.concatenate([x] * n, axis=1)


def _flash_first(s, pv, m_ref, l_ref, acc_ref):
    m = jnp.max(s, axis=-1, keepdims=True)
    p = jnp.exp(s - m)
    m_ref[...] = jnp.broadcast_to(m, m_ref.shape)
    if l_ref is not None:
        l_ref[...] = jnp.broadcast_to(jnp.sum(p, axis=-1, keepdims=True), l_ref.shape)
    acc_ref[...] = pv(p.astype(BF16))


def _flash_masked(s, ok, pv, m_ref, l_ref, acc_ref):
    s = jnp.where(ok, s, NEG)
    m_old = m_ref[...]
    m_new = jnp.maximum(m_old, jnp.max(s, axis=-1, keepdims=True))
    p = jnp.where(ok, jnp.exp(s - _rep(m_new, s.shape[1] // LANES)), 0.0)
    alpha = jnp.exp(m_old - m_new)
    m_ref[...] = m_new
    if l_ref is not None:
        l_ref[...] = alpha * l_ref[...] + jnp.sum(p, axis=-1, keepdims=True)
    acc_ref[...] = _rep(alpha, acc_ref.shape[1] // LANES) * acc_ref[...] + pv(p.astype(BF16))


def _flash_next(s, pv, m_ref, l_ref, acc_ref):
    m_old = m_ref[...]
    m_new = jnp.maximum(m_old, jnp.max(s, axis=-1, keepdims=True))
    p = jnp.exp(s - _rep(m_new, s.shape[1] // LANES))
    alpha = jnp.exp(m_old - m_new)
    m_ref[...] = m_new
    if l_ref is not None:
        l_ref[...] = alpha * l_ref[...] + jnp.sum(p, axis=-1, keepdims=True)
    acc_ref[...] = _rep(alpha, acc_ref.shape[1] // LANES) * acc_ref[...] + pv(p.astype(BF16))


def _with_ones(v):
    return jnp.concatenate([v, jnp.ones(v.shape, BF16)], axis=1)


def _finish(acc_ref):
    acc = acc_ref[...]
    return acc[:, 0:LANES] / acc[:, LANES:2 * LANES]


def _slot_rows(q_ref, n_slots):
    tq = q_ref.shape[0]
    lane = lax.broadcasted_iota(jnp.int32, (tq, LANES), 1)
    ev, od = [], []
    for s in range(n_slots):
        qs = q_ref[:, s * LANES:(s + 1) * LANES]
        ev.append(jnp.where(lane < HEAD_DIM, qs, jnp.zeros_like(qs)))
        od.append(jnp.where(lane >= HEAD_DIM, qs, jnp.zeros_like(qs)))
    return jnp.concatenate(ev + od, axis=0)


def _slot_out(o, n_slots, tq):
    lane = lax.broadcasted_iota(jnp.int32, (tq, LANES), 1)
    outs = []
    for s in range(n_slots):
        e = o[s * tq:(s + 1) * tq, :]
        d = o[(n_slots + s) * tq:(n_slots + s + 1) * tq, :]
        outs.append(jnp.where(lane < HEAD_DIM, e, d))
    return jnp.concatenate(outs, axis=1)


def _moba_prompt_kernel(q_ref, k_ref, v_ref, bm_ref, o_ref, m_ref, acc_ref):
    i = pl.program_id(2)
    tq = q_ref.shape[0]
    nb = k_ref.shape[0] // MOBA_BLOCK
    rows = 8 * tq
    qa = _slot_rows(q_ref, 4)
    bhi, blo = _split_bf16(bm_ref[...])
    gate_t = _dot_nt(bhi, qa) + _dot_nt(blo, qa)
    blk = lax.broadcasted_iota(jnp.int32, (nb, rows), 0)
    gate_t = jnp.where(blk < i, gate_t, -jnp.inf)
    sel_t = _topk_mask_t(gate_t, MOBA_TOPK, nb) & (blk < i)
    qaug = jnp.concatenate([qa, _bias_rows(sel_t).astype(BF16)], axis=1)

    kd = k_ref[pl.ds(pl.multiple_of(i * MOBA_BLOCK, MOBA_BLOCK), MOBA_BLOCK), :]
    vd = v_ref[pl.ds(pl.multiple_of(i * MOBA_BLOCK, MOBA_BLOCK), MOBA_BLOCK), :]
    qi = lax.broadcasted_iota(jnp.int32, (rows, MOBA_BLOCK), 0) & (tq - 1)
    ki = lax.broadcasted_iota(jnp.int32, (rows, MOBA_BLOCK), 1)
    _flash_first(jnp.where(ki <= qi, _dot_nt(qa, kd), NEG), lambda p: _dot(p, _with_ones(vd)),
                 m_ref, None, acc_ref)

    lane = lax.broadcasted_iota(jnp.int32, (MOBA_BLOCK, LANES), 1)
    for j in range(nb - 1):
        @pl.when(j < i)
        def _(j=j):
            onehot = jnp.where(lane == j, 1.0, 0.0).astype(BF16)
            kj = jnp.concatenate([k_ref[j * MOBA_BLOCK:(j + 1) * MOBA_BLOCK, :], onehot], axis=1)
            vj = _with_ones(v_ref[j * MOBA_BLOCK:(j + 1) * MOBA_BLOCK, :])
            _flash_next(_dot_nt(qaug, kj), lambda p: _dot(p, vj), m_ref, None, acc_ref)

    o_ref[...] = _slot_out(_finish(acc_ref), 4, tq).astype(o_ref.dtype)


def _moba_prompt_attn(q, kvb, bm, batch, seq):
    tq = MOBA_BLOCK
    nq = seq // tq
    nb = seq // MOBA_BLOCK
    return pl.pallas_call(
        _moba_prompt_kernel,
        out_shape=jax.ShapeDtypeStruct((batch * seq, 1024), BF16),
        grid=(batch, 2, nq),
        in_specs=[pl.BlockSpec((tq, 512), lambda b, p, i: (b * nq + i, p)),
                  pl.BlockSpec((seq, 128), lambda b, p, i: (b, p)),
                  pl.BlockSpec((seq, 128), lambda b, p, i: (b, 2 + p)),
                  pl.BlockSpec((nb, 128), lambda b, p, i: (b, p))],
        out_specs=pl.BlockSpec((tq, 512), lambda b, p, i: (b * nq + i, p)),
        scratch_shapes=[pltpu.VMEM((8 * tq, 128), F32), pltpu.VMEM((8 * tq, 256), F32)],
        compiler_params=_cparams("parallel", "parallel", "arbitrary"),
    )(q, kvb, kvb, bm)


def _bmean_kernel(k_ref, o_ref):
    nb = o_ref.shape[0]
    k = k_ref[...].reshape(nb, MOBA_BLOCK, k_ref.shape[1])
    o_ref[...] = jnp.sum(k, axis=1) * (1.0 / MOBA_BLOCK)


def _moba_bmean(kv, batch, seq):
    nb = seq // MOBA_BLOCK
    return pl.pallas_call(
        _bmean_kernel,
        out_shape=jax.ShapeDtypeStruct((batch * nb, 256), F32),
        grid=(batch,),
        in_specs=[pl.BlockSpec((seq, 256), lambda b: (b, 0))],
        out_specs=pl.BlockSpec((nb, 256), lambda b: (b, 0)),
        compiler_params=_cparams("parallel"),
    )(kv)


def _moba_sample_kernel(pt_ref, q_ref, kvn_ref, *refs, bps, nb_past):
    pages = refs[:2 * bps]
    o_ref = refs[2 * bps]
    qa_ref, bm_ref, mst_ref, lst_ref, oacc_ref = refs[2 * bps + 1:]
    s = pl.program_id(1)
    rows = 128
    lane128 = lax.broadcasted_iota(jnp.int32, (8, LANES), 1)

    @pl.when(s == 0)
    def _():
        q = q_ref[...]
        z = jnp.zeros((8, LANES), F32)
        ev, od = [], []
        for sl in range(8):
            qs = q[:, sl * LANES:(sl + 1) * LANES]
            e = jnp.where(lane128 < HEAD_DIM, qs, 0.0)
            d = jnp.where(lane128 >= HEAD_DIM, qs, 0.0)
            if sl // 4 == 0:
                ev.append(jnp.concatenate([e, z], axis=1))
                od.append(jnp.concatenate([d, z], axis=1))
            else:
                ev.append(jnp.concatenate([z, e], axis=1))
                od.append(jnp.concatenate([z, d], axis=1))
        qa_ref[...] = jnp.concatenate(ev + od, axis=0).astype(BF16)
        bm_ref[...] = jnp.zeros_like(bm_ref)
        mst_ref[...] = jnp.zeros_like(mst_ref)
        lst_ref[...] = jnp.zeros_like(lst_ref)

    qa = qa_ref[...]
    col = lax.broadcasted_iota(jnp.int32, (rows, LANES), 1)
    for t in range(bps):
        jg = s * bps + t
        kt = jnp.concatenate([pages[2 * t][0:256, :], pages[2 * t + 1][0:256, :]], axis=1)
        vt = jnp.concatenate([pages[2 * t][256:512, :], pages[2 * t + 1][256:512, :]], axis=1)
        sc = _dot(qa, kt.astype(BF16))
        bm_ref[...] = jnp.where(col == jg, jnp.sum(sc, axis=-1, keepdims=True), bm_ref[...])
        m = jnp.max(sc, axis=-1, keepdims=True)
        p = jnp.exp(sc - m)
        oacc_ref[jg] = _dot_nt(p.astype(BF16), vt.astype(BF16))
        mst_ref[...] = jnp.where(col == jg, m, mst_ref[...])
        lst_ref[...] = jnp.where(col == jg, jnp.sum(p, axis=-1, keepdims=True), lst_ref[...])

    @pl.when(s == pl.num_programs(1) - 1)
    def _():
        kn = kvn_ref[0:256, :].astype(BF16)
        vn = kvn_ref[256:512, :].astype(BF16)
        sn = _dot(qa, kn)
        qi = lax.broadcasted_iota(jnp.int32, (rows, LANES), 0) & 7
        okn = col <= qi
        sn = jnp.where(okn, sn, NEG)
        mn = jnp.max(sn, axis=-1, keepdims=True)
        pn = jnp.where(okn, jnp.exp(sn - mn), 0.0)
        ln = jnp.sum(pn, axis=-1, keepdims=True)
        on = _dot_nt(pn.astype(BF16), vn)
        gate_t = bm_ref[...].T[0:nb_past, :]
        sel_t = jnp.where(_topk_mask_t(gate_t, MOBA_TOPK, nb_past), 1.0, 0.0)
        sel = jnp.concatenate([sel_t, jnp.zeros((LANES - nb_past, rows), F32)], axis=0).T > 0.5
        mst = mst_ref[...]
        mstar = jnp.maximum(jnp.max(jnp.where(sel, mst, NEG), axis=-1, keepdims=True), mn)
        w = jnp.where(sel, jnp.exp(mst - mstar), 0.0)
        wn = jnp.exp(mn - mstar)
        den = jnp.sum(w * lst_ref[...], axis=-1, keepdims=True) + wn * ln
        acc = wn * on
        for j in range(nb_past):
            acc = acc + w[:, j:j + 1] * oacc_ref[j]
        o = acc / den
        outs = []
        for sl in range(8):
            c = (sl // 4) * LANES
            e = o[sl * 8:(sl + 1) * 8, c:c + LANES]
            d = o[(8 + sl) * 8:(9 + sl) * 8, c:c + LANES]
            outs.append(jnp.where(lane128 < HEAD_DIM, e, d))
        o_ref[...] = jnp.concatenate(outs, axis=1)


def _pages_t(cache):
    return jnp.swapaxes(cache.reshape(cache.shape[0], PAGE, -1), 1, 2)


def _rows_t(x, db):
    xt = jnp.swapaxes(x.reshape(db, x.shape[0] // db, x.shape[1]), 1, 2)
    return jnp.pad(xt, ((0, 0), (0, 0), (0, LANES - xt.shape[2])))


def _moba_sample_attn(q, kvn_t, cache_t, page_table, bps):
    db, n_pages = page_table.shape
    nb_past = n_pages * PAGE // MOBA_BLOCK
    assert n_pages % (2 * bps) == 0 and nb_past < LANES
    nsteps = n_pages // (2 * bps)

    def page_spec(t):
        return pl.BlockSpec((None, 512, PAGE), lambda b, s, pt: (pt[b, s * 2 * bps + t], 0, 0))

    grid_spec = pltpu.PrefetchScalarGridSpec(
        num_scalar_prefetch=1,
        grid=(db, nsteps),
        in_specs=[pl.BlockSpec((8, 1024), lambda b, s, pt: (b, 0)),
                  pl.BlockSpec((None, 512, LANES), lambda b, s, pt: (b, 0, 0))]
                 + [page_spec(t) for t in range(2 * bps)],
        out_specs=pl.BlockSpec((8, 1024), lambda b, s, pt: (b, 0)),
        scratch_shapes=[pltpu.VMEM((128, 256), BF16), pltpu.VMEM((128, LANES), F32),
                        pltpu.VMEM((128, LANES), F32), pltpu.VMEM((128, LANES), F32),
                        pltpu.VMEM((nb_past, 128, 256), F32)],
    )
    return pl.pallas_call(
        functools.partial(_moba_sample_kernel, bps=bps, nb_past=nb_past),
        out_shape=jax.ShapeDtypeStruct((db * 8, 1024), F32),
        grid_spec=grid_spec,
        compiler_params=_cparams("parallel", "arbitrary"),
    )(page_table, q, kvn_t, *([cache_t] * (2 * bps)))


def _moba_layer(xp, xs, cache, page_table, ln_g, w_in, qn, kn, w_o, batch, seq, past_len):
    perm = _slot_perm(MOBA_KV_HEADS)
    w_in_p = jnp.concatenate([w_in[:, :1024][:, perm], w_in[:, 1024:]], axis=1).astype(BF16)
    w_o_p = w_o[perm, :].astype(BF16)
    g = ln_g.reshape(1, -1)
    dec = 8
    q, kv, kvb, kvt = _moba_proj(xp, g, w_in_p, qn, kn, np.arange(seq), 512, BF16)
    bm = _moba_bmean(kv, batch, seq)
    o = _moba_prompt_attn(q, kvb, bm, batch, seq)
    yp = _outproj(xp, o, w_o_p, 512)
    ns = xs.shape[0]
    pos_s = np.tile(past_len + np.arange(dec), ns // dec)
    qs, kvs, _, _ = _moba_proj(xs, g, w_in_p, qn, kn, pos_s, ns, F32)
    db, n_pages = page_table.shape
    bps = max(b for b in (1, 2, 4, 8) if n_pages % (2 * b) == 0)
    os_ = _moba_sample_attn(qs, _rows_t(kvs, db), _pages_t(cache), page_table, bps)
    ys = _outproj(xs, os_, w_o_p, ns)
    return yp, ys, kvt, kvs


def _nsa_proj_kernel(x_ref, g_ref, w_ref, gm_ref, c_ref, cos_ref, sin_ref,
                     q_ref, kv_ref, kvb_ref, gate_ref, kvt_ref):
    h = _rms_rows(x_ref[...], g_ref[...]).astype(BF16)
    gm = gm_ref[...]
    cos, sin = cos_ref[...], sin_ref[...]
    qg, up = c_ref[0:1, :], c_ref[4:5, :]
    for c in range(4):
        a = _dot(h, w_ref[:, c * 256:(c + 1) * 256])
        y = _rope(_group_norm(a, gm, qg), cos, sin, up, HEAD_DIM // 2)
        q_ref[:, c * 256:(c + 1) * 256] = (y * HEAD_DIM ** -0.5).astype(q_ref.dtype)
    gm1 = gm_ref[0:128, 0:128]
    for br in range(3):
        a = _dot(h, w_ref[:, 1024 + br * 256:1024 + (br + 1) * 256])
        k = _rope(_group_norm(a[:, 0:128], gm1, c_ref[1 + br:2 + br, 0:128]),
                  cos[:, 0:128], sin[:, 0:128], up[:, 0:128], HEAD_DIM // 2)
        v = a[:, 128:256]
        kv_ref[:, br * 256:br * 256 + 128] = k
        kv_ref[:, br * 256 + 128:(br + 1) * 256] = v
        kvb_ref[:, br * 256:br * 256 + 128] = k.astype(BF16)
        kvb_ref[:, br * 256 + 128:(br + 1) * 256] = v.astype(BF16)
        kvt_ref[br * 256:br * 256 + 128, :] = k.T
        kvt_ref[br * 256 + 128:(br + 1) * 256, :] = v.T
    a = _dot(h, w_ref[:, 1792:1920])
    gate_ref[...] = 1.0 / (1.0 + jnp.exp(-a))


def _nsa_proj(x, g, w, qn, kn, pos, tn, qdtype):
    n, d = x.shape
    period = pos.shape[0]
    cos, sin, up = _rope_tables(pos, 256, _head64_segs(256))
    consts = np.zeros((8, 256), np.float32)
    consts[4] = up[0]
    consts = jnp.asarray(consts).at[0].set(jnp.tile(qn, 4))
    for br in range(3):
        consts = consts.at[1 + br].set(jnp.tile(kn[br], 4))
    gm = jnp.asarray(_group_mean_matrix([(lo, lo + 64) for lo in range(0, 256, 64)], 256), BF16)
    nper = period // tn
    tab = pl.BlockSpec((tn, 256), lambda i: (i % nper, 0))
    row = lambda wd: pl.BlockSpec((tn, wd), lambda i: (i, 0))
    return pl.pallas_call(
        _nsa_proj_kernel,
        out_shape=(jax.ShapeDtypeStruct((n, 1024), qdtype),
                   jax.ShapeDtypeStruct((n, 768), F32),
                   jax.ShapeDtypeStruct((n, 768), BF16),
                   jax.ShapeDtypeStruct((n, 128), F32),
                   jax.ShapeDtypeStruct((n // period, 768, period), F32)),
        grid=(n // tn,),
        in_specs=[row(d), _full((1, d)), _full(w.shape), _full((256, 256)), _full((8, 256)), tab, tab],
        out_specs=(row(1024), row(768), row(768), row(128), _t_spec(768, tn, nper)),
        compiler_params=_cparams("parallel"),
    )(x, g, w, gm, consts, jnp.asarray(cos), jnp.asarray(sin))


def _nsa_cmp_weights(cmp_pe, cmp_w1, cmp_w2):
    w1h, pe_rows, w2blk = [], [], []
    for i in range(2):
        w1h.append(jnp.concatenate([cmp_w1[i][:1024], cmp_w1[i][1024:]], axis=1).astype(BF16))
        pe_rows.append(jnp.concatenate([cmp_pe[i].reshape(2, 1024), jnp.zeros((6, 1024), F32)], axis=0))
        z2 = jnp.zeros_like(cmp_w2[i])
        w2blk.append(jnp.concatenate([jnp.concatenate([cmp_w2[i], z2], axis=1),
                                      jnp.concatenate([z2, cmp_w2[i]], axis=1)], axis=0).astype(BF16))
    return w1h, pe_rows, w2blk


def _cmp_chunks(row_ref, m):
    lane = lax.broadcasted_iota(jnp.int32, (m, LANES), 1)
    g0, g1 = [], []
    for a in range(NSA_CMP_STRIDE // 2):
        xe = row_ref[pl.ds(2 * a, m, stride=NSA_CMP_STRIDE), :]
        xo = row_ref[pl.ds(2 * a + 1, m, stride=NSA_CMP_STRIDE), :]
        g0.append(jnp.where(lane < HEAD_DIM, xe, pltpu.roll(xo, HEAD_DIM, 1)))
        g1.append(jnp.where(lane < HEAD_DIM, pltpu.roll(xe, HEAD_DIM, 1), xo))
    return jnp.concatenate([jnp.concatenate(g0, axis=1), jnp.concatenate(g1, axis=1)], axis=0).astype(BF16)


def _cmp_tokens(pre_a, pre_b, pe_ref, w_ref, b1, w2_ref, m):
    pc = _dot(pe_ref[...].astype(BF16), w_ref[...])
    c = pc[0:1, 0:256] + pc[1:2, 256:512] + b1
    hid = _gelu_tanh(pre_a[:, 0:256] + pre_b[:, 256:512] + c)
    hid = jnp.concatenate([hid[0:m], hid[m:2 * m]], axis=1).astype(BF16)
    return _dot(hid, w2_ref[...])


def _nsa_cmp_prompt_kernel(k_ref, v_ref, wk_ref, wv_ref, pek_ref, pev_ref, b1_ref, w2k_ref, w2v_ref, o_ref):
    nch = o_ref.shape[0]
    for i, (r_ref, w_ref, pe_ref, w2_ref) in enumerate(((k_ref, wk_ref, pek_ref, w2k_ref),
                                                         (v_ref, wv_ref, pev_ref, w2v_ref))):
        pre = _dot(_cmp_chunks(r_ref, nch), w_ref[...])
        nxt = pltpu.roll(pre, 2 * nch - 1, 0)
        o_ref[:, i * 128:(i + 1) * 128] = _cmp_tokens(pre, nxt, pe_ref, w_ref, b1_ref[i:i + 1, :], w2_ref, nch)


def _nsa_cmp_prompt(kv, wbig, pe_rows, b1, w2blk, batch, seq):
    nch = seq // NSA_CMP_STRIDE
    return pl.pallas_call(
        _nsa_cmp_prompt_kernel,
        out_shape=jax.ShapeDtypeStruct((batch * nch, 256), F32),
        grid=(batch,),
        in_specs=[pl.BlockSpec((seq, 128), lambda b: (b, 0)), pl.BlockSpec((seq, 128), lambda b: (b, 1)),
                  _full((1024, 512)), _full((1024, 512)), _full((8, 1024)), _full((8, 1024)),
                  _full((8, 256)), _full((512, 128)), _full((512, 128))],
        out_specs=pl.BlockSpec((nch, 256), lambda b: (b, 0)),
        compiler_params=_cparams("parallel"),
    )(kv, kv, wbig[0], wbig[1], pe_rows[0], pe_rows[1], b1, w2blk[0], w2blk[1])


def _flash_step(s, ok, v, m_ref, l_ref, acc_ref):
    s = jnp.where(ok, s, NEG)
    m_old = m_ref[...]
    m_new = jnp.maximum(m_old, jnp.max(s, axis=-1, keepdims=True))
    p = jnp.where(ok, jnp.exp(s - m_new), 0.0)
    alpha = jnp.exp(m_old - m_new)
    m_ref[...] = m_new
    l_ref[...] = alpha * l_ref[...] + jnp.sum(p, axis=-1, keepdims=True)
    acc_ref[...] = alpha * acc_ref[...] + _dot(p.astype(BF16), v)


def _flash_init(m_ref, l_ref, acc_ref):
    m_ref[...] = jnp.full(m_ref.shape, NEG, F32)
    l_ref[...] = jnp.zeros(l_ref.shape, F32)
    acc_ref[...] = jnp.zeros(acc_ref.shape, F32)


def _nsa_prompt_kernel(q_ref, ks_ref, vs_ref, kw_ref, vw_ref, kvc_ref, gate_ref, ovl_ref, e_ref,
                       o_ref, m_ref, acc_ref, os_ref):
    i = pl.program_id(1)
    tq = q_ref.shape[0]
    rows = 16 * tq
    tk = 256
    q0 = i * tq
    qa = _slot_rows(q_ref, 8)
    ntok = kvc_ref.shape[0]

    kc = kvc_ref[:, 0:128].astype(BF16)
    vc = kvc_ref[:, 128:256].astype(BF16)
    qpos_c = q0 + (lax.broadcasted_iota(jnp.int32, (rows, ntok), 0) & (tq - 1))
    tok = lax.broadcasted_iota(jnp.int32, (rows, ntok), 1)
    ok = tok * NSA_CMP_STRIDE + (NSA_CMP_LEN - 1) <= qpos_c
    sc = jnp.where(ok, _dot_nt(qa, kc), NEG)
    mc = jnp.max(sc, axis=-1, keepdims=True)
    ec = jnp.where(ok, jnp.exp(sc - mc), 0.0)
    pc = ec / jnp.maximum(jnp.sum(ec, axis=-1, keepdims=True), TINY)
    o_c = _dot(pc.astype(BF16), vc)

    pg = jnp.sum(pc.reshape(2, 8, tq, ntok), axis=1).reshape(2 * tq, ntok)
    phi, plo = _split_bf16(pg)
    nsb = ks_ref.shape[0] // NSA_SLC_BLOCK
    imp_t = (_dot_nt(ovl_ref[...], phi) + _dot_nt(ovl_ref[...], plo))[0:nsb, :]
    qp2 = q0 + (lax.broadcasted_iota(jnp.int32, (nsb, 2 * tq), 1) & (tq - 1))
    own = qp2 >> 6
    jb = lax.broadcasted_iota(jnp.int32, (nsb, 2 * tq), 0)
    allowed = jb <= own
    forced = (jb == 0) | (jb == own) | (jb == own - 1)
    imp_t = jnp.where(forced, NSA_FORCE_SCORE, imp_t)
    imp_t = jnp.where(allowed, imp_t, -jnp.inf)
    sel_t = _topk_mask_t(imp_t, NSA_SLC_TOPN, nsb) & allowed
    bias = _bias_rows(sel_t).astype(BF16)
    qaug = jnp.concatenate([qa, jnp.concatenate([bias[0:tq]] * 8 + [bias[tq:2 * tq]] * 8, axis=0)], axis=1)

    qp_r = q0 + (lax.broadcasted_iota(jnp.int32, (rows, tk), 0) & (tq - 1))
    klr = lax.broadcasted_iota(jnp.int32, (rows, tk), 1)
    jd = (q0 + tq - 1) // tk
    offd = pl.multiple_of(jd * tk, tk)

    def slc_scores(off):
        kj = jnp.concatenate([ks_ref[pl.ds(off, tk), :], e_ref[pl.ds(off, tk), :]], axis=1)
        return _dot_nt(qaug, kj)

    _flash_first(jnp.where(offd + klr <= qp_r, slc_scores(offd), NEG),
                 lambda p: _dot(p, _with_ones(vs_ref[pl.ds(offd, tk), :])), m_ref, None, acc_ref)

    def slc_body(j, carry):
        off = pl.multiple_of(j * tk, tk)
        _flash_next(slc_scores(off), lambda p: _dot(p, _with_ones(vs_ref[pl.ds(off, tk), :])),
                    m_ref, None, acc_ref)
        return carry

    lax.fori_loop(0, jd, slc_body, 0)
    os_ref[...] = _finish(acc_ref)

    def win_scores(off):
        dist = qp_r - (off + klr)
        return jnp.where((dist >= 0) & (dist < NSA_WINDOW), _dot_nt(qa, kw_ref[pl.ds(off, tk), :]), NEG)

    _flash_first(win_scores(offd), lambda p: _dot(p, _with_ones(vw_ref[pl.ds(offd, tk), :])),
                 m_ref, None, acc_ref)

    def win_body(j, carry):
        off = pl.multiple_of(j * tk, tk)
        _flash_next(win_scores(off), lambda p: _dot(p, _with_ones(vw_ref[pl.ds(off, tk), :])),
                    m_ref, None, acc_ref)
        return carry

    lax.fori_loop(jnp.maximum(q0 - (NSA_WINDOW - 1), 0) // tk, jd, win_body, 0)
    o_w = _finish(acc_ref)
    o_s = os_ref[...]

    gate = gate_ref[...]
    outs = []
    for h in range(16):
        r0 = h * tq
        outs.append(gate[:, 3 * h:3 * h + 1] * o_c[r0:r0 + tq, :]
                    + gate[:, 3 * h + 1:3 * h + 2] * o_s[r0:r0 + tq, :]
                    + gate[:, 3 * h + 2:3 * h + 3] * o_w[r0:r0 + tq, :])
    o_ref[...] = _slot_out(jnp.concatenate(outs, axis=0), 8, tq).astype(o_ref.dtype)


def _nsa_overlap(n_tok, n_blk, tok_shift, rows, cols):
    ovl = np.zeros((rows, cols), np.float32)
    t = np.arange(n_tok)[:, None] * NSA_CMP_STRIDE
    b = np.arange(n_blk)[None, :] * NSA_SLC_BLOCK
    ovl[tok_shift:tok_shift + n_tok, :n_blk] = ((t < b + NSA_SLC_BLOCK) & (t + NSA_CMP_LEN > b))
    return ovl


def _nsa_prompt_attn(q, kvb, kvc, gate, batch, seq):
    tq, tk = 128, 256
    nq = seq // tq
    nch = seq // NSA_CMP_STRIDE
    n_tok = (seq - NSA_CMP_LEN) // NSA_CMP_STRIDE + 1
    nsb = seq // NSA_SLC_BLOCK
    assert nch == LANES and nsb <= LANES
    ovl = jnp.asarray(_nsa_overlap(n_tok, nsb, 0, nch, LANES).T, BF16)
    e = np.zeros((seq, LANES), np.float32)
    e[np.arange(seq), np.arange(seq) // NSA_SLC_BLOCK] = 1.0
    e = jnp.asarray(e, BF16)
    seqcol = lambda c: pl.BlockSpec((seq, 128), lambda b, i: (b, c))
    return pl.pallas_call(
        _nsa_prompt_kernel,
        out_shape=jax.ShapeDtypeStruct((batch * seq, 1024), BF16),
        grid=(batch, nq),
        in_specs=[pl.BlockSpec((tq, 1024), lambda b, i: (b * nq + i, 0)),
                  seqcol(2), seqcol(3), seqcol(4), seqcol(5),
                  pl.BlockSpec((nch, 256), lambda b, i: (b, 0)),
                  pl.BlockSpec((tq, 128), lambda b, i: (b * nq + i, 0)),
                  _full((nch, LANES)), _full(e.shape)],
        out_specs=pl.BlockSpec((tq, 1024), lambda b, i: (b * nq + i, 0)),
        scratch_shapes=[pltpu.VMEM((16 * tq, 128), F32), pltpu.VMEM((16 * tq, 256), F32),
                        pltpu.VMEM((16 * tq, 128), F32)],
        compiler_params=_cparams("parallel", "arbitrary"),
    )(q, kvb, kvb, kvb, kvb, kvc, gate, ovl, e)


def _topk_mask_iter(score, k):
    rows, w = score.shape
    lane = lax.broadcasted_iota(jnp.int32, (rows, w), 1)
    taken = jnp.zeros((rows, w), jnp.int32)
    for _ in range(k):
        free = taken == 0
        cur = jnp.where(free, score, -jnp.inf)
        m = jnp.max(cur, axis=-1, keepdims=True)
        idx = jnp.min(jnp.where(free & (cur == m), lane, w), axis=-1, keepdims=True)
        taken = jnp.where(lane == idx, 1, taken)
    return taken > 0


def _sample_q_rows(q):
    lane = lax.broadcasted_iota(jnp.int32, (8, LANES), 1)
    ev = [jnp.where(lane < HEAD_DIM, q[:, s * LANES:(s + 1) * LANES], 0.0) for s in range(8)]
    od = [jnp.where(lane >= HEAD_DIM, q[:, s * LANES:(s + 1) * LANES], 0.0) for s in range(8)]
    return jnp.concatenate(ev + od, axis=0)


def _nsa_sample_cmp_kernel(pt_ref, q_ref, wk_ref, wv_ref, pek_ref, pev_ref, b1_ref, w2k_ref, w2v_ref,
                           ovl_ref, *refs, pps, past, blk_per_step):
    pages = refs[:pps]
    oc_ref, sel_ref = refs[pps], refs[pps + 1]
    kvc_ref, carry_ref, stage_ref = refs[pps + 2:]
    s = pl.program_id(1)
    m = pps * (PAGE // NSA_CMP_STRIDE)

    @pl.when(s == 0)
    def _():
        carry_ref[...] = jnp.zeros_like(carry_ref)

    row = lax.broadcasted_iota(jnp.int32, (2 * m, 512), 0)
    for i, (w_ref, pe_ref, w2_ref) in enumerate(((wk_ref, pek_ref, w2k_ref), (wv_ref, pev_ref, w2v_ref))):
        for t, pg in enumerate(pages):
            stage_ref[t * PAGE:(t + 1) * PAGE, :] = pg[i * 128:(i + 1) * 128, :].T
        pre = _dot(_cmp_chunks(stage_ref, m), w_ref[...])
        prev = pltpu.roll(pre, 1, 0)
        prev = jnp.where(row == 0, carry_ref[2 * i:2 * i + 1, :], prev)
        prev = jnp.where(row == m, carry_ref[2 * i + 1:2 * i + 2, :], prev)
        carry_ref[2 * i:2 * i + 1, :] = pre[m - 1:m, :]
        carry_ref[2 * i + 1:2 * i + 2, :] = pre[2 * m - 1:2 * m, :]
        kvc_ref[pl.ds(pl.multiple_of(s * m, m), m), i * 128:(i + 1) * 128] = _cmp_tokens(
            prev, pre, pe_ref, w_ref, b1_ref[i:i + 1, :], w2_ref, m)

    @pl.when(s == pl.num_programs(1) - 1)
    def _():
        nt = kvc_ref.shape[0]
        qa = _sample_q_rows(q_ref[...]).astype(BF16)
        kc = kvc_ref[:, 0:128].astype(BF16)
        vc = kvc_ref[:, 128:256].astype(BF16)
        r = lax.broadcasted_iota(jnp.int32, (128, nt), 1)
        qpos = past + (lax.broadcasted_iota(jnp.int32, (128, nt), 0) & 7)
        ok = (r >= 1) & ((r - 1) * NSA_CMP_STRIDE + (NSA_CMP_LEN - 1) <= qpos)
        sc = jnp.where(ok, _dot_nt(qa, kc), NEG)
        mc = jnp.max(sc, axis=-1, keepdims=True)
        ec = jnp.where(ok, jnp.exp(sc - mc), 0.0)
        pc = ec / jnp.maximum(jnp.sum(ec, axis=-1, keepdims=True), TINY)
        oc_ref[...] = _dot(pc.astype(BF16), vc)
        pg = jnp.sum(pc.reshape(2, 8, 8, nt), axis=1).reshape(16, nt)
        phi, plo = _split_bf16(pg)
        imp = _dot(phi, ovl_ref[...]) + _dot(plo, ovl_ref[...])
        nl = imp.shape[1]
        own = (past + (lax.broadcasted_iota(jnp.int32, (16, nl), 0) & 7)) >> 6
        jb = lax.broadcasted_iota(jnp.int32, (16, nl), 1)
        allowed = jb <= own
        forced = (jb == 0) | (jb == own) | (jb == own - 1)
        imp = jnp.where(forced, NSA_FORCE_SCORE, imp)
        imp = jnp.where(allowed, imp, -jnp.inf)
        sel = jnp.where(_topk_mask_iter(imp, NSA_SLC_TOPN) & allowed, 1.0, 0.0)
        lane = lax.broadcasted_iota(jnp.int32, (16, LANES), 1)
        for st in range(sel_ref.shape[0]):
            piece = sel if st == 0 else pltpu.roll(sel, nl - st * blk_per_step, 1)
            sel_ref[st] = jnp.where(lane < blk_per_step, piece[:, 0:LANES], 0.0)


def _nsa_sample_cmp(q, cache, page_table, w1h, pe_rows, b1, w2blk, past, pps, sel_pps):
    db, n_pages = page_table.shape
    nsteps = n_pages // pps
    nsel = n_pages // sel_pps
    nt = n_pages * (PAGE // NSA_CMP_STRIDE)
    n_tok = (past + 8 - NSA_CMP_LEN) // NSA_CMP_STRIDE + 1
    nsb = -(-(past + 8) // NSA_SLC_BLOCK)
    nl = -(-nsb // LANES) * LANES
    blk_per_step = sel_pps * PAGE // NSA_SLC_BLOCK
    assert n_tok == nt - 1 and blk_per_step <= LANES
    ovl = jnp.asarray(_nsa_overlap(n_tok, nsb, 1, nt, nl), BF16)

    def page_spec(t):
        return pl.BlockSpec((None, 256, PAGE), lambda b, s, pt: (pt[b, s * pps + t], 0, 0))

    grid_spec = pltpu.PrefetchScalarGridSpec(
        num_scalar_prefetch=1,
        grid=(db, nsteps),
        in_specs=[pl.BlockSpec((8, 1024), lambda b, s, pt: (b, 0)),
                  _full((1024, 512)), _full((1024, 512)), _full((8, 1024)), _full((8, 1024)),
                  _full((8, 256)), _full((512, 128)), _full((512, 128)), _full((nt, nl))]
                 + [page_spec(t) for t in range(pps)],
        out_specs=(pl.BlockSpec((128, 128), lambda b, s, pt: (b, 0)),
                   pl.BlockSpec((None, nsel, 16, LANES), lambda b, s, pt: (b, 0, 0, 0))),
        scratch_shapes=[pltpu.VMEM((nt, 256), F32), pltpu.VMEM((8, 512), F32),
                        pltpu.VMEM((pps * PAGE, 128), F32)],
    )
    return pl.pallas_call(
        functools.partial(_nsa_sample_cmp_kernel, pps=pps, past=past, blk_per_step=blk_per_step),
        out_shape=(jax.ShapeDtypeStruct((db * 128, 128), F32),
                   jax.ShapeDtypeStruct((db, nsel, 16, LANES), F32)),
        grid_spec=grid_spec,
        compiler_params=_cparams("parallel", "arbitrary"),
    )(page_table, q, w1h[0], w1h[1], pe_rows[0], pe_rows[1], b1, w2blk[0], w2blk[1], ovl,
      *([cache] * pps))


def _nsa_sample_attn_kernel(pt_ref, q_ref, sel_ref, oc_ref, kvn_ref, win_ref, gate_ref, e_ref, *refs,
                            pps, past):
    pages = refs[:pps]
    o_ref = refs[pps]
    qa_ref, m_ref, acc_ref = refs[pps + 1:]
    s = pl.program_id(1)
    rows = 128

    @pl.when(s == 0)
    def _():
        qa_ref[...] = _sample_q_rows(q_ref[...]).astype(BF16)
        m_ref[...] = jnp.full(m_ref.shape, NEG, F32)
        acc_ref[...] = jnp.zeros(acc_ref.shape, F32)

    qa = qa_ref[...]
    kt = jnp.concatenate([pg[0:128, :] for pg in pages], axis=1).astype(BF16)
    vt = jnp.concatenate([pg[128:256, :] for pg in pages], axis=1).astype(BF16)
    ones_t = jnp.ones(vt.shape, BF16)
    vt1 = jnp.concatenate([vt, ones_t], axis=0)
    sel = sel_ref[...]
    selrows = jnp.concatenate([sel[0:8, :]] * 8 + [sel[8:16, :]] * 8, axis=0)
    qaug = jnp.concatenate([qa, jnp.where(selrows > 0.5, 0.0, NEG).astype(BF16)], axis=1)
    _flash_next(_dot(qaug, jnp.concatenate([kt, e_ref[...]], axis=0)), lambda p: _dot_nt(p, vt1),
                m_ref, None, acc_ref)

    @pl.when(s == pl.num_programs(1) - 1)
    def _():
        col = lax.broadcasted_iota(jnp.int32, (rows, LANES), 1)
        qi = lax.broadcasted_iota(jnp.int32, (rows, LANES), 0) & 7
        kn = kvn_ref[256:384, :].astype(BF16)
        vn = jnp.concatenate([kvn_ref[384:512, :].astype(BF16), ones_t[:, 0:LANES]], axis=0)
        _flash_masked(_dot(qa, kn), col <= qi, lambda p: _dot_nt(p, vn), m_ref, None, acc_ref)
        o_s = _finish(acc_ref)
        wb = win_ref.shape[1]
        kw = jnp.concatenate([win_ref[0:128, :], kvn_ref[512:640, :]], axis=1).astype(BF16)
        vw = jnp.concatenate([win_ref[128:256, :], kvn_ref[640:768, :]], axis=1).astype(BF16)
        nw = wb + LANES
        c = lax.broadcasted_iota(jnp.int32, (rows, nw), 1)
        qpos = past + (lax.broadcasted_iota(jnp.int32, (rows, nw), 0) & 7)
        wpos = past - wb + c
        dist = qpos - wpos
        okw = (dist >= 0) & (dist < NSA_WINDOW) & (wpos >= 0) & (c < wb + 8)
        sw = jnp.where(okw, _dot(qa, kw), NEG)
        mw = jnp.max(sw, axis=-1, keepdims=True)
        pw = jnp.where(okw, jnp.exp(sw - mw), 0.0)
        o_w = _dot_nt(pw.astype(BF16), vw) / jnp.maximum(jnp.sum(pw, axis=-1, keepdims=True), TINY)
        o_c = oc_ref[...]
        gate = gate_ref[...]
        lane8 = lax.broadcasted_iota(jnp.int32, (8, LANES), 1)
        hs = []
        for h in range(16):
            r0 = h * 8
            hs.append(gate[:, 3 * h:3 * h + 1] * o_c[r0:r0 + 8, :]
                      + gate[:, 3 * h + 1:3 * h + 2] * o_s[r0:r0 + 8, :]
                      + gate[:, 3 * h + 2:3 * h + 3] * o_w[r0:r0 + 8, :])
        o_ref[...] = jnp.concatenate([jnp.where(lane8 < HEAD_DIM, hs[sl], hs[8 + sl]) for sl in range(8)],
                                     axis=1)


def _nsa_sample_attn(q, sel, o_c, kvn, state_win, gate, cache, page_table, past, pps):
    db, n_pages = page_table.shape
    nsteps = n_pages // pps
    wb = state_win.shape[2]
    nk = pps * PAGE
    e = np.zeros((LANES, nk), np.float32)
    e[np.arange(nk) // NSA_SLC_BLOCK, np.arange(nk)] = 1.0

    def page_spec(t):
        return pl.BlockSpec((None, 256, PAGE), lambda b, s, pt: (pt[b, s * pps + t], 1, 0))

    grid_spec = pltpu.PrefetchScalarGridSpec(
        num_scalar_prefetch=1,
        grid=(db, nsteps),
        in_specs=[pl.BlockSpec((8, 1024), lambda b, s, pt: (b, 0)),
                  pl.BlockSpec((None, None, 16, LANES), lambda b, s, pt: (b, s, 0, 0)),
                  pl.BlockSpec((128, 128), lambda b, s, pt: (b, 0)),
                  pl.BlockSpec((None, 768, LANES), lambda b, s, pt: (b, 0, 0)),
                  pl.BlockSpec((None, 256, wb), lambda b, s, pt: (b, 0, 0)),
                  pl.BlockSpec((8, 128), lambda b, s, pt: (b, 0)),
                  _full((LANES, nk))]
                 + [page_spec(t) for t in range(pps)],
        out_specs=pl.BlockSpec((8, 1024), lambda b, s, pt: (b, 0)),
        scratch_shapes=[pltpu.VMEM((128, 128), BF16), pltpu.VMEM((128, 128), F32), pltpu.VMEM((128, 256), F32)],
    )
    return pl.pallas_call(
        functools.partial(_nsa_sample_attn_kernel, pps=pps, past=past),
        out_shape=jax.ShapeDtypeStruct((db * 8, 1024), F32),
        grid_spec=grid_spec,
        compiler_params=_cparams("parallel", "arbitrary"),
    )(page_table, q, sel, o_c, kvn, state_win, gate, jnp.asarray(e, BF16), *([cache] * pps))


def _nsa_layer(xp, xs, cache, state_win, page_table, ln_g, w_in, qn, kn, cmp_pe, cmp_w1, cmp_b1, cmp_w2,
               w_o, batch, seq, past_len):
    perm = _slot_perm(NSA_KV_HEADS)
    w_in_p = jnp.concatenate([w_in[:, :1024][:, perm], w_in[:, 1024:],
                              jnp.zeros((w_in.shape[0], 1920 - w_in.shape[1]), F32)], axis=1).astype(BF16)
    w_o_p = w_o[perm, :].astype(BF16)
    g = ln_g.reshape(1, -1)
    wbig, pe_rows, w2blk = _nsa_cmp_weights(cmp_pe, cmp_w1, cmp_w2)
    b1 = jnp.concatenate([cmp_b1, jnp.zeros((6, NSA_CMP_HIDDEN), F32)], axis=0)
    dec = 8
    q, kv, kvb, gate, kvt = _nsa_proj(xp, g, w_in_p, qn, kn, np.arange(seq), 512, BF16)
    kvc = _nsa_cmp_prompt(kv, wbig, pe_rows, b1, w2blk, batch, seq)
    o = _nsa_prompt_attn(q, kvb, kvc, gate, batch, seq)
    yp = _outproj(xp, o, w_o_p, 512)
    ns = xs.shape[0]
    db, n_pages = page_table.shape
    pps = min(16, n_pages)
    pos_s = np.tile(past_len + np.arange(dec), ns // dec)
    qs, kvs, _, gs, _ = _nsa_proj(xs, g, w_in_p, qn, kn, pos_s, ns, F32)
    cache_t = _pages_t(cache)
    win_t = jnp.swapaxes(state_win.reshape(db, state_win.shape[1], 256), 1, 2)
    o_c, sel = _nsa_sample_cmp(qs, cache_t, page_table, wbig, pe_rows, b1, w2blk, past_len,
                               min(32, n_pages), pps)
    os_ = _nsa_sample_attn(qs, sel, o_c, _rows_t(kvs, db), win_t, gs, cache_t, page_table, past_len, pps)
    ys = _outproj(xs, os_, w_o_p, ns)
    return yp, ys, kvt, kvs


MLA_QK = MLA_NOPE + MLA_ROPE


def _mla_proj_kernel(x_ref, g_ref, w_ref, wuq_ref, gm_ref, c_ref, cosq_ref, sinq_ref, cosk_ref, sink_ref,
                     q_ref, lat_ref, latt_ref):
    h = _rms_rows(x_ref[...], g_ref[...]).astype(BF16)
    cq = _rms_rows(_dot(h, w_ref[:, 0:MLA_Q_LORA]), c_ref[0:1, 0:MLA_Q_LORA]).astype(BF16)
    ckv = _rms_rows(_dot(h, w_ref[:, MLA_Q_LORA:MLA_Q_LORA + MLA_KV_LORA]), c_ref[1:2, 0:MLA_KV_LORA])
    a = _dot(h, w_ref[:, 640:768])
    ms = jnp.sum(a * a, axis=-1, keepdims=True) * (1.0 / MLA_ROPE)
    kpe = a * lax.rsqrt(ms + NORM_EPS) * c_ref[2:3, 0:128]
    kpe = _rope(kpe, cosk_ref[...], sink_ref[...], c_ref[3:4, 0:128], MLA_ROPE // 2)
    lat_ref[:, 0:MLA_KV_LORA] = ckv
    lat_ref[:, MLA_KV_LORA:MLA_KV_LORA + MLA_ROPE] = kpe[:, 0:MLA_ROPE]
    latt_ref[0:MLA_KV_LORA, :] = ckv.T
    latt_ref[MLA_KV_LORA:MLA_KV_LORA + MLA_ROPE, :] = kpe.T[0:MLA_ROPE, :]
    gm = gm_ref[...]
    qg, up = c_ref[4:5, 0:256], c_ref[5:6, 0:256]
    cos, sin = cosq_ref[...], sinq_ref[...]
    for c in range(8):
        a = _dot(cq, wuq_ref[:, c * 256:(c + 1) * 256])
        y = _rope(_group_norm(a, gm, qg), cos, sin, up, MLA_ROPE // 2)
        q_ref[:, c * 256:(c + 1) * 256] = (y * MLA_QK ** -0.5).astype(q_ref.dtype)


def _mla_cat_groups(width):
    g = []
    for lo in range(0, width, LANES):
        g += [(lo, lo + MLA_NOPE), (lo + MLA_NOPE, lo + MLA_QK)]
    return g


def _mla_cat_cols():
    idx = -np.ones((16, LANES), np.int64)
    for h in range(16):
        idx[h, :MLA_QK] = h * MLA_QK + np.arange(MLA_QK)
    return idx.reshape(-1)


def _take_cols(w, idx):
    wz = jnp.concatenate([w, jnp.zeros((w.shape[0], 1), w.dtype)], axis=1)
    return wz[:, np.where(idx < 0, w.shape[1], idx)]


def _mla_proj(x, g, w_dqkv, g_q, g_kv, w_uq, qn, kn, pos, tn, qdtype):
    n, d = x.shape
    period = pos.shape[0]
    w = jnp.concatenate([w_dqkv, jnp.zeros((d, 768 - w_dqkv.shape[1]), F32)], axis=1).astype(BF16)
    wuq = _take_cols(w_uq, _mla_cat_cols()).astype(BF16)
    cosq, sinq, upq = _rope_tables(pos, 256, [(MLA_NOPE, MLA_ROPE), (LANES + MLA_NOPE, MLA_ROPE)])
    cosk, sink, upk = _rope_tables(pos, 128, [(0, MLA_ROPE)])
    consts = jnp.zeros((8, 384), F32)
    consts = consts.at[0, :].set(g_q).at[1, 0:256].set(g_kv).at[2, 0:MLA_ROPE].set(kn[MLA_NOPE:])
    consts = consts.at[3, 0:128].set(jnp.asarray(upk[0])).at[5, 0:256].set(jnp.asarray(upq[0]))
    qgain = jnp.concatenate([qn, jnp.zeros((LANES - MLA_QK,), F32)])
    consts = consts.at[4, 0:256].set(jnp.tile(qgain, 2))
    gm = jnp.asarray(_group_mean_matrix(_mla_cat_groups(256), 256), BF16)
    nper = period // tn
    tabq = pl.BlockSpec((tn, 256), lambda i: (i % nper, 0))
    tabk = pl.BlockSpec((tn, 128), lambda i: (i % nper, 0))
    row = lambda wd: pl.BlockSpec((tn, wd), lambda i: (i, 0))
    return pl.pallas_call(
        _mla_proj_kernel,
        out_shape=(jax.ShapeDtypeStruct((n, 2048), qdtype), jax.ShapeDtypeStruct((n, 288), F32),
                   jax.ShapeDtypeStruct((n // period, 288, period), F32)),
        grid=(n // tn,),
        in_specs=[row(d), _full((1, d)), _full(w.shape), _full(wuq.shape), _full((256, 256)), _full((8, 384)),
                  tabq, tabq, tabk, tabk],
        out_specs=(row(2048), row(288), _t_spec(288, tn, nper)),
        compiler_params=_cparams("parallel"),
    )(x, g, w, wuq, gm, consts, jnp.asarray(cosq), jnp.asarray(sinq), jnp.asarray(cosk), jnp.asarray(sink))


def _mla_expand_kernel(lat_ref, wk_ref, wv_ref, gm_ref, c_ref, place_ref, k_ref, v_ref):
    ckv = lat_ref[:, 0:MLA_KV_LORA].astype(BF16)
    kpe = _dot(lat_ref[:, MLA_KV_LORA:MLA_KV_LORA + MLA_ROPE].astype(BF16), place_ref[...])
    gm = gm_ref[...]
    kg = c_ref[0:1, :]
    for c in range(8):
        e = _dot(ckv, wk_ref[:, c * 256:(c + 1) * 256])
        k_ref[:, c * 256:(c + 1) * 256] = (_group_norm(e, gm, kg) + kpe).astype(BF16)
    v_ref[...] = _dot(ckv, wv_ref[...]).astype(BF16)


def _mla_split_ukv(w_ukv):
    w = w_ukv.reshape(MLA_KV_LORA, 16, MLA_NOPE + MLA_V)
    return w[:, :, :MLA_NOPE].reshape(MLA_KV_LORA, 16 * MLA_NOPE), w[:, :, MLA_NOPE:].reshape(MLA_KV_LORA, 16 * MLA_V)


def _mla_expand(lat, w_ukv, kn, tn):
    n = lat.shape[0]
    wk_nat, wv = _mla_split_ukv(w_ukv)
    idx = -np.ones((16, LANES), np.int64)
    for h in range(16):
        idx[h, :MLA_NOPE] = h * MLA_NOPE + np.arange(MLA_NOPE)
    wk = _take_cols(wk_nat, idx.reshape(-1)).astype(BF16)
    gm = jnp.asarray(_group_mean_matrix([(lo, lo + MLA_NOPE) for lo in (0, LANES)], 256), BF16)
    kgain = jnp.concatenate([kn[:MLA_NOPE], jnp.zeros((LANES - MLA_NOPE,), F32)])
    consts = jnp.zeros((8, 256), F32).at[0].set(jnp.tile(kgain, 2))
    place = np.zeros((MLA_ROPE, 256), np.float32)
    for lo in (MLA_NOPE, LANES + MLA_NOPE):
        place[np.arange(MLA_ROPE), lo + np.arange(MLA_ROPE)] = 1.0
    row = lambda wd: pl.BlockSpec((tn, wd), lambda i: (i, 0))
    return pl.pallas_call(
        _mla_expand_kernel,
        out_shape=(jax.ShapeDtypeStruct((n, 2048), BF16), jax.ShapeDtypeStruct((n, 1024), BF16)),
        grid=(n // tn,),
        in_specs=[row(288), _full(wk.shape), _full((256, 1024)), _full((256, 256)), _full((8, 256)),
                  _full((MLA_ROPE, 256))],
        out_specs=(row(2048), row(1024)),
        compiler_params=_cparams("parallel"),
    )(lat, wk, wv.astype(BF16), gm, consts, jnp.asarray(place, BF16))


def _mla_prompt_kernel(q_ref, k_ref, v_ref, o_ref, m_ref, acc_ref):
    i = pl.program_id(2)
    tq = q_ref.shape[0]
    tk = tq
    q0 = q_ref[:, 0:128]
    q1 = q_ref[:, 128:256]

    def scores(off):
        return jnp.concatenate([_dot_nt(q0, k_ref[pl.ds(off, tk), 0:128]),
                                _dot_nt(q1, k_ref[pl.ds(off, tk), 128:256])], axis=0)

    offd = pl.multiple_of(i * tk, tk)
    qi = lax.broadcasted_iota(jnp.int32, (2 * tq, tk), 0) & (tq - 1)
    ki = lax.broadcasted_iota(jnp.int32, (2 * tq, tk), 1)
    _flash_first(jnp.where(ki <= qi, scores(offd), NEG),
                 lambda p: _dot(p, _with_ones(v_ref[pl.ds(offd, tk), :])), m_ref, None, acc_ref)

    def body(j, carry):
        off = pl.multiple_of(j * tk, tk)
        _flash_next(scores(off), lambda p: _dot(p, _with_ones(v_ref[pl.ds(off, tk), :])), m_ref, None, acc_ref)
        return carry

    lax.fori_loop(0, i, body, 0)
    o = _finish(acc_ref)
    lane = lax.broadcasted_iota(jnp.int32, (tq, LANES), 1)
    o_ref[...] = jnp.where(lane < MLA_V, o[0:tq, :], o[tq:2 * tq, :]).astype(o_ref.dtype)


def _mla_prompt_attn(q, k, v, batch, seq):
    tq = 512
    nq = seq // tq
    return pl.pallas_call(
        _mla_prompt_kernel,
        out_shape=jax.ShapeDtypeStruct((batch * seq, 1024), BF16),
        grid=(batch, 8, nq),
        in_specs=[pl.BlockSpec((tq, 256), lambda b, p, i: (b * nq + i, p)),
                  pl.BlockSpec((seq, 256), lambda b, p, i: (b, p)),
                  pl.BlockSpec((seq, 128), lambda b, p, i: (b, p))],
        out_specs=pl.BlockSpec((tq, 128), lambda b, p, i: (b * nq + i, p)),
        scratch_shapes=[pltpu.VMEM((2 * tq, 128), F32), pltpu.VMEM((2 * tq, 256), F32)],
        compiler_params=_cparams("parallel", "parallel", "arbitrary"),
    )(q, k, v)


def _mla_sample_kernel(pt_ref, q_ref, latn_ref, wk_ref, wkt_ref, wv_ref, kg_ref, *refs, pps):
    pages = refs[:pps]
    o_ref = refs[pps]
    qt_ref, qpe_ref, m_ref, l_ref, acc_ref = refs[pps + 1:]
    s = pl.program_id(1)
    lane8 = lax.broadcasted_iota(jnp.int32, (8, LANES), 1)

    @pl.when(s == 0)
    def _():
        z = jnp.zeros((8, LANES), F32)
        qg_rows, pe_rows = [], []
        for h in range(16):
            ch = q_ref[:, h * LANES:(h + 1) * LANES]
            nope = jnp.where(lane8 < MLA_NOPE, ch, 0.0)
            if h % 2:
                nope = pltpu.roll(nope, MLA_NOPE, 1)
            qg_rows.append(jnp.concatenate([z] * (h // 2) + [nope] + [z] * (7 - h // 2), axis=1))
            pe_rows.append(pltpu.roll(ch, LANES - MLA_NOPE, 1)[:, 0:MLA_ROPE])
        qg = (jnp.concatenate(qg_rows, axis=0) * kg_ref[...]).astype(BF16)
        qt_ref[...] = _dot_nt(qg, wk_ref[...]).astype(BF16)
        qpe_ref[...] = jnp.concatenate(pe_rows, axis=0).astype(BF16)
        _flash_init(m_ref, l_ref, acc_ref)

    def update(lat_t, ok):
        ckv = lat_t[0:MLA_KV_LORA, :].astype(BF16)
        kpe = lat_t[MLA_KV_LORA:MLA_KV_LORA + MLA_ROPE, :].astype(BF16)
        n = lat_t.shape[1]
        e = _dot(wkt_ref[...], ckv)
        ms = jnp.sum((e * e).reshape(16, MLA_NOPE, n), axis=1) * (1.0 / MLA_NOPE)
        rs = lax.rsqrt(ms + NORM_EPS)
        rs = jnp.broadcast_to(rs[:, None, :], (16, 8, n)).reshape(LANES, n)
        sc = _dot(qt_ref[...], ckv) * rs + _dot(qpe_ref[...], kpe)
        pv = lambda p: _dot_nt(p, ckv)
        if ok is None:
            _flash_next(sc, pv, m_ref, l_ref, acc_ref)
        else:
            _flash_masked(sc, ok, pv, m_ref, l_ref, acc_ref)

    update(jnp.concatenate([pg[...] for pg in pages], axis=1), None)

    @pl.when(s == pl.num_programs(1) - 1)
    def _():
        rho = lax.broadcasted_iota(jnp.int32, (LANES, LANES), 0)
        t = lax.broadcasted_iota(jnp.int32, (LANES, LANES), 1)
        update(latn_ref[...], t <= (rho & 7))
        olat = (acc_ref[...] / _rep(l_ref[...], 2)).astype(BF16)
        ofull = _dot(olat, wv_ref[...])
        r_head = lax.broadcasted_iota(jnp.int32, (LANES, 1024), 0) >> 3
        c_head = lax.broadcasted_iota(jnp.int32, (LANES, 1024), 1) >> 6
        ofull = jnp.where(r_head == c_head, ofull, 0.0)
        out = ofull[0:8, :]
        for h in range(1, 16):
            out = out + ofull[h * 8:(h + 1) * 8, :]
        o_ref[...] = out


def _mla_sample_attn(q, latn_t, cache_t, page_table, w_ukv, kn, pps):
    db, n_pages = page_table.shape
    nsteps = n_pages // pps
    wk, wv = _mla_split_ukv(w_ukv)
    wk = wk.astype(BF16)
    kg = jnp.tile(kn[:MLA_NOPE], 16).reshape(1, 1024)

    def page_spec(t):
        return pl.BlockSpec((None, 288, PAGE), lambda b, s, pt: (pt[b, s * pps + t], 0, 0))

    grid_spec = pltpu.PrefetchScalarGridSpec(
        num_scalar_prefetch=1,
        grid=(db, nsteps),
        in_specs=[pl.BlockSpec((8, 2048), lambda b, s, pt: (b, 0)),
                  pl.BlockSpec((None, 288, LANES), lambda b, s, pt: (b, 0, 0)),
                  _full((256, 1024)), _full((1024, 256)), _full((256, 1024)), _full((1, 1024))]
                 + [page_spec(t) for t in range(pps)],
        out_specs=pl.BlockSpec((8, 1024), lambda b, s, pt: (b, 0)),
        scratch_shapes=[pltpu.VMEM((128, 256), BF16), pltpu.VMEM((128, MLA_ROPE), BF16),
                        pltpu.VMEM((128, 128), F32), pltpu.VMEM((128, 128), F32), pltpu.VMEM((128, 256), F32)],
    )
    return pl.pallas_call(
        functools.partial(_mla_sample_kernel, pps=pps),
        out_shape=jax.ShapeDtypeStruct((db * 8, 1024), F32),
        grid_spec=grid_spec,
        compiler_params=_cparams("parallel", "arbitrary"),
    )(page_table, q, latn_t, wk, wk.T, wv.astype(BF16), kg, *([cache_t] * pps))


def _mla_layer(xp, xs, cache, page_table, ln_g, w_dqkv, g_q, g_kv, w_uq, w_ukv, qn, kn, w_o,
               batch, seq, past_len):
    g = ln_g.reshape(1, -1)
    w_o_b = w_o.astype(BF16)
    dec = 8
    q, lat, latt = _mla_proj(xp, g, w_dqkv, g_q, g_kv, w_uq, qn, kn, np.arange(seq), 512, BF16)
    k, v = _mla_expand(lat, w_ukv, kn, 512)
    o = _mla_prompt_attn(q, k, v, batch, seq)
    yp = _outproj(xp, o, w_o_b, 512)
    ns = xs.shape[0]
    pos_s = np.tile(past_len + np.arange(dec), ns // dec)
    qs, lats, _ = _mla_proj(xs, g, w_dqkv, g_q, g_kv, w_uq, qn, kn, pos_s, ns, F32)
    os_ = _mla_sample_attn(qs, _rows_t(lats, page_table.shape[0]), _pages_t(cache), page_table, w_ukv, kn,
                           min(16, page_table.shape[1]))
    ys = _outproj(xs, os_, w_o_b, ns)
    return yp, ys, latt, lats


def kernel(x_prompt, x_sample, cache_kv_0, cache_kv_1, state_win_1, cache_lat_2, cache_kv_3, page_table, ln1_g, ln2_g, mlp_w1, mlp_w2, moba_w_in_0, moba_qn_0, moba_kn_0, moba_w_o_0, nsa_w_in_1, nsa_qn_1, nsa_kn_1, nsa_cmp_pe_1, nsa_cmp_w1_1, nsa_cmp_b1_1, nsa_cmp_w2_1, nsa_w_o_1, mla_w_dqkv_2, mla_g_q_2, mla_g_kv_2, mla_w_uq_2, mla_w_ukv_2, mla_qn_2, mla_kn_2, mla_w_o_2, moba_w_in_3, moba_qn_3, moba_kn_3, moba_w_o_3):
    batch, seq, d = x_prompt.shape
    db, dec, _ = x_sample.shape
    past = page_table.shape[1] * PAGE
    assert dec == 8 and seq % 512 == 0
    xp = x_prompt.reshape(batch * seq, d)
    xs = x_sample.reshape(db * dec, d)

    def mlp(i, xp, xs):
        w1 = mlp_w1[i].astype(BF16)
        w2 = mlp_w2[i].astype(BF16)
        g = ln2_g[i].reshape(1, d)
        return _mlp(xp, g, w1, w2, 512), _mlp(xs, g, w1, w2, db * dec)

    xp, xs, kv0_p, kv0_s = _moba_layer(xp, xs, cache_kv_0, page_table, ln1_g[0], moba_w_in_0, moba_qn_0,
                                       moba_kn_0, moba_w_o_0, batch, seq, past)
    xp, xs = mlp(0, xp, xs)
    xp, xs, kv1_p, kv1_s = _nsa_layer(xp, xs, cache_kv_1, state_win_1, page_table, ln1_g[1], nsa_w_in_1,
                                      nsa_qn_1, nsa_kn_1, nsa_cmp_pe_1, nsa_cmp_w1_1, nsa_cmp_b1_1,
                                      nsa_cmp_w2_1, nsa_w_o_1, batch, seq, past)
    xp, xs = mlp(1, xp, xs)
    xp, xs, lat_p, lat_s = _mla_layer(xp, xs, cache_lat_2, page_table, ln1_g[2], mla_w_dqkv_2, mla_g_q_2,
                                      mla_g_kv_2, mla_w_uq_2, mla_w_ukv_2, mla_qn_2, mla_kn_2, mla_w_o_2,
                                      batch, seq, past)
    xp, xs = mlp(2, xp, xs)
    xp, xs, kv3_p, kv3_s = _moba_layer(xp, xs, cache_kv_3, page_table, ln1_g[3], moba_w_in_3, moba_qn_3,
                                       moba_kn_3, moba_w_o_3, batch, seq, past)
    xp, xs = mlp(3, xp, xs)

    def rows_major(x_t, *feat):
        return jnp.moveaxis(x_t.reshape((batch,) + feat + (x_t.shape[-1],)), -1, 1)

    wb_p = min(NSA_WINDOW, seq)
    win_new = kv1_s[:, 512:768].reshape(db, dec, 2, NSA_KV_HEADS, HEAD_DIM)
    win_s = jnp.concatenate([state_win_1, win_new], axis=1)[:, dec:]
    return (xp.reshape(batch, seq, d), xs.reshape(db, dec, d),
            rows_major(kv0_p, 2, MOBA_KV_HEADS, HEAD_DIM),
            rows_major(kv1_p[:, :512], 4, NSA_KV_HEADS, HEAD_DIM),
            rows_major(kv1_p[:, 512:768, seq - wb_p:], 2, NSA_KV_HEADS, HEAD_DIM),
            rows_major(lat_p, MLA_KV_LORA + MLA_ROPE),
            rows_major(kv3_p, 2, MOBA_KV_HEADS, HEAD_DIM),
            kv0_s.reshape(db, dec, 2, MOBA_KV_HEADS, HEAD_DIM),
            kv1_s[:, :512].reshape(db, dec, 4, NSA_KV_HEADS, HEAD_DIM),
            win_s,
            lat_s.reshape(db, dec, MLA_KV_LORA + MLA_ROPE),
            kv3_s.reshape(db, dec, 2, MOBA_KV_HEADS, HEAD_DIM))
```

```python
import functools

import numpy as np
import jax
import jax.numpy as jnp
from jax import lax
from jax.experimental import pallas as pl
from jax.experimental.pallas import tpu as pltpu

F32 = jnp.float32
BF16 = jnp.bfloat16

HEAD_DIM = 64
ROPE_THETA = 10000.0
NORM_EPS = 1e-6
PAGE = 128
MOBA_KV_HEADS = 4
MOBA_BLOCK = 256
MOBA_TOPK = 3
NSA_KV_HEADS = 2
NSA_CMP_LEN = 32
NSA_CMP_STRIDE = 16
NSA_CMP_HIDDEN = 256
NSA_SLC_BLOCK = 64
NSA_SLC_TOPN = 16
NSA_WINDOW = 512
NSA_FORCE_SCORE = 1e9
MLA_Q_LORA = 384
MLA_KV_LORA = 256
MLA_NOPE = 64
MLA_ROPE = 32
MLA_V = 64

LANES = 128
VMEM_LIMIT_BYTES = 56 * 1024 * 1024
NEG = -1e30
TINY = float(np.finfo(np.float32).tiny)
LOG2E = 1.4426950408889634


def _cparams(*sem):
    return pltpu.CompilerParams(dimension_semantics=sem, vmem_limit_bytes=VMEM_LIMIT_BYTES)


def _dot(a, b):
    return jnp.dot(a, b, preferred_element_type=F32)


def _dot_nt(a, b):
    return lax.dot_general(a, b, (((1,), (1,)), ((), ())), preferred_element_type=F32)


def _dot_tn(a, b):
    return lax.dot_general(a, b, (((0,), (0,)), ((), ())), preferred_element_type=F32)


def _split_bf16(x):
    hi = x.astype(BF16)
    lo = (x - hi.astype(F32)).astype(BF16)
    return hi, lo


def _full(shape):
    n = len(shape)
    return pl.BlockSpec(shape, lambda *_: (0,) * n)


def _slot_perm(kv_heads, n_heads=16):
    grp = n_heads // kv_heads
    cols = []
    for s in range(n_heads // 2):
        p, r = divmod(s, grp)
        for h in ((2 * p) * grp + r, (2 * p + 1) * grp + r):
            cols.append(h * HEAD_DIM + np.arange(HEAD_DIM))
    return np.concatenate(cols)


def _slot_heads(kv_heads, n_heads=16):
    grp = n_heads // kv_heads
    out = []
    for s in range(n_heads // 2):
        p, r = divmod(s, grp)
        out.append(((2 * p) * grp + r, (2 * p + 1) * grp + r))
    return out


def _group_mean_matrix(groups, width):
    m = np.zeros((width, width), np.float32)
    for lo, hi in groups:
        m[lo:hi, lo:hi] = 1.0 / (hi - lo)
    return m


def _rope_tables(pos, width, segs):
    pos = np.asarray(pos, np.float64)
    cos = np.ones((pos.shape[0], width), np.float64)
    sin = np.zeros((pos.shape[0], width), np.float64)
    up = np.zeros((1, width), np.float32)
    for lo, dim in segs:
        half = dim // 2
        inv = ROPE_THETA ** (-np.arange(half, dtype=np.float64) / half)
        ang = pos[:, None] * inv[None, :]
        cos[:, lo:lo + half] = np.cos(ang)
        cos[:, lo + half:lo + dim] = np.cos(ang)
        sin[:, lo:lo + half] = -np.sin(ang)
        sin[:, lo + half:lo + dim] = np.sin(ang)
        up[:, lo:lo + half] = 1.0
    return cos.astype(np.float32), sin.astype(np.float32), up


def _head64_segs(width):
    return [(lo, HEAD_DIM) for lo in range(0, width, HEAD_DIM)]


def _rms_rows(x, g):
    ms = jnp.mean(x * x, axis=-1, keepdims=True)
    return x * lax.rsqrt(ms + NORM_EPS) * g


def _group_norm(a, gm, gain):
    ms = _dot((a * a).astype(BF16), gm)
    return a * lax.rsqrt(ms + NORM_EPS) * gain


def _rope(y, cos, sin, up, shift):
    w = y.shape[-1]
    hi = pltpu.roll(y, w - shift, 1)
    lo = pltpu.roll(y, shift, 1)
    partner = jnp.where(up > 0.5, hi, lo)
    return y * cos + partner * sin


def _gelu_tanh(x):
    return 0.5 * x * (1.0 + jnp.tanh(0.7978845608028654 * (x + 0.044715 * (x * x * x))))


def _mixer_out_mlp_kernel(x_ref, o_ref, wo_ref, g_ref, w1_ref, w2_ref, y_ref, *, ff_chunk):
    x = x_ref[...] + _dot(o_ref[...].astype(BF16), wo_ref[...])
    h = _rms_rows(x, g_ref[...]).astype(BF16)
    acc = x
    for c in range(w1_ref.shape[1] // ff_chunk):
        u = _dot(h, w1_ref[:, c * ff_chunk:(c + 1) * ff_chunk])
        u = jnp.maximum(u, 0.0)
        acc = acc + _dot((u * u).astype(BF16), w2_ref[c * ff_chunk:(c + 1) * ff_chunk, :])
    y_ref[...] = acc


def _mixer_out_mlp(x, o, wo, g, w1, w2, tn):
    n, d = x.shape
    k = o.shape[1]
    ff = w1.shape[1]
    row = lambda wd: pl.BlockSpec((tn, wd), lambda i: (i, 0))
    return pl.pallas_call(
        functools.partial(_mixer_out_mlp_kernel, ff_chunk=1024),
        out_shape=jax.ShapeDtypeStruct((n, d), F32),
        grid=(n // tn,),
        in_specs=[row(d), row(k), _full((k, d)), _full((1, d)), _full((d, ff)), _full((ff, d))],
        out_specs=row(d),
        compiler_params=_cparams("parallel"),
    )(x, o, wo, g, w1, w2)


def _moba_proj_kernel(x_ref, g_ref, w_ref, gm_ref, c_ref, cos_ref, sin_ref, q_ref, kv_ref, kvb_ref, kvt_ref):
    h = _rms_rows(x_ref[...], g_ref[...]).astype(BF16)
    gm = gm_ref[...]
    cos, sin = cos_ref[...], sin_ref[...]
    qg, kg, up = c_ref[0:1, :], c_ref[1:2, :], c_ref[2:3, :]
    for c in range(4):
        a = _dot(h, w_ref[:, c * 256:(c + 1) * 256])
        y = _rope(_group_norm(a, gm, qg), cos, sin, up, HEAD_DIM // 2)
        q_ref[:, c * 256:(c + 1) * 256] = (y * (HEAD_DIM ** -0.5 * LOG2E)).astype(q_ref.dtype)
    a = _dot(h, w_ref[:, 1024:1280])
    k = _rope(_group_norm(a, gm, kg), cos, sin, up, HEAD_DIM // 2)
    v = _dot(h, w_ref[:, 1280:1536])
    kv_ref[:, 0:256] = k
    kv_ref[:, 256:512] = v
    kvb_ref[:, 0:256] = k.astype(BF16)
    kvb_ref[:, 256:512] = v.astype(BF16)
    kvt_ref[0:256, :] = k.T
    kvt_ref[256:512, :] = v.T


def _t_spec(feat, tn, nper):
    return pl.BlockSpec((None, feat, tn), lambda i: (i // nper, 0, i % nper))


def _moba_proj(x, g, w, qn, kn, pos, tn, qdtype):
    n, d = x.shape
    period = pos.shape[0]
    cos, sin, up = _rope_tables(pos, 256, _head64_segs(256))
    consts = np.zeros((8, 256), np.float32)
    consts[2] = up[0]
    consts = jnp.asarray(consts).at[0].set(jnp.tile(qn, 4)).at[1].set(jnp.tile(kn, 4))
    gm = jnp.asarray(_group_mean_matrix([(lo, lo + 64) for lo in range(0, 256, 64)], 256), BF16)
    nper = period // tn
    tab = pl.BlockSpec((tn, 256), lambda i: (i % nper, 0))
    return pl.pallas_call(
        _moba_proj_kernel,
        out_shape=(jax.ShapeDtypeStruct((n, 1024), qdtype),
                   jax.ShapeDtypeStruct((n, 512), F32),
                   jax.ShapeDtypeStruct((n, 512), BF16),
                   jax.ShapeDtypeStruct((n // period, 512, period), F32)),
        grid=(n // tn,),
        in_specs=[pl.BlockSpec((tn, d), lambda i: (i, 0)), _full((1, d)), _full(w.shape),
                  _full((256, 256)), _full((8, 256)), tab, tab],
        out_specs=(pl.BlockSpec((tn, 1024), lambda i: (i, 0)),
                   pl.BlockSpec((tn, 512), lambda i: (i, 0)),
                   pl.BlockSpec((tn, 512), lambda i: (i, 0)),
                   _t_spec(512, tn, nper)),
        compiler_params=_cparams("parallel"),
    )(x, g, w, gm, consts, jnp.asarray(cos), jnp.asarray(sin))


def _topk_mask_t(score_t, k, n_valid):
    nc, n = score_t.shape
    cand = lax.broadcasted_iota(jnp.int32, (nc, n), 0)
    rank = jnp.zeros((nc, n), jnp.int32)
    for j in range(n_valid):
        sj = score_t[j:j + 1, :]
        ahead = (sj > score_t) | ((sj == score_t) & (cand > j))
        rank = rank + ahead.astype(jnp.int32)
    return rank < k


def _bias_rows(sel_t):
    nc, n = sel_t.shape
    bias_t = jnp.where(sel_t, 0.0, NEG)
    if nc < LANES:
        bias_t = jnp.concatenate([bias_t, jnp.zeros((LANES - nc, n), F32)], axis=0)
    return bias_t.T


def _rep(x, n):
    return x if n == 1 else jnp.concatenate([x] * n, axis=1)


def _flash_first(s, pv, m_ref, l_ref, acc_ref):
    m = jnp.max(s, axis=-1, keepdims=True)
    p = jnp.exp2(s - m)
    m_ref[...] = jnp.broadcast_to(m, m_ref.shape)
    if l_ref is not None:
        l_ref[...] = jnp.broadcast_to(jnp.sum(p, axis=-1, keepdims=True), l_ref.shape)
    acc_ref[...] = pv(p.astype(BF16))


def _flash_masked(s, ok, pv, m_ref, l_ref, acc_ref):
    s = jnp.where(ok, s, NEG)
    m_old = m_ref[...]
    m_new = jnp.maximum(m_old, jnp.max(s, axis=-1, keepdims=True))
    p = jnp.where(ok, jnp.exp2(s - _rep(m_new, s.shape[1] // LANES)), 0.0)
    alpha = jnp.exp2(m_old - m_new)
    m_ref[...] = m_new
    if l_ref is not None:
        l_ref[...] = alpha * l_ref[...] + jnp.sum(p, axis=-1, keepdims=True)
    acc_ref[...] = _rep(alpha, acc_ref.shape[1] // LANES) * acc_ref[...] + pv(p.astype(BF16))


def _flash_next(s, pv, m_ref, l_ref, acc_ref):
    m_old = m_ref[...]
    m_new = jnp.maximum(m_old, jnp.max(s, axis=-1, keepdims=True))
    p = jnp.exp2(s - _rep(m_new, s.shape[1] // LANES))
    alpha = jnp.exp2(m_old - m_new)
    m_ref[...] = m_new
    if l_ref is not None:
        l_ref[...] = alpha * l_ref[...] + jnp.sum(p, axis=-1, keepdims=True)
    acc_ref[...] = _rep(alpha, acc_ref.shape[1] // LANES) * acc_ref[...] + pv(p.astype(BF16))


def _with_ones(v):
    return jnp.concatenate([v, jnp.ones(v.shape, BF16)], axis=1)


def _finish(acc_ref):
    acc = acc_ref[...]
    return acc[:, 0:LANES] / acc[:, LANES:2 * LANES]


def _slot_rows(q_ref, n_slots):
    tq = q_ref.shape[0]
    lane = lax.broadcasted_iota(jnp.int32, (tq, LANES), 1)
    ev, od = [], []
    for s in range(n_slots):
        qs = q_ref[:, s * LANES:(s + 1) * LANES]
        ev.append(jnp.where(lane < HEAD_DIM, qs, jnp.zeros_like(qs)))
        od.append(jnp.where(lane >= HEAD_DIM, qs, jnp.zeros_like(qs)))
    return jnp.concatenate(ev + od, axis=0)


def _slot_out(o, n_slots, tq):
    lane = lax.broadcasted_iota(jnp.int32, (tq, LANES), 1)
    outs = []
    for s in range(n_slots):
        e = o[s * tq:(s + 1) * tq, :]
        d = o[(n_slots + s) * tq:(n_slots + s + 1) * tq, :]
        outs.append(jnp.where(lane < HEAD_DIM, e, d))
    return jnp.concatenate(outs, axis=1)


def _moba_prompt_kernel(q_ref, k_ref, v_ref, bm_ref, o_ref, m_ref, acc_ref):
    i = pl.program_id(2)
    tq = q_ref.shape[0]
    nb = k_ref.shape[0] // MOBA_BLOCK
    rows = 8 * tq
    qa = _slot_rows(q_ref, 4)
    bhi, blo = _split_bf16(bm_ref[...])
    gate_t = _dot_nt(bhi, qa) + _dot_nt(blo, qa)
    blk = lax.broadcasted_iota(jnp.int32, (nb, rows), 0)
    gate_t = jnp.where(blk < i, gate_t, -jnp.inf)
    sel_t = _topk_mask_t(gate_t, MOBA_TOPK, nb) & (blk < i)
    qaug = jnp.concatenate([qa, _bias_rows(sel_t).astype(BF16)], axis=1)

    kd = k_ref[pl.ds(pl.multiple_of(i * MOBA_BLOCK, MOBA_BLOCK), MOBA_BLOCK), :]
    vd = v_ref[pl.ds(pl.multiple_of(i * MOBA_BLOCK, MOBA_BLOCK), MOBA_BLOCK), :]
    qi = lax.broadcasted_iota(jnp.int32, (rows, MOBA_BLOCK), 0) & (tq - 1)
    ki = lax.broadcasted_iota(jnp.int32, (rows, MOBA_BLOCK), 1)
    _flash_first(jnp.where(ki <= qi, _dot_nt(qa, kd), NEG), lambda p: _dot(p, _with_ones(vd)),
                 m_ref, None, acc_ref)

    lane = lax.broadcasted_iota(jnp.int32, (MOBA_BLOCK, LANES), 1)
    for j in range(nb - 1):
        @pl.when(j < i)
        def _(j=j):
            onehot = jnp.where(lane == j, 1.0, 0.0).astype(BF16)
            kj = jnp.concatenate([k_ref[j * MOBA_BLOCK:(j + 1) * MOBA_BLOCK, :], onehot], axis=1)
            vj = _with_ones(v_ref[j * MOBA_BLOCK:(j + 1) * MOBA_BLOCK, :])
            _flash_next(_dot_nt(qaug, kj), lambda p: _dot(p, vj), m_ref, None, acc_ref)

    o_ref[...] = _slot_out(_finish(acc_ref), 4, tq).astype(o_ref.dtype)


def _moba_prompt_attn(q, kvb, bm, batch, seq):
    tq = MOBA_BLOCK
    nq = seq // tq
    nb = seq // MOBA_BLOCK
    return pl.pallas_call(
        _moba_prompt_kernel,
        out_shape=jax.ShapeDtypeStruct((batch * seq, 1024), BF16),
        grid=(batch, 2, nq),
        in_specs=[pl.BlockSpec((tq, 512), lambda b, p, i: (b * nq + i, p)),
                  pl.BlockSpec((seq, 128), lambda b, p, i: (b, p)),
                  pl.BlockSpec((seq, 128), lambda b, p, i: (b, 2 + p)),
                  pl.BlockSpec((nb, 128), lambda b, p, i: (b, p))],
        out_specs=pl.BlockSpec((tq, 512), lambda b, p, i: (b * nq + i, p)),
        scratch_shapes=[pltpu.VMEM((8 * tq, 128), F32), pltpu.VMEM((8 * tq, 256), F32)],
        compiler_params=_cparams("parallel", "parallel", "arbitrary"),
    )(q, kvb, kvb, bm)


def _bmean_kernel(k_ref, o_ref):
    nb = o_ref.shape[0]
    k = k_ref[...].reshape(nb, MOBA_BLOCK, k_ref.shape[1])
    o_ref[...] = jnp.sum(k, axis=1) * (1.0 / MOBA_BLOCK)


def _moba_bmean(kv, batch, seq):
    nb = seq // MOBA_BLOCK
    return pl.pallas_call(
        _bmean_kernel,
        out_shape=jax.ShapeDtypeStruct((batch * nb, 256), F32),
        grid=(batch,),
        in_specs=[pl.BlockSpec((seq, 256), lambda b: (b, 0))],
        out_specs=pl.BlockSpec((nb, 256), lambda b: (b, 0)),
        compiler_params=_cparams("parallel"),
    )(kv)


def _moba_sample_kernel(pt_ref, q_ref, kvn_ref, *refs, bps, nb_past):
    pages = refs[:2 * bps]
    o_ref = refs[2 * bps]
    qa_ref, bm_ref, mst_ref, lst_ref, oacc_ref = refs[2 * bps + 1:]
    s = pl.program_id(1)
    rows = 128
    lane128 = lax.broadcasted_iota(jnp.int32, (8, LANES), 1)

    @pl.when(s == 0)
    def _():
        q = q_ref[...]
        z = jnp.zeros((8, LANES), F32)
        ev, od = [], []
        for sl in range(8):
            qs = q[:, sl * LANES:(sl + 1) * LANES]
            e = jnp.where(lane128 < HEAD_DIM, qs, 0.0)
            d = jnp.where(lane128 >= HEAD_DIM, qs, 0.0)
            if sl // 4 == 0:
                ev.append(jnp.concatenate([e, z], axis=1))
                od.append(jnp.concatenate([d, z], axis=1))
            else:
                ev.append(jnp.concatenate([z, e], axis=1))
                od.append(jnp.concatenate([z, d], axis=1))
        qa_ref[...] = jnp.concatenate(ev + od, axis=0).astype(BF16)
        bm_ref[...] = jnp.zeros_like(bm_ref)
        mst_ref[...] = jnp.zeros_like(mst_ref)
        lst_ref[...] = jnp.zeros_like(lst_ref)

    qa = qa_ref[...]
    col = lax.broadcasted_iota(jnp.int32, (rows, LANES), 1)
    for t in range(bps):
        jg = s * bps + t
        kt = jnp.concatenate([pages[2 * t][0:256, :], pages[2 * t + 1][0:256, :]], axis=1)
        vt = jnp.concatenate([pages[2 * t][256:512, :], pages[2 * t + 1][256:512, :]], axis=1)
        sc = _dot(qa, kt.astype(BF16))
        bm_ref[...] = jnp.where(col == jg, jnp.sum(sc, axis=-1, keepdims=True), bm_ref[...])
        m = jnp.max(sc, axis=-1, keepdims=True)
        p = jnp.exp2(sc - m)
        oacc_ref[jg] = _dot_nt(p.astype(BF16), vt.astype(BF16))
        mst_ref[...] = jnp.where(col == jg, m, mst_ref[...])
        lst_ref[...] = jnp.where(col == jg, jnp.sum(p, axis=-1, keepdims=True), lst_ref[...])

    @pl.when(s == pl.num_programs(1) - 1)
    def _():
        kn = kvn_ref[0:256, :].astype(BF16)
        vn = kvn_ref[256:512, :].astype(BF16)
        sn = _dot(qa, kn)
        qi = lax.broadcasted_iota(jnp.int32, (rows, LANES), 0) & 7
        okn = col <= qi
        sn = jnp.where(okn, sn, NEG)
        mn = jnp.max(sn, axis=-1, keepdims=True)
        pn = jnp.where(okn, jnp.exp2(sn - mn), 0.0)
        ln = jnp.sum(pn, axis=-1, keepdims=True)
        on = _dot_nt(pn.astype(BF16), vn)
        gate_t = bm_ref[...].T[0:nb_past, :]
        sel_t = jnp.where(_topk_mask_t(gate_t, MOBA_TOPK, nb_past), 1.0, 0.0)
        sel = jnp.concatenate([sel_t, jnp.zeros((LANES - nb_past, rows), F32)], axis=0).T > 0.5
        mst = mst_ref[...]
        mstar = jnp.maximum(jnp.max(jnp.where(sel, mst, NEG), axis=-1, keepdims=True), mn)
        w = jnp.where(sel, jnp.exp2(mst - mstar), 0.0)
        wn = jnp.exp2(mn - mstar)
        den = jnp.sum(w * lst_ref[...], axis=-1, keepdims=True) + wn * ln
        acc = wn * on
        for j in range(nb_past):
            acc = acc + w[:, j:j + 1] * oacc_ref[j]
        o = acc / den
        outs = []
        for sl in range(8):
            c = (sl // 4) * LANES
            e = o[sl * 8:(sl + 1) * 8, c:c + LANES]
            d = o[(8 + sl) * 8:(9 + sl) * 8, c:c + LANES]
            outs.append(jnp.where(lane128 < HEAD_DIM, e, d))
        o_ref[...] = jnp.concatenate(outs, axis=1)


def _pages_t(cache):
    return jnp.swapaxes(cache.reshape(cache.shape[0], PAGE, -1), 1, 2)


def _rows_t(x, db):
    xt = jnp.swapaxes(x.reshape(db, x.shape[0] // db, x.shape[1]), 1, 2)
    return jnp.pad(xt, ((0, 0), (0, 0), (0, LANES - xt.shape[2])))


def _moba_sample_attn(q, kvn_t, cache_t, page_table, bps):
    db, n_pages = page_table.shape
    nb_past = n_pages * PAGE // MOBA_BLOCK
    assert n_pages % (2 * bps) == 0 and nb_past < LANES
    nsteps = n_pages // (2 * bps)

    def page_spec(t):
        return pl.BlockSpec((None, 512, PAGE), lambda b, s, pt: (pt[b, s * 2 * bps + t], 0, 0))

    grid_spec = pltpu.PrefetchScalarGridSpec(
        num_scalar_prefetch=1,
        grid=(db, nsteps),
        in_specs=[pl.BlockSpec((8, 1024), lambda b, s, pt: (b, 0)),
                  pl.BlockSpec((None, 512, LANES), lambda b, s, pt: (b, 0, 0))]
                 + [page_spec(t) for t in range(2 * bps)],
        out_specs=pl.BlockSpec((8, 1024), lambda b, s, pt: (b, 0)),
        scratch_shapes=[pltpu.VMEM((128, 256), BF16), pltpu.VMEM((128, LANES), F32),
                        pltpu.VMEM((128, LANES), F32), pltpu.VMEM((128, LANES), F32),
                        pltpu.VMEM((nb_past, 128, 256), F32)],
    )
    return pl.pallas_call(
        functools.partial(_moba_sample_kernel, bps=bps, nb_past=nb_past),
        out_shape=jax.ShapeDtypeStruct((db * 8, 1024), F32),
        grid_spec=grid_spec,
        compiler_params=_cparams("parallel", "arbitrary"),
    )(page_table, q, kvn_t, *([cache_t] * (2 * bps)))


def _moba_layer(xp, xs, cache, page_table, ln_g, w_in, qn, kn, w_o, batch, seq, past_len):
    perm = _slot_perm(MOBA_KV_HEADS)
    w_in_p = jnp.concatenate([w_in[:, :1024][:, perm], w_in[:, 1024:]], axis=1).astype(BF16)
    w_o_p = w_o[perm, :].astype(BF16)
    g = ln_g.reshape(1, -1)
    dec = 8
    q, kv, kvb, kvt = _moba_proj(xp, g, w_in_p, qn, kn, np.arange(seq), 512, BF16)
    bm = _moba_bmean(kv, batch, seq)
    o = _moba_prompt_attn(q, kvb, bm, batch, seq)
    ns = xs.shape[0]
    pos_s = np.tile(past_len + np.arange(dec), ns // dec)
    qs, kvs, _, _ = _moba_proj(xs, g, w_in_p, qn, kn, pos_s, ns, F32)
    db, n_pages = page_table.shape
    bps = max(b for b in (1, 2, 4, 8) if n_pages % (2 * b) == 0)
    os_ = _moba_sample_attn(qs, _rows_t(kvs, db), _pages_t(cache), page_table, bps)
    return o, os_, w_o_p, kvt, kvs


def _nsa_proj_kernel(x_ref, g_ref, w_ref, gm_ref, c_ref, cos_ref, sin_ref,
                     q_ref, kv_ref, kvb_ref, gate_ref, kvt_ref):
    h = _rms_rows(x_ref[...], g_ref[...]).astype(BF16)
    gm = gm_ref[...]
    cos, sin = cos_ref[...], sin_ref[...]
    qg, up = c_ref[0:1, :], c_ref[4:5, :]
    for c in range(4):
        a = _dot(h, w_ref[:, c * 256:(c + 1) * 256])
        y = _rope(_group_norm(a, gm, qg), cos, sin, up, HEAD_DIM // 2)
        q_ref[:, c * 256:(c + 1) * 256] = (y * (HEAD_DIM ** -0.5 * LOG2E)).astype(q_ref.dtype)
    gm1 = gm_ref[0:128, 0:128]
    for br in range(3):
        a = _dot(h, w_ref[:, 1024 + br * 256:1024 + (br + 1) * 256])
        k = _rope(_group_norm(a[:, 0:128], gm1, c_ref[1 + br:2 + br, 0:128]),
                  cos[:, 0:128], sin[:, 0:128], up[:, 0:128], HEAD_DIM // 2)
        v = a[:, 128:256]
        kv_ref[:, br * 256:br * 256 + 128] = k
        kv_ref[:, br * 256 + 128:(br + 1) * 256] = v
        kvb_ref[:, br * 256:br * 256 + 128] = k.astype(BF16)
        kvb_ref[:, br * 256 + 128:(br + 1) * 256] = v.astype(BF16)
        kvt_ref[br * 256:br * 256 + 128, :] = k.T
        kvt_ref[br * 256 + 128:(br + 1) * 256, :] = v.T
    a = _dot(h, w_ref[:, 1792:1920])
    gate_ref[...] = 1.0 / (1.0 + jnp.exp(-a))


def _nsa_proj(x, g, w, qn, kn, pos, tn, qdtype):
    n, d = x.shape
    period = pos.shape[0]
    cos, sin, up = _rope_tables(pos, 256, _head64_segs(256))
    consts = np.zeros((8, 256), np.float32)
    consts[4] = up[0]
    consts = jnp.asarray(consts).at[0].set(jnp.tile(qn, 4))
    for br in range(3):
        consts = consts.at[1 + br].set(jnp.tile(kn[br], 4))
    gm = jnp.asarray(_group_mean_matrix([(lo, lo + 64) for lo in range(0, 256, 64)], 256), BF16)
    nper = period // tn
    tab = pl.BlockSpec((tn, 256), lambda i: (i % nper, 0))
    row = lambda wd: pl.BlockSpec((tn, wd), lambda i: (i, 0))
    return pl.pallas_call(
        _nsa_proj_kernel,
        out_shape=(jax.ShapeDtypeStruct((n, 1024), qdtype),
                   jax.ShapeDtypeStruct((n, 768), F32),
                   jax.ShapeDtypeStruct((n, 768), BF16),
                   jax.ShapeDtypeStruct((n, 128), F32),
                   jax.ShapeDtypeStruct((n // period, 768, period), F32)),
        grid=(n // tn,),
        in_specs=[row(d), _full((1, d)), _full(w.shape), _full((256, 256)), _full((8, 256)), tab, tab],
        out_specs=(row(1024), row(768), row(768), row(128), _t_spec(768, tn, nper)),
        compiler_params=_cparams("parallel"),
    )(x, g, w, gm, consts, jnp.asarray(cos), jnp.asarray(sin))


def _nsa_cmp_weights(cmp_pe, cmp_w1, cmp_w2):
    w1h, pe_rows, w2blk = [], [], []
    for i in range(2):
        w1h.append(jnp.concatenate([cmp_w1[i][:1024], cmp_w1[i][1024:]], axis=1).astype(BF16))
        pe_rows.append(jnp.concatenate([cmp_pe[i].reshape(2, 1024), jnp.zeros((6, 1024), F32)], axis=0))
        z2 = jnp.zeros_like(cmp_w2[i])
        w2blk.append(jnp.concatenate([jnp.concatenate([cmp_w2[i], z2], axis=1),
                                      jnp.concatenate([z2, cmp_w2[i]], axis=1)], axis=0).astype(BF16))
    return w1h, pe_rows, w2blk


def _cmp_chunks(row_ref, m):
    lane = lax.broadcasted_iota(jnp.int32, (m, LANES), 1)
    g0, g1 = [], []
    for a in range(NSA_CMP_STRIDE // 2):
        xe = row_ref[pl.ds(2 * a, m, stride=NSA_CMP_STRIDE), :]
        xo = row_ref[pl.ds(2 * a + 1, m, stride=NSA_CMP_STRIDE), :]
        g0.append(jnp.where(lane < HEAD_DIM, xe, pltpu.roll(xo, HEAD_DIM, 1)))
        g1.append(jnp.where(lane < HEAD_DIM, pltpu.roll(xe, HEAD_DIM, 1), xo))
    return jnp.concatenate([jnp.concatenate(g0, axis=1), jnp.concatenate(g1, axis=1)], axis=0).astype(BF16)


def _cmp_tokens(pre_a, pre_b, pe_ref, w_ref, b1, w2_ref, m):
    pc = _dot(pe_ref[...].astype(BF16), w_ref[...])
    c = pc[0:1, 0:256] + pc[1:2, 256:512] + b1
    hid = _gelu_tanh(pre_a[:, 0:256] + pre_b[:, 256:512] + c)
    hid = jnp.concatenate([hid[0:m], hid[m:2 * m]], axis=1).astype(BF16)
    return _dot(hid, w2_ref[...])


def _nsa_cmp_prompt_kernel(k_ref, v_ref, wk_ref, wv_ref, pek_ref, pev_ref, b1_ref, w2k_ref, w2v_ref, o_ref):
    nch = o_ref.shape[0]
    for i, (r_ref, w_ref, pe_ref, w2_ref) in enumerate(((k_ref, wk_ref, pek_ref, w2k_ref),
                                                         (v_ref, wv_ref, pev_ref, w2v_ref))):
        pre = _dot(_cmp_chunks(r_ref, nch), w_ref[...])
        nxt = pltpu.roll(pre, 2 * nch - 1, 0)
        o_ref[:, i * 128:(i + 1) * 128] = _cmp_tokens(pre, nxt, pe_ref, w_ref, b1_ref[i:i + 1, :], w2_ref, nch)


def _nsa_cmp_prompt(kv, wbig, pe_rows, b1, w2blk, batch, seq):
    nch = seq // NSA_CMP_STRIDE
    return pl.pallas_call(
        _nsa_cmp_prompt_kernel,
        out_shape=jax.ShapeDtypeStruct((batch * nch, 256), F32),
        grid=(batch,),
        in_specs=[pl.BlockSpec((seq, 128), lambda b: (b, 0)), pl.BlockSpec((seq, 128), lambda b: (b, 1)),
                  _full((1024, 512)), _full((1024, 512)), _full((8, 1024)), _full((8, 1024)),
                  _full((8, 256)), _full((512, 128)), _full((512, 128))],
        out_specs=pl.BlockSpec((nch, 256), lambda b: (b, 0)),
        compiler_params=_cparams("parallel"),
    )(kv, kv, wbig[0], wbig[1], pe_rows[0], pe_rows[1], b1, w2blk[0], w2blk[1])


def _flash_init(m_ref, l_ref, acc_ref):
    m_ref[...] = jnp.full(m_ref.shape, NEG, F32)
    l_ref[...] = jnp.zeros(l_ref.shape, F32)
    acc_ref[...] = jnp.zeros(acc_ref.shape, F32)


def _nsa_prompt_kernel(q_ref, ks_ref, vs_ref, kw_ref, vw_ref, kvc_ref, gate_ref, ovl_ref, e_ref,
                       o_ref, m_ref, acc_ref, os_ref):
    i = pl.program_id(1)
    tq = q_ref.shape[0]
    rows = 16 * tq
    tk = 256
    q0 = i * tq
    qa = _slot_rows(q_ref, 8)
    ntok = kvc_ref.shape[0]

    kc = kvc_ref[:, 0:128].astype(BF16)
    vc = kvc_ref[:, 128:256].astype(BF16)
    qpos_c = q0 + (lax.broadcasted_iota(jnp.int32, (rows, ntok), 0) & (tq - 1))
    tok = lax.broadcasted_iota(jnp.int32, (rows, ntok), 1)
    ok = tok * NSA_CMP_STRIDE + (NSA_CMP_LEN - 1) <= qpos_c
    sc = jnp.where(ok, _dot_nt(qa, kc), NEG)
    mc = jnp.max(sc, axis=-1, keepdims=True)
    ec = jnp.where(ok, jnp.exp2(sc - mc), 0.0)
    pc = ec / jnp.maximum(jnp.sum(ec, axis=-1, keepdims=True), TINY)
    o_c = _dot(pc.astype(BF16), vc)

    pg = jnp.sum(pc.reshape(2, 8, tq, ntok), axis=1).reshape(2 * tq, ntok)
    phi, plo = _split_bf16(pg)
    nsb = ks_ref.shape[0] // NSA_SLC_BLOCK
    imp_t = (_dot_nt(ovl_ref[...], phi) + _dot_nt(ovl_ref[...], plo))[0:nsb, :]
    qp2 = q0 + (lax.broadcasted_iota(jnp.int32, (nsb, 2 * tq), 1) & (tq - 1))
    own = qp2 >> 6
    jb = lax.broadcasted_iota(jnp.int32, (nsb, 2 * tq), 0)
    allowed = jb <= own
    forced = (jb == 0) | (jb == own) | (jb == own - 1)
    imp_t = jnp.where(forced, NSA_FORCE_SCORE, imp_t)
    imp_t = jnp.where(allowed, imp_t, -jnp.inf)
    sel_t = _topk_mask_t(imp_t, NSA_SLC_TOPN, nsb) & allowed
    bias = _bias_rows(sel_t).astype(BF16)
    qaug = jnp.concatenate([qa, jnp.concatenate([bias[0:tq]] * 8 + [bias[tq:2 * tq]] * 8, axis=0)], axis=1)

    qp_r = q0 + (lax.broadcasted_iota(jnp.int32, (rows, tk), 0) & (tq - 1))
    klr = lax.broadcasted_iota(jnp.int32, (rows, tk), 1)
    jd = (q0 + tq - 1) // tk
    offd = pl.multiple_of(jd * tk, tk)

    def slc_scores(off):
        kj = jnp.concatenate([ks_ref[pl.ds(off, tk), :], e_ref[pl.ds(off, tk), :]], axis=1)
        return _dot_nt(qaug, kj)

    _flash_first(jnp.where(offd + klr <= qp_r, slc_scores(offd), NEG),
                 lambda p: _dot(p, _with_ones(vs_ref[pl.ds(offd, tk), :])), m_ref, None, acc_ref)

    def slc_body(j, carry):
        off = pl.multiple_of(j * tk, tk)
        _flash_next(slc_scores(off), lambda p: _dot(p, _with_ones(vs_ref[pl.ds(off, tk), :])),
                    m_ref, None, acc_ref)
        return carry

    lax.fori_loop(0, jd, slc_body, 0)
    os_ref[...] = _finish(acc_ref)

    def win_scores(off):
        dist = qp_r - (off + klr)
        return jnp.where((dist >= 0) & (dist < NSA_WINDOW), _dot_nt(qa, kw_ref[pl.ds(off, tk), :]), NEG)

    _flash_first(win_scores(offd), lambda p: _dot(p, _with_ones(vw_ref[pl.ds(offd, tk), :])),
                 m_ref, None, acc_ref)

    def win_body(j, carry):
        off = pl.multiple_of(j * tk, tk)
        _flash_next(win_scores(off), lambda p: _dot(p, _with_ones(vw_ref[pl.ds(off, tk), :])),
                    m_ref, None, acc_ref)
        return carry

    lax.fori_loop(jnp.maximum(q0 - (NSA_WINDOW - 1), 0) // tk, jd, win_body, 0)
    o_w = _finish(acc_ref)
    o_s = os_ref[...]

    gate = gate_ref[...]
    outs = []
    for h in range(16):
        r0 = h * tq
        outs.append(gate[:, 3 * h:3 * h + 1] * o_c[r0:r0 + tq, :]
                    + gate[:, 3 * h + 1:3 * h + 2] * o_s[r0:r0 + tq, :]
                    + gate[:, 3 * h + 2:3 * h + 3] * o_w[r0:r0 + tq, :])
    o_ref[...] = _slot_out(jnp.concatenate(outs, axis=0), 8, tq).astype(o_ref.dtype)


def _nsa_overlap(n_tok, n_blk, tok_shift, rows, cols):
    ovl = np.zeros((rows, cols), np.float32)
    t = np.arange(n_tok)[:, None] * NSA_CMP_STRIDE
    b = np.arange(n_blk)[None, :] * NSA_SLC_BLOCK
    ovl[tok_shift:tok_shift + n_tok, :n_blk] = ((t < b + NSA_SLC_BLOCK) & (t + NSA_CMP_LEN > b))
    return ovl


def _nsa_prompt_attn(q, kvb, kvc, gate, batch, seq):
    tq, tk = 128, 256
    nq = seq // tq
    nch = seq // NSA_CMP_STRIDE
    n_tok = (seq - NSA_CMP_LEN) // NSA_CMP_STRIDE + 1
    nsb = seq // NSA_SLC_BLOCK
    assert nch == LANES and nsb <= LANES
    ovl = jnp.asarray(_nsa_overlap(n_tok, nsb, 0, nch, LANES).T, BF16)
    e = np.zeros((seq, LANES), np.float32)
    e[np.arange(seq), np.arange(seq) // NSA_SLC_BLOCK] = 1.0
    e = jnp.asarray(e, BF16)
    seqcol = lambda c: pl.BlockSpec((seq, 128), lambda b, i: (b, c))
    return pl.pallas_call(
        _nsa_prompt_kernel,
        out_shape=jax.ShapeDtypeStruct((batch * seq, 1024), BF16),
        grid=(batch, nq),
        in_specs=[pl.BlockSpec((tq, 1024), lambda b, i: (b * nq + i, 0)),
                  seqcol(2), seqcol(3), seqcol(4), seqcol(5),
                  pl.BlockSpec((nch, 256), lambda b, i: (b, 0)),
                  pl.BlockSpec((tq, 128), lambda b, i: (b * nq + i, 0)),
                  _full((nch, LANES)), _full(e.shape)],
        out_specs=pl.BlockSpec((tq, 1024), lambda b, i: (b * nq + i, 0)),
        scratch_shapes=[pltpu.VMEM((16 * tq, 128), F32), pltpu.VMEM((16 * tq, 256), F32),
                        pltpu.VMEM((16 * tq, 128), F32)],
        compiler_params=_cparams("parallel", "arbitrary"),
    )(q, kvb, kvb, kvb, kvb, kvc, gate, ovl, e)


def _topk_mask_iter(score, k):
    rows, w = score.shape
    lane = lax.broadcasted_iota(jnp.int32, (rows, w), 1)
    taken = jnp.zeros((rows, w), jnp.int32)
    for _ in range(k):
        free = taken == 0
        cur = jnp.where(free, score, -jnp.inf)
        m = jnp.max(cur, axis=-1, keepdims=True)
        idx = jnp.min(jnp.where(free & (cur == m), lane, w), axis=-1, keepdims=True)
        taken = jnp.where(lane == idx, 1, taken)
    return taken > 0


def _sample_q_rows(q):
    lane = lax.broadcasted_iota(jnp.int32, (8, LANES), 1)
    ev = [jnp.where(lane < HEAD_DIM, q[:, s * LANES:(s + 1) * LANES], 0.0) for s in range(8)]
    od = [jnp.where(lane >= HEAD_DIM, q[:, s * LANES:(s + 1) * LANES], 0.0) for s in range(8)]
    return jnp.concatenate(ev + od, axis=0)


def _nsa_sample_cmp_kernel(pt_ref, q_ref, wk_ref, wv_ref, pek_ref, pev_ref, b1_ref, w2k_ref, w2v_ref,
                           ovl_ref, *refs, pps, past, blk_per_step):
    pages = refs[:pps]
    oc_ref, sel_ref = refs[pps], refs[pps + 1]
    kvc_ref, carry_ref, stage_ref = refs[pps + 2:]
    s = pl.program_id(1)
    m = pps * (PAGE // NSA_CMP_STRIDE)

    @pl.when(s == 0)
    def _():
        carry_ref[...] = jnp.zeros_like(carry_ref)

    row = lax.broadcasted_iota(jnp.int32, (2 * m, 512), 0)
    for i, (w_ref, pe_ref, w2_ref) in enumerate(((wk_ref, pek_ref, w2k_ref), (wv_ref, pev_ref, w2v_ref))):
        for t, pg in enumerate(pages):
            stage_ref[t * PAGE:(t + 1) * PAGE, :] = pg[i * 128:(i + 1) * 128, :].T
        pre = _dot(_cmp_chunks(stage_ref, m), w_ref[...])
        prev = pltpu.roll(pre, 1, 0)
        prev = jnp.where(row == 0, carry_ref[2 * i:2 * i + 1, :], prev)
        prev = jnp.where(row == m, carry_ref[2 * i + 1:2 * i + 2, :], prev)
        carry_ref[2 * i:2 * i + 1, :] = pre[m - 1:m, :]
        carry_ref[2 * i + 1:2 * i + 2, :] = pre[2 * m - 1:2 * m, :]
        kvc_ref[pl.ds(pl.multiple_of(s * m, m), m), i * 128:(i + 1) * 128] = _cmp_tokens(
            prev, pre, pe_ref, w_ref, b1_ref[i:i + 1, :], w2_ref, m)

    @pl.when(s == pl.num_programs(1) - 1)
    def _():
        nt = kvc_ref.shape[0]
        qa = _sample_q_rows(q_ref[...]).astype(BF16)
        kc = kvc_ref[:, 0:128].astype(BF16)
        vc = kvc_ref[:, 128:256].astype(BF16)
        r = lax.broadcasted_iota(jnp.int32, (128, nt), 1)
        qpos = past + (lax.broadcasted_iota(jnp.int32, (128, nt), 0) & 7)
        ok = (r >= 1) & ((r - 1) * NSA_CMP_STRIDE + (NSA_CMP_LEN - 1) <= qpos)
        sc = jnp.where(ok, _dot_nt(qa, kc), NEG)
        mc = jnp.max(sc, axis=-1, keepdims=True)
        ec = jnp.where(ok, jnp.exp2(sc - mc), 0.0)
        pc = ec / jnp.maximum(jnp.sum(ec, axis=-1, keepdims=True), TINY)
        oc_ref[...] = _dot(pc.astype(BF16), vc)
        pg = jnp.sum(pc.reshape(2, 8, 8, nt), axis=1).reshape(16, nt)
        phi, plo = _split_bf16(pg)
        imp = _dot(phi, ovl_ref[...]) + _dot(plo, ovl_ref[...])
        nl = imp.shape[1]
        own = (past + (lax.broadcasted_iota(jnp.int32, (16, nl), 0) & 7)) >> 6
        jb = lax.broadcasted_iota(jnp.int32, (16, nl), 1)
        allowed = jb <= own
        forced = (jb == 0) | (jb == own) | (jb == own - 1)
        imp = jnp.where(forced, NSA_FORCE_SCORE, imp)
        imp = jnp.where(allowed, imp, -jnp.inf)
        sel = jnp.where(_topk_mask_iter(imp, NSA_SLC_TOPN) & allowed, 1.0, 0.0)
        lane = lax.broadcasted_iota(jnp.int32, (16, LANES), 1)
        for st in range(sel_ref.shape[0]):
            piece = sel if st == 0 else pltpu.roll(sel, nl - st * blk_per_step, 1)
            sel_ref[st] = jnp.where(lane < blk_per_step, piece[:, 0:LANES], 0.0)


def _nsa_sample_cmp(q, cache, page_table, w1h, pe_rows, b1, w2blk, past, pps, sel_pps):
    db, n_pages = page_table.shape
    nsteps = n_pages // pps
    nsel = n_pages // sel_pps
    nt = n_pages * (PAGE // NSA_CMP_STRIDE)
    n_tok = (past + 8 - NSA_CMP_LEN) // NSA_CMP_STRIDE + 1
    nsb = -(-(past + 8) // NSA_SLC_BLOCK)
    nl = -(-nsb // LANES) * LANES
    blk_per_step = sel_pps * PAGE // NSA_SLC_BLOCK
    assert n_tok == nt - 1 and blk_per_step <= LANES
    ovl = jnp.asarray(_nsa_overlap(n_tok, nsb, 1, nt, nl), BF16)

    def page_spec(t):
        return pl.BlockSpec((None, 256, PAGE), lambda b, s, pt: (pt[b, s * pps + t], 0, 0))

    grid_spec = pltpu.PrefetchScalarGridSpec(
        num_scalar_prefetch=1,
        grid=(db, nsteps),
        in_specs=[pl.BlockSpec((8, 1024), lambda b, s, pt: (b, 0)),
                  _full((1024, 512)), _full((1024, 512)), _full((8, 1024)), _full((8, 1024)),
                  _full((8, 256)), _full((512, 128)), _full((512, 128)), _full((nt, nl))]
                 + [page_spec(t) for t in range(pps)],
        out_specs=(pl.BlockSpec((128, 128), lambda b, s, pt: (b, 0)),
                   pl.BlockSpec((None, nsel, 16, LANES), lambda b, s, pt: (b, 0, 0, 0))),
        scratch_shapes=[pltpu.VMEM((nt, 256), F32), pltpu.VMEM((8, 512), F32),
                        pltpu.VMEM((pps * PAGE, 128), F32)],
    )
    return pl.pallas_call(
        functools.partial(_nsa_sample_cmp_kernel, pps=pps, past=past, blk_per_step=blk_per_step),
        out_shape=(jax.ShapeDtypeStruct((db * 128, 128), F32),
                   jax.ShapeDtypeStruct((db, nsel, 16, LANES), F32)),
        grid_spec=grid_spec,
        compiler_params=_cparams("parallel", "arbitrary"),
    )(page_table, q, w1h[0], w1h[1], pe_rows[0], pe_rows[1], b1, w2blk[0], w2blk[1], ovl,
      *([cache] * pps))


def _nsa_sample_attn_kernel(pt_ref, q_ref, sel_ref, oc_ref, kvn_ref, win_ref, gate_ref, e_ref, *refs,
                            pps, past):
    pages = refs[:pps]
    o_ref = refs[pps]
    qa_ref, m_ref, acc_ref = refs[pps + 1:]
    s = pl.program_id(1)
    rows = 128

    @pl.when(s == 0)
    def _():
        qa_ref[...] = _sample_q_rows(q_ref[...]).astype(BF16)
        m_ref[...] = jnp.full(m_ref.shape, NEG, F32)
        acc_ref[...] = jnp.zeros(acc_ref.shape, F32)

    qa = qa_ref[...]
    kt = jnp.concatenate([pg[0:128, :] for pg in pages], axis=1).astype(BF16)
    vt = jnp.concatenate([pg[128:256, :] for pg in pages], axis=1).astype(BF16)
    ones_t = jnp.ones(vt.shape, BF16)
    vt1 = jnp.concatenate([vt, ones_t], axis=0)
    sel = sel_ref[...]
    selrows = jnp.concatenate([sel[0:8, :]] * 8 + [sel[8:16, :]] * 8, axis=0)
    qaug = jnp.concatenate([qa, jnp.where(selrows > 0.5, 0.0, NEG).astype(BF16)], axis=1)
    _flash_next(_dot(qaug, jnp.concatenate([kt, e_ref[...]], axis=0)), lambda p: _dot_nt(p, vt1),
                m_ref, None, acc_ref)

    @pl.when(s == pl.num_programs(1) - 1)
    def _():
        col = lax.broadcasted_iota(jnp.int32, (rows, LANES), 1)
        qi = lax.broadcasted_iota(jnp.int32, (rows, LANES), 0) & 7
        kn = kvn_ref[256:384, :].astype(BF16)
        vn = jnp.concatenate([kvn_ref[384:512, :].astype(BF16), ones_t[:, 0:LANES]], axis=0)
        _flash_masked(_dot(qa, kn), col <= qi, lambda p: _dot_nt(p, vn), m_ref, None, acc_ref)
        o_s = _finish(acc_ref)
        wb = win_ref.shape[1]
        kw = jnp.concatenate([win_ref[0:128, :], kvn_ref[512:640, :]], axis=1).astype(BF16)
        vw = jnp.concatenate([win_ref[128:256, :], kvn_ref[640:768, :]], axis=1).astype(BF16)
        nw = wb + LANES
        c = lax.broadcasted_iota(jnp.int32, (rows, nw), 1)
        qpos = past + (lax.broadcasted_iota(jnp.int32, (rows, nw), 0) & 7)
        wpos = past - wb + c
        dist = qpos - wpos
        okw = (dist >= 0) & (dist < NSA_WINDOW) & (wpos >= 0) & (c < wb + 8)
        sw = jnp.where(okw, _dot(qa, kw), NEG)
        mw = jnp.max(sw, axis=-1, keepdims=True)
        pw = jnp.where(okw, jnp.exp2(sw - mw), 0.0)
        o_w = _dot_nt(pw.astype(BF16), vw) / jnp.maximum(jnp.sum(pw, axis=-1, keepdims=True), TINY)
        o_c = oc_ref[...]
        gate = gate_ref[...]
        lane8 = lax.broadcasted_iota(jnp.int32, (8, LANES), 1)
        hs = []
        for h in range(16):
            r0 = h * 8
            hs.append(gate[:, 3 * h:3 * h + 1] * o_c[r0:r0 + 8, :]
                      + gate[:, 3 * h + 1:3 * h + 2] * o_s[r0:r0 + 8, :]
                      + gate[:, 3 * h + 2:3 * h + 3] * o_w[r0:r0 + 8, :])
        o_ref[...] = jnp.concatenate([jnp.where(lane8 < HEAD_DIM, hs[sl], hs[8 + sl]) for sl in range(8)],
                                     axis=1)


def _nsa_sample_attn(q, sel, o_c, kvn, state_win, gate, cache, page_table, past, pps):
    db, n_pages = page_table.shape
    nsteps = n_pages // pps
    wb = state_win.shape[2]
    nk = pps * PAGE
    e = np.zeros((LANES, nk), np.float32)
    e[np.arange(nk) // NSA_SLC_BLOCK, np.arange(nk)] = 1.0

    def page_spec(t):
        return pl.BlockSpec((None, 256, PAGE), lambda b, s, pt: (pt[b, s * pps + t], 1, 0))

    grid_spec = pltpu.PrefetchScalarGridSpec(
        num_scalar_prefetch=1,
        grid=(db, nsteps),
        in_specs=[pl.BlockSpec((8, 1024), lambda b, s, pt: (b, 0)),
                  pl.BlockSpec((None, None, 16, LANES), lambda b, s, pt: (b, s, 0, 0)),
                  pl.BlockSpec((128, 128), lambda b, s, pt: (b, 0)),
                  pl.BlockSpec((None, 768, LANES), lambda b, s, pt: (b, 0, 0)),
                  pl.BlockSpec((None, 256, wb), lambda b, s, pt: (b, 0, 0)),
                  pl.BlockSpec((8, 128), lambda b, s, pt: (b, 0)),
                  _full((LANES, nk))]
                 + [page_spec(t) for t in range(pps)],
        out_specs=pl.BlockSpec((8, 1024), lambda b, s, pt: (b, 0)),
        scratch_shapes=[pltpu.VMEM((128, 128), BF16), pltpu.VMEM((128, 128), F32), pltpu.VMEM((128, 256), F32)],
    )
    return pl.pallas_call(
        functools.partial(_nsa_sample_attn_kernel, pps=pps, past=past),
        out_shape=jax.ShapeDtypeStruct((db * 8, 1024), F32),
        grid_spec=grid_spec,
        compiler_params=_cparams("parallel", "arbitrary"),
    )(page_table, q, sel, o_c, kvn, state_win, gate, jnp.asarray(e, BF16), *([cache] * pps))


def _nsa_layer(xp, xs, cache, state_win, page_table, ln_g, w_in, qn, kn, cmp_pe, cmp_w1, cmp_b1, cmp_w2,
               w_o, batch, seq, past_len):
    perm = _slot_perm(NSA_KV_HEADS)
    w_in_p = jnp.concatenate([w_in[:, :1024][:, perm], w_in[:, 1024:],
                              jnp.zeros((w_in.shape[0], 1920 - w_in.shape[1]), F32)], axis=1).astype(BF16)
    w_o_p = w_o[perm, :].astype(BF16)
    g = ln_g.reshape(1, -1)
    wbig, pe_rows, w2blk = _nsa_cmp_weights(cmp_pe, cmp_w1, cmp_w2)
    b1 = jnp.concatenate([cmp_b1, jnp.zeros((6, NSA_CMP_HIDDEN), F32)], axis=0)
    dec = 8
    q, kv, kvb, gate, kvt = _nsa_proj(xp, g, w_in_p, qn, kn, np.arange(seq), 512, BF16)
    kvc = _nsa_cmp_prompt(kv, wbig, pe_rows, b1, w2blk, batch, seq)
    o = _nsa_prompt_attn(q, kvb, kvc, gate, batch, seq)
    ns = xs.shape[0]
    db, n_pages = page_table.shape
    pps = min(16, n_pages)
    pos_s = np.tile(past_len + np.arange(dec), ns // dec)
    qs, kvs, _, gs, _ = _nsa_proj(xs, g, w_in_p, qn, kn, pos_s, ns, F32)
    cache_t = _pages_t(cache)
    win_t = jnp.swapaxes(state_win.reshape(db, state_win.shape[1], 256), 1, 2)
    o_c, sel = _nsa_sample_cmp(qs, cache_t, page_table, wbig, pe_rows, b1, w2blk, past_len,
                               min(32, n_pages), pps)
    os_ = _nsa_sample_attn(qs, sel, o_c, _rows_t(kvs, db), win_t, gs, cache_t, page_table, past_len, pps)
    return o, os_, w_o_p, kvt, kvs


MLA_QK = MLA_NOPE + MLA_ROPE


def _mla_proj_kernel(x_ref, g_ref, w_ref, wuq_ref, gm_ref, c_ref, cosq_ref, sinq_ref, cosk_ref, sink_ref,
                     q_ref, lat_ref, latt_ref):
    h = _rms_rows(x_ref[...], g_ref[...]).astype(BF16)
    cq = _rms_rows(_dot(h, w_ref[:, 0:MLA_Q_LORA]), c_ref[0:1, 0:MLA_Q_LORA]).astype(BF16)
    ckv = _rms_rows(_dot(h, w_ref[:, MLA_Q_LORA:MLA_Q_LORA + MLA_KV_LORA]), c_ref[1:2, 0:MLA_KV_LORA])
    a = _dot(h, w_ref[:, 640:768])
    ms = jnp.sum(a * a, axis=-1, keepdims=True) * (1.0 / MLA_ROPE)
    kpe = a * lax.rsqrt(ms + NORM_EPS) * c_ref[2:3, 0:128]
    kpe = _rope(kpe, cosk_ref[...], sink_ref[...], c_ref[3:4, 0:128], MLA_ROPE // 2)
    lat_ref[:, 0:MLA_KV_LORA] = ckv
    lat_ref[:, MLA_KV_LORA:MLA_KV_LORA + MLA_ROPE] = kpe[:, 0:MLA_ROPE]
    latt_ref[0:MLA_KV_LORA, :] = ckv.T
    latt_ref[MLA_KV_LORA:MLA_KV_LORA + MLA_ROPE, :] = kpe.T[0:MLA_ROPE, :]
    gm = gm_ref[...]
    qg, up = c_ref[4:5, 0:256], c_ref[5:6, 0:256]
    cos, sin = cosq_ref[...], sinq_ref[...]
    for c in range(8):
        a = _dot(cq, wuq_ref[:, c * 256:(c + 1) * 256])
        y = _rope(_group_norm(a, gm, qg), cos, sin, up, MLA_ROPE // 2)
        q_ref[:, c * 256:(c + 1) * 256] = (y * (MLA_QK ** -0.5 * LOG2E)).astype(q_ref.dtype)


def _mla_cat_groups(width):
    g = []
    for lo in range(0, width, LANES):
        g += [(lo, lo + MLA_NOPE), (lo + MLA_NOPE, lo + MLA_QK)]
    return g


def _mla_cat_cols():
    idx = -np.ones((16, LANES), np.int64)
    for h in range(16):
        idx[h, :MLA_QK] = h * MLA_QK + np.arange(MLA_QK)
    return idx.reshape(-1)


def _take_cols(w, idx):
    wz = jnp.concatenate([w, jnp.zeros((w.shape[0], 1), w.dtype)], axis=1)
    return wz[:, np.where(idx < 0, w.shape[1], idx)]


def _mla_proj(x, g, w_dqkv, g_q, g_kv, w_uq, qn, kn, pos, tn, qdtype):
    n, d = x.shape
    period = pos.shape[0]
    w = jnp.concatenate([w_dqkv, jnp.zeros((d, 768 - w_dqkv.shape[1]), F32)], axis=1).astype(BF16)
    wuq = _take_cols(w_uq, _mla_cat_cols()).astype(BF16)
    cosq, sinq, upq = _rope_tables(pos, 256, [(MLA_NOPE, MLA_ROPE), (LANES + MLA_NOPE, MLA_ROPE)])
    cosk, sink, upk = _rope_tables(pos, 128, [(0, MLA_ROPE)])
    consts = jnp.zeros((8, 384), F32)
    consts = consts.at[0, :].set(g_q).at[1, 0:256].set(g_kv).at[2, 0:MLA_ROPE].set(kn[MLA_NOPE:])
    consts = consts.at[3, 0:128].set(jnp.asarray(upk[0])).at[5, 0:256].set(jnp.asarray(upq[0]))
    qgain = jnp.concatenate([qn, jnp.zeros((LANES - MLA_QK,), F32)])
    consts = consts.at[4, 0:256].set(jnp.tile(qgain, 2))
    gm = jnp.asarray(_group_mean_matrix(_mla_cat_groups(256), 256), BF16)
    nper = period // tn
    tabq = pl.BlockSpec((tn, 256), lambda i: (i % nper, 0))
    tabk = pl.BlockSpec((tn, 128), lambda i: (i % nper, 0))
    row = lambda wd: pl.BlockSpec((tn, wd), lambda i: (i, 0))
    return pl.pallas_call(
        _mla_proj_kernel,
        out_shape=(jax.ShapeDtypeStruct((n, 2048), qdtype), jax.ShapeDtypeStruct((n, 288), F32),
                   jax.ShapeDtypeStruct((n // period, 288, period), F32)),
        grid=(n // tn,),
        in_specs=[row(d), _full((1, d)), _full(w.shape), _full(wuq.shape), _full((256, 256)), _full((8, 384)),
                  tabq, tabq, tabk, tabk],
        out_specs=(row(2048), row(288), _t_spec(288, tn, nper)),
        compiler_params=_cparams("parallel"),
    )(x, g, w, wuq, gm, consts, jnp.asarray(cosq), jnp.asarray(sinq), jnp.asarray(cosk), jnp.asarray(sink))


def _mla_expand_kernel(lat_ref, wk_ref, wv_ref, gm_ref, c_ref, place_ref, k_ref, v_ref):
    ckv = lat_ref[:, 0:MLA_KV_LORA].astype(BF16)
    kpe = _dot(lat_ref[:, MLA_KV_LORA:MLA_KV_LORA + MLA_ROPE].astype(BF16), place_ref[...])
    gm = gm_ref[...]
    kg = c_ref[0:1, :]
    for c in range(8):
        e = _dot(ckv, wk_ref[:, c * 256:(c + 1) * 256])
        k_ref[:, c * 256:(c + 1) * 256] = (_group_norm(e, gm, kg) + kpe).astype(BF16)
    v_ref[...] = _dot(ckv, wv_ref[...]).astype(BF16)


def _mla_split_ukv(w_ukv):
    w = w_ukv.reshape(MLA_KV_LORA, 16, MLA_NOPE + MLA_V)
    return w[:, :, :MLA_NOPE].reshape(MLA_KV_LORA, 16 * MLA_NOPE), w[:, :, MLA_NOPE:].reshape(MLA_KV_LORA, 16 * MLA_V)


def _mla_expand(lat, w_ukv, kn, tn):
    n = lat.shape[0]
    wk_nat, wv = _mla_split_ukv(w_ukv)
    idx = -np.ones((16, LANES), np.int64)
    for h in range(16):
        idx[h, :MLA_NOPE] = h * MLA_NOPE + np.arange(MLA_NOPE)
    wk = _take_cols(wk_nat, idx.reshape(-1)).astype(BF16)
    gm = jnp.asarray(_group_mean_matrix([(lo, lo + MLA_NOPE) for lo in (0, LANES)], 256), BF16)
    kgain = jnp.concatenate([kn[:MLA_NOPE], jnp.zeros((LANES - MLA_NOPE,), F32)])
    consts = jnp.zeros((8, 256), F32).at[0].set(jnp.tile(kgain, 2))
    place = np.zeros((MLA_ROPE, 256), np.float32)
    for lo in (MLA_NOPE, LANES + MLA_NOPE):
        place[np.arange(MLA_ROPE), lo + np.arange(MLA_ROPE)] = 1.0
    row = lambda wd: pl.BlockSpec((tn, wd), lambda i: (i, 0))
    return pl.pallas_call(
        _mla_expand_kernel,
        out_shape=(jax.ShapeDtypeStruct((n, 2048), BF16), jax.ShapeDtypeStruct((n, 1024), BF16)),
        grid=(n // tn,),
        in_specs=[row(288), _full(wk.shape), _full((256, 1024)), _full((256, 256)), _full((8, 256)),
                  _full((MLA_ROPE, 256))],
        out_specs=(row(2048), row(1024)),
        compiler_params=_cparams("parallel"),
    )(lat, wk, wv.astype(BF16), gm, consts, jnp.asarray(place, BF16))


def _mla_prompt_kernel(q_ref, k_ref, v_ref, o_ref, m_ref, acc_ref):
    i = pl.program_id(2)
    tq = q_ref.shape[0]
    tk = tq
    q0 = q_ref[:, 0:128]
    q1 = q_ref[:, 128:256]

    def scores(off):
        return jnp.concatenate([_dot_nt(q0, k_ref[pl.ds(off, tk), 0:128]),
                                _dot_nt(q1, k_ref[pl.ds(off, tk), 128:256])], axis=0)

    offd = pl.multiple_of(i * tk, tk)
    qi = lax.broadcasted_iota(jnp.int32, (2 * tq, tk), 0) & (tq - 1)
    ki = lax.broadcasted_iota(jnp.int32, (2 * tq, tk), 1)
    _flash_first(jnp.where(ki <= qi, scores(offd), NEG),
                 lambda p: _dot(p, _with_ones(v_ref[pl.ds(offd, tk), :])), m_ref, None, acc_ref)

    def body(j, carry):
        off = pl.multiple_of(j * tk, tk)
        _flash_next(scores(off), lambda p: _dot(p, _with_ones(v_ref[pl.ds(off, tk), :])), m_ref, None, acc_ref)
        return carry

    lax.fori_loop(0, i, body, 0)
    o = _finish(acc_ref)
    lane = lax.broadcasted_iota(jnp.int32, (tq, LANES), 1)
    o_ref[...] = jnp.where(lane < MLA_V, o[0:tq, :], o[tq:2 * tq, :]).astype(o_ref.dtype)


def _mla_prompt_attn(q, k, v, batch, seq):
    tq = 512
    nq = seq // tq
    return pl.pallas_call(
        _mla_prompt_kernel,
        out_shape=jax.ShapeDtypeStruct((batch * seq, 1024), BF16),
        grid=(batch, 8, nq),
        in_specs=[pl.BlockSpec((tq, 256), lambda b, p, i: (b * nq + i, p)),
                  pl.BlockSpec((seq, 256), lambda b, p, i: (b, p)),
                  pl.BlockSpec((seq, 128), lambda b, p, i: (b, p))],
        out_specs=pl.BlockSpec((tq, 128), lambda b, p, i: (b * nq + i, p)),
        scratch_shapes=[pltpu.VMEM((2 * tq, 128), F32), pltpu.VMEM((2 * tq, 256), F32)],
        compiler_params=_cparams("parallel", "parallel", "arbitrary"),
    )(q, k, v)


def _mla_sample_kernel(pt_ref, q_ref, latn_ref, wk_ref, wkt_ref, wv_ref, kg_ref, *refs, pps):
    pages = refs[:pps]
    o_ref = refs[pps]
    qt_ref, qpe_ref, m_ref, l_ref, acc_ref = refs[pps + 1:]
    s = pl.program_id(1)
    lane8 = lax.broadcasted_iota(jnp.int32, (8, LANES), 1)

    @pl.when(s == 0)
    def _():
        z = jnp.zeros((8, LANES), F32)
        qg_rows, pe_rows = [], []
        for h in range(16):
            ch = q_ref[:, h * LANES:(h + 1) * LANES]
            nope = jnp.where(lane8 < MLA_NOPE, ch, 0.0)
            if h % 2:
                nope = pltpu.roll(nope, MLA_NOPE, 1)
            qg_rows.append(jnp.concatenate([z] * (h // 2) + [nope] + [z] * (7 - h // 2), axis=1))
            pe_rows.append(pltpu.roll(ch, LANES - MLA_NOPE, 1)[:, 0:MLA_ROPE])
        qg = (jnp.concatenate(qg_rows, axis=0) * kg_ref[...]).astype(BF16)
        qt_ref[...] = _dot_nt(qg, wk_ref[...]).astype(BF16)
        qpe_ref[...] = jnp.concatenate(pe_rows, axis=0).astype(BF16)
        _flash_init(m_ref, l_ref, acc_ref)

    def update(lat_t, ok):
        ckv = lat_t[0:MLA_KV_LORA, :].astype(BF16)
        kpe = lat_t[MLA_KV_LORA:MLA_KV_LORA + MLA_ROPE, :].astype(BF16)
        n = lat_t.shape[1]
        e = _dot(wkt_ref[...], ckv)
        ms = jnp.sum((e * e).reshape(16, MLA_NOPE, n), axis=1) * (1.0 / MLA_NOPE)
        rs = lax.rsqrt(ms + NORM_EPS)
        rs = jnp.broadcast_to(rs[:, None, :], (16, 8, n)).reshape(LANES, n)
        sc = _dot(qt_ref[...], ckv) * rs + _dot(qpe_ref[...], kpe)
        pv = lambda p: _dot_nt(p, ckv)
        if ok is None:
            _flash_next(sc, pv, m_ref, l_ref, acc_ref)
        else:
            _flash_masked(sc, ok, pv, m_ref, l_ref, acc_ref)

    update(jnp.concatenate([pg[...] for pg in pages], axis=1), None)

    @pl.when(s == pl.num_programs(1) - 1)
    def _():
        rho = lax.broadcasted_iota(jnp.int32, (LANES, LANES), 0)
        t = lax.broadcasted_iota(jnp.int32, (LANES, LANES), 1)
        update(latn_ref[...], t <= (rho & 7))
        olat = (acc_ref[...] / _rep(l_ref[...], 2)).astype(BF16)
        ofull = _dot(olat, wv_ref[...])
        r_head = lax.broadcasted_iota(jnp.int32, (LANES, 1024), 0) >> 3
        c_head = lax.broadcasted_iota(jnp.int32, (LANES, 1024), 1) >> 6
        ofull = jnp.where(r_head == c_head, ofull, 0.0)
        out = ofull[0:8, :]
        for h in range(1, 16):
            out = out + ofull[h * 8:(h + 1) * 8, :]
        o_ref[...] = out


def _mla_sample_attn(q, latn_t, cache_t, page_table, w_ukv, kn, pps):
    db, n_pages = page_table.shape
    nsteps = n_pages // pps
    wk, wv = _mla_split_ukv(w_ukv)
    wk = wk.astype(BF16)
    kg = jnp.tile(kn[:MLA_NOPE], 16).reshape(1, 1024)

    def page_spec(t):
        return pl.BlockSpec((None, 288, PAGE), lambda b, s, pt: (pt[b, s * pps + t], 0, 0))

    grid_spec = pltpu.PrefetchScalarGridSpec(
        num_scalar_prefetch=1,
        grid=(db, nsteps),
        in_specs=[pl.BlockSpec((8, 2048), lambda b, s, pt: (b, 0)),
                  pl.BlockSpec((None, 288, LANES), lambda b, s, pt: (b, 0, 0)),
                  _full((256, 1024)), _full((1024, 256)), _full((256, 1024)), _full((1, 1024))]
                 + [page_spec(t) for t in range(pps)],
        out_specs=pl.BlockSpec((8, 1024), lambda b, s, pt: (b, 0)),
        scratch_shapes=[pltpu.VMEM((128, 256), BF16), pltpu.VMEM((128, MLA_ROPE), BF16),
                        pltpu.VMEM((128, 128), F32), pltpu.VMEM((128, 128), F32), pltpu.VMEM((128, 256), F32)],
    )
    return pl.pallas_call(
        functools.partial(_mla_sample_kernel, pps=pps),
        out_shape=jax.ShapeDtypeStruct((db * 8, 1024), F32),
        grid_spec=grid_spec,
        compiler_params=_cparams("parallel", "arbitrary"),
    )(page_table, q, latn_t, wk, wk.T, wv.astype(BF16), kg, *([cache_t] * pps))


def _mla_layer(xp, xs, cache, page_table, ln_g, w_dqkv, g_q, g_kv, w_uq, w_ukv, qn, kn, w_o,
               batch, seq, past_len):
    g = ln_g.reshape(1, -1)
    w_o_b = w_o.astype(BF16)
    dec = 8
    q, lat, latt = _mla_proj(xp, g, w_dqkv, g_q, g_kv, w_uq, qn, kn, np.arange(seq), 512, BF16)
    k, v = _mla_expand(lat, w_ukv, kn, 512)
    o = _mla_prompt_attn(q, k, v, batch, seq)
    ns = xs.shape[0]
    pos_s = np.tile(past_len + np.arange(dec), ns // dec)
    qs, lats, _ = _mla_proj(xs, g, w_dqkv, g_q, g_kv, w_uq, qn, kn, pos_s, ns, F32)
    os_ = _mla_sample_attn(qs, _rows_t(lats, page_table.shape[0]), _pages_t(cache), page_table, w_ukv, kn,
                           min(16, page_table.shape[1]))
    return o, os_, w_o_b, latt, lats


def kernel(x_prompt, x_sample, cache_kv_0, cache_kv_1, state_win_1, cache_lat_2, cache_kv_3, page_table, ln1_g, ln2_g, mlp_w1, mlp_w2, moba_w_in_0, moba_qn_0, moba_kn_0, moba_w_o_0, nsa_w_in_1, nsa_qn_1, nsa_kn_1, nsa_cmp_pe_1, nsa_cmp_w1_1, nsa_cmp_b1_1, nsa_cmp_w2_1, nsa_w_o_1, mla_w_dqkv_2, mla_g_q_2, mla_g_kv_2, mla_w_uq_2, mla_w_ukv_2, mla_qn_2, mla_kn_2, mla_w_o_2, moba_w_in_3, moba_qn_3, moba_kn_3, moba_w_o_3):
    batch, seq, d = x_prompt.shape
    db, dec, _ = x_sample.shape
    past = page_table.shape[1] * PAGE
    assert dec == 8 and seq % 512 == 0
    xp = x_prompt.reshape(batch * seq, d)
    xs = x_sample.reshape(db * dec, d)

    def finish_layer(i, xp, xs, op, os_, wo):
        w1 = mlp_w1[i].astype(BF16)
        w2 = mlp_w2[i].astype(BF16)
        g = ln2_g[i].reshape(1, d)
        return (_mixer_out_mlp(xp, op, wo, g, w1, w2, 512), _mixer_out_mlp(xs, os_, wo, g, w1, w2, db * dec))

    op, os_, wo, kv0_p, kv0_s = _moba_layer(xp, xs, cache_kv_0, page_table, ln1_g[0], moba_w_in_0, moba_qn_0,
                                            moba_kn_0, moba_w_o_0, batch, seq, past)
    xp, xs = finish_layer(0, xp, xs, op, os_, wo)
    op, os_, wo, kv1_p, kv1_s = _nsa_layer(xp, xs, cache_kv_1, state_win_1, page_table, ln1_g[1], nsa_w_in_1,
                                           nsa_qn_1, nsa_kn_1, nsa_cmp_pe_1, nsa_cmp_w1_1, nsa_cmp_b1_1,
                                           nsa_cmp_w2_1, nsa_w_o_1, batch, seq, past)
    xp, xs = finish_layer(1, xp, xs, op, os_, wo)
    op, os_, wo, lat_p, lat_s = _mla_layer(xp, xs, cache_lat_2, page_table, ln1_g[2], mla_w_dqkv_2, mla_g_q_2,
                                           mla_g_kv_2, mla_w_uq_2, mla_w_ukv_2, mla_qn_2, mla_kn_2, mla_w_o_2,
                                           batch, seq, past)
    xp, xs = finish_layer(2, xp, xs, op, os_, wo)
    op, os_, wo, kv3_p, kv3_s = _moba_layer(xp, xs, cache_kv_3, page_table, ln1_g[3], moba_w_in_3, moba_qn_3,
                                            moba_kn_3, moba_w_o_3, batch, seq, past)
    xp, xs = finish_layer(3, xp, xs, op, os_, wo)

    def rows_major(x_t, *feat):
        return jnp.moveaxis(x_t.reshape((batch,) + feat + (x_t.shape[-1],)), -1, 1)

    wb_p = min(NSA_WINDOW, seq)
    win_new = kv1_s[:, 512:768].reshape(db, dec, 2, NSA_KV_HEADS, HEAD_DIM)
    win_s = jnp.concatenate([state_win_1, win_new], axis=1)[:, dec:]
    return (xp.reshape(batch, seq, d), xs.reshape(db, dec, d),
            rows_major(kv0_p, 2, MOBA_KV_HEADS, HEAD_DIM),
            rows_major(kv1_p[:, :512], 4, NSA_KV_HEADS, HEAD_DIM),
            rows_major(kv1_p[:, 512:768, seq - wb_p:], 2, NSA_KV_HEADS, HEAD_DIM),
            rows_major(lat_p, MLA_KV_LORA + MLA_ROPE),
            rows_major(kv3_p, 2, MOBA_KV_HEADS, HEAD_DIM),
            kv0_s.reshape(db, dec, 2, MOBA_KV_HEADS, HEAD_DIM),
            kv1_s[:, :512].reshape(db, dec, 4, NSA_KV_HEADS, HEAD_DIM),
            win_s,
            lat_s.reshape(db, dec, MLA_KV_LORA + MLA_ROPE),
            kv3_s.reshape(db, dec, 2, MOBA_KV_HEADS, HEAD_DIM))
```

```python
import functools

import numpy as np
import jax
import jax.numpy as jnp
from jax import lax
from jax.experimental import pallas as pl
from jax.experimental.pallas import tpu as pltpu

F32 = jnp.float32
BF16 = jnp.bfloat16

HEAD_DIM = 64
ROPE_THETA = 10000.0
NORM_EPS = 1e-6
PAGE = 128
MOBA_KV_HEADS = 4
MOBA_BLOCK = 256
MOBA_TOPK = 3
NSA_KV_HEADS = 2
NSA_CMP_LEN = 32
NSA_CMP_STRIDE = 16
NSA_CMP_HIDDEN = 256
NSA_SLC_BLOCK = 64
NSA_SLC_TOPN = 16
NSA_WINDOW = 512
NSA_FORCE_SCORE = 1e9
MLA_Q_LORA = 384
MLA_KV_LORA = 256
MLA_NOPE = 64
MLA_ROPE = 32
MLA_V = 64

LANES = 128
VMEM_LIMIT_BYTES = 56 * 1024 * 1024
NEG = -1e30
TINY = float(np.finfo(np.float32).tiny)
LOG2E = 1.4426950408889634


def _cparams(*sem):
    return pltpu.CompilerParams(dimension_semantics=sem, vmem_limit_bytes=VMEM_LIMIT_BYTES)


def _dot(a, b):
    return jnp.dot(a, b, preferred_element_type=F32)


def _dot_nt(a, b):
    return lax.dot_general(a, b, (((1,), (1,)), ((), ())), preferred_element_type=F32)


def _dot_tn(a, b):
    return lax.dot_general(a, b, (((0,), (0,)), ((), ())), preferred_element_type=F32)


def _split_bf16(x):
    hi = x.astype(BF16)
    lo = (x - hi.astype(F32)).astype(BF16)
    return hi, lo


def _full(shape):
    n = len(shape)
    return pl.BlockSpec(shape, lambda *_: (0,) * n)


def _slot_perm(kv_heads, n_heads=16):
    grp = n_heads // kv_heads
    cols = []
    for s in range(n_heads // 2):
        p, r = divmod(s, grp)
        for h in ((2 * p) * grp + r, (2 * p + 1) * grp + r):
            cols.append(h * HEAD_DIM + np.arange(HEAD_DIM))
    return np.concatenate(cols)


def _slot_heads(kv_heads, n_heads=16):
    grp = n_heads // kv_heads
    out = []
    for s in range(n_heads // 2):
        p, r = divmod(s, grp)
        out.append(((2 * p) * grp + r, (2 * p + 1) * grp + r))
    return out


def _group_mean_matrix(groups, width):
    m = np.zeros((width, width), np.float32)
    for lo, hi in groups:
        m[lo:hi, lo:hi] = 1.0 / (hi - lo)
    return m


def _rope_tables(pos, width, segs):
    pos = np.asarray(pos, np.float64)
    cos = np.ones((pos.shape[0], width), np.float64)
    sin = np.zeros((pos.shape[0], width), np.float64)
    up = np.zeros((1, width), np.float32)
    for lo, dim in segs:
        half = dim // 2
        inv = ROPE_THETA ** (-np.arange(half, dtype=np.float64) / half)
        ang = pos[:, None] * inv[None, :]
        cos[:, lo:lo + half] = np.cos(ang)
        cos[:, lo + half:lo + dim] = np.cos(ang)
        sin[:, lo:lo + half] = -np.sin(ang)
        sin[:, lo + half:lo + dim] = np.sin(ang)
        up[:, lo:lo + half] = 1.0
    return cos.astype(np.float32), sin.astype(np.float32), up


def _head64_segs(width):
    return [(lo, HEAD_DIM) for lo in range(0, width, HEAD_DIM)]


def _rms_rows(x, g):
    ms = jnp.mean(x * x, axis=-1, keepdims=True)
    return x * lax.rsqrt(ms + NORM_EPS) * g


def _group_norm(a, gm, gain):
    ms = _dot((a * a).astype(BF16), gm)
    return a * lax.rsqrt(ms + NORM_EPS) * gain


def _rope(y, cos, sin, up, shift):
    w = y.shape[-1]
    hi = pltpu.roll(y, w - shift, 1)
    lo = pltpu.roll(y, shift, 1)
    partner = jnp.where(up > 0.5, hi, lo)
    return y * cos + partner * sin


def _gelu_tanh(x):
    return 0.5 * x * (1.0 + jnp.tanh(0.7978845608028654 * (x + 0.044715 * (x * x * x))))


def _mixer_out_mlp_kernel(x_ref, o_ref, wo_ref, g_ref, w1_ref, w2_ref, y_ref, *, ff_chunk):
    x = x_ref[...] + _dot(o_ref[...].astype(BF16), wo_ref[...])
    h = _rms_rows(x, g_ref[...]).astype(BF16)
    acc = x
    for c in range(w1_ref.shape[1] // ff_chunk):
        u = _dot(h, w1_ref[:, c * ff_chunk:(c + 1) * ff_chunk])
        u = jnp.maximum(u, 0.0)
        acc = acc + _dot((u * u).astype(BF16), w2_ref[c * ff_chunk:(c + 1) * ff_chunk, :])
    y_ref[...] = acc


def _mixer_out_mlp(x, o, wo, g, w1, w2, tn):
    n, d = x.shape
    k = o.shape[1]
    ff = w1.shape[1]
    row = lambda wd: pl.BlockSpec((tn, wd), lambda i: (i, 0))
    return pl.pallas_call(
        functools.partial(_mixer_out_mlp_kernel, ff_chunk=1024),
        out_shape=jax.ShapeDtypeStruct((n, d), F32),
        grid=(n // tn,),
        in_specs=[row(d), row(k), _full((k, d)), _full((1, d)), _full((d, ff)), _full((ff, d))],
        out_specs=row(d),
        compiler_params=_cparams("parallel"),
    )(x, o, wo, g, w1, w2)


def _moba_proj_kernel(x_ref, g_ref, w_ref, gm_ref, c_ref, cos_ref, sin_ref, q_ref, kv_ref, kvb_ref, kvt_ref):
    h = _rms_rows(x_ref[...], g_ref[...]).astype(BF16)
    gm = gm_ref[...]
    cos, sin = cos_ref[...], sin_ref[...]
    qg, kg, up = c_ref[0:1, :], c_ref[1:2, :], c_ref[2:3, :]
    for c in range(4):
        a = _dot(h, w_ref[:, c * 256:(c + 1) * 256])
        y = _rope(_group_norm(a, gm, qg), cos, sin, up, HEAD_DIM // 2)
        q_ref[:, c * 256:(c + 1) * 256] = (y * (HEAD_DIM ** -0.5 * LOG2E)).astype(q_ref.dtype)
    a = _dot(h, w_ref[:, 1024:1280])
    k = _rope(_group_norm(a, gm, kg), cos, sin, up, HEAD_DIM // 2)
    v = _dot(h, w_ref[:, 1280:1536])
    kv_ref[:, 0:256] = k
    kv_ref[:, 256:512] = v
    kvb_ref[:, 0:256] = k.astype(BF16)
    kvb_ref[:, 256:512] = v.astype(BF16)
    kvt_ref[0:256, :] = k.T
    kvt_ref[256:512, :] = v.T


def _t_spec(feat, tn, nper):
    return pl.BlockSpec((None, feat, tn), lambda i: (i // nper, 0, i % nper))


def _moba_proj(x, g, w, qn, kn, pos, tn, qdtype):
    n, d = x.shape
    period = pos.shape[0]
    cos, sin, up = _rope_tables(pos, 256, _head64_segs(256))
    consts = np.zeros((8, 256), np.float32)
    consts[2] = up[0]
    consts = jnp.asarray(consts).at[0].set(jnp.tile(qn, 4)).at[1].set(jnp.tile(kn, 4))
    gm = jnp.asarray(_group_mean_matrix([(lo, lo + 64) for lo in range(0, 256, 64)], 256), BF16)
    nper = period // tn
    tab = pl.BlockSpec((tn, 256), lambda i: (i % nper, 0))
    return pl.pallas_call(
        _moba_proj_kernel,
        out_shape=(jax.ShapeDtypeStruct((n, 1024), qdtype),
                   jax.ShapeDtypeStruct((n, 512), F32),
                   jax.ShapeDtypeStruct((n, 512), BF16),
                   jax.ShapeDtypeStruct((n // period, 512, period), F32)),
        grid=(n // tn,),
        in_specs=[pl.BlockSpec((tn, d), lambda i: (i, 0)), _full((1, d)), _full(w.shape),
                  _full((256, 256)), _full((8, 256)), tab, tab],
        out_specs=(pl.BlockSpec((tn, 1024), lambda i: (i, 0)),
                   pl.BlockSpec((tn, 512), lambda i: (i, 0)),
                   pl.BlockSpec((tn, 512), lambda i: (i, 0)),
                   _t_spec(512, tn, nper)),
        compiler_params=_cparams("parallel"),
    )(x, g, w, gm, consts, jnp.asarray(cos), jnp.asarray(sin))


def _topk_mask_t(score_t, k, n_valid):
    nc, n = score_t.shape
    cand = lax.broadcasted_iota(jnp.int32, (nc, n), 0)
    rank = jnp.zeros((nc, n), jnp.int32)
    for j in range(n_valid):
        sj = score_t[j:j + 1, :]
        ahead = (sj > score_t) | ((sj == score_t) & (cand > j))
        rank = rank + ahead.astype(jnp.int32)
    return rank < k


def _bias_rows(sel_t):
    nc, n = sel_t.shape
    bias_t = jnp.where(sel_t, 0.0, NEG)
    if nc < LANES:
        bias_t = jnp.concatenate([bias_t, jnp.zeros((LANES - nc, n), F32)], axis=0)
    return bias_t.T


def _rep(x, n):
    return x if n == 1 else jnp.concatenate([x] * n, axis=1)


def _flash_first(s, pv, m_ref, l_ref, acc_ref):
    m = jnp.max(s, axis=-1, keepdims=True)
    p = jnp.exp2(s - m)
    m_ref[...] = jnp.broadcast_to(m, m_ref.shape)
    if l_ref is not None:
        l_ref[...] = jnp.broadcast_to(jnp.sum(p, axis=-1, keepdims=True), l_ref.shape)
    acc_ref[...] = pv(p.astype(BF16))


def _flash_masked(s, ok, pv, m_ref, l_ref, acc_ref):
    s = jnp.where(ok, s, NEG)
    m_old = m_ref[...]
    m_new = jnp.maximum(m_old, jnp.max(s, axis=-1, keepdims=True))
    p = jnp.where(ok, jnp.exp2(s - _rep(m_new, s.shape[1] // LANES)), 0.0)
    alpha = jnp.exp2(m_old - m_new)
    m_ref[...] = m_new
    if l_ref is not None:
        l_ref[...] = alpha * l_ref[...] + jnp.sum(p, axis=-1, keepdims=True)
    acc_ref[...] = _rep(alpha, acc_ref.shape[1] // LANES) * acc_ref[...] + pv(p.astype(BF16))


def _flash_next(s, pv, m_ref, l_ref, acc_ref):
    m_old = m_ref[...]
    m_new = jnp.maximum(m_old, jnp.max(s, axis=-1, keepdims=True))
    p = jnp.exp2(s - _rep(m_new, s.shape[1] // LANES))
    alpha = jnp.exp2(m_old - m_new)
    m_ref[...] = m_new
    if l_ref is not None:
        l_ref[...] = alpha * l_ref[...] + jnp.sum(p, axis=-1, keepdims=True)
    acc_ref[...] = _rep(alpha, acc_ref.shape[1] // LANES) * acc_ref[...] + pv(p.astype(BF16))


def _with_ones(v):
    return jnp.concatenate([v, jnp.ones(v.shape, BF16)], axis=1)


def _finish(acc_ref):
    acc = acc_ref[...]
    return acc[:, 0:LANES] / acc[:, LANES:2 * LANES]


def _slot_rows(q_ref, n_slots, first=0):
    tq = q_ref.shape[0]
    lane = lax.broadcasted_iota(jnp.int32, (tq, LANES), 1)
    ev, od = [], []
    for s in range(first, first + n_slots):
        qs = q_ref[:, s * LANES:(s + 1) * LANES]
        ev.append(jnp.where(lane < HEAD_DIM, qs, jnp.zeros_like(qs)))
        od.append(jnp.where(lane >= HEAD_DIM, qs, jnp.zeros_like(qs)))
    return jnp.concatenate(ev + od, axis=0)


def _slot_out(o, n_slots, tq):
    lane = lax.broadcasted_iota(jnp.int32, (tq, LANES), 1)
    outs = []
    for s in range(n_slots):
        e = o[s * tq:(s + 1) * tq, :]
        d = o[(n_slots + s) * tq:(n_slots + s + 1) * tq, :]
        outs.append(jnp.where(lane < HEAD_DIM, e, d))
    return jnp.concatenate(outs, axis=1)


def _moba_prompt_kernel(q_ref, k_ref, v_ref, bm_ref, o_ref, m_ref, acc_ref):
    i = pl.program_id(1)
    tq = q_ref.shape[0]
    nb = k_ref.shape[0] // MOBA_BLOCK
    rows = 8 * tq
    qas, qaugs = [], []
    blk = lax.broadcasted_iota(jnp.int32, (nb, rows), 0)
    for pr in range(2):
        qa = _slot_rows(q_ref, 4, first=4 * pr)
        bhi, blo = _split_bf16(bm_ref[:, pr * LANES:(pr + 1) * LANES])
        gate_t = _dot_nt(bhi, qa) + _dot_nt(blo, qa)
        gate_t = jnp.where(blk < i, gate_t, -jnp.inf)
        sel_t = _topk_mask_t(gate_t, MOBA_TOPK, nb) & (blk < i)
        qas.append(qa)
        qaugs.append(jnp.concatenate([qa, _bias_rows(sel_t).astype(BF16)], axis=1))

    def kv_tile(off, pr):
        return (k_ref[pl.ds(off, MOBA_BLOCK), pr * LANES:(pr + 1) * LANES],
                _with_ones(v_ref[pl.ds(off, MOBA_BLOCK), pr * LANES:(pr + 1) * LANES]))

    offd = pl.multiple_of(i * MOBA_BLOCK, MOBA_BLOCK)
    qi = lax.broadcasted_iota(jnp.int32, (rows, MOBA_BLOCK), 0) & (tq - 1)
    ki = lax.broadcasted_iota(jnp.int32, (rows, MOBA_BLOCK), 1)
    for pr in range(2):
        kd, vd = kv_tile(offd, pr)
        _flash_first(jnp.where(ki <= qi, _dot_nt(qas[pr], kd), NEG), lambda p: _dot(p, vd),
                     m_ref.at[pr], None, acc_ref.at[pr])

    lane = lax.broadcasted_iota(jnp.int32, (MOBA_BLOCK, LANES), 1)

    def past_block(j, carry):
        off = pl.multiple_of(j * MOBA_BLOCK, MOBA_BLOCK)
        onehot = jnp.where(lane == j, 1.0, 0.0).astype(BF16)
        for pr in range(2):
            kj, vj = kv_tile(off, pr)
            _flash_next(_dot_nt(qaugs[pr], jnp.concatenate([kj, onehot], axis=1)), lambda p: _dot(p, vj),
                        m_ref.at[pr], None, acc_ref.at[pr])
        return carry

    lax.fori_loop(0, i, past_block, 0)

    o_ref[...] = jnp.concatenate([_slot_out(_finish(acc_ref.at[pr]), 4, tq) for pr in range(2)],
                                 axis=1).astype(o_ref.dtype)


def _moba_prompt_attn(q, kvb, bm, batch, seq):
    tq = MOBA_BLOCK
    nq = seq // tq
    nb = seq // MOBA_BLOCK
    return pl.pallas_call(
        _moba_prompt_kernel,
        out_shape=jax.ShapeDtypeStruct((batch * seq, 1024), BF16),
        grid=(batch, nq),
        in_specs=[pl.BlockSpec((tq, 1024), lambda b, i: (b * nq + i, 0)),
                  pl.BlockSpec((seq, 256), lambda b, i: (b, 0)),
                  pl.BlockSpec((seq, 256), lambda b, i: (b, 1)),
                  pl.BlockSpec((nb, 256), lambda b, i: (b, 0))],
        out_specs=pl.BlockSpec((tq, 1024), lambda b, i: (b * nq + i, 0)),
        scratch_shapes=[pltpu.VMEM((2, 8 * tq, 128), F32), pltpu.VMEM((2, 8 * tq, 256), F32)],
        compiler_params=_cparams("parallel", "arbitrary"),
    )(q, kvb, kvb, bm)


def _bmean_kernel(k_ref, o_ref):
    nb = o_ref.shape[0]
    k = k_ref[...].reshape(nb, MOBA_BLOCK, k_ref.shape[1])
    o_ref[...] = jnp.sum(k, axis=1) * (1.0 / MOBA_BLOCK)


def _moba_bmean(kv, batch, seq):
    nb = seq // MOBA_BLOCK
    return pl.pallas_call(
        _bmean_kernel,
        out_shape=jax.ShapeDtypeStruct((batch * nb, 256), F32),
        grid=(batch,),
        in_specs=[pl.BlockSpec((seq, 256), lambda b: (b, 0))],
        out_specs=pl.BlockSpec((nb, 256), lambda b: (b, 0)),
        compiler_params=_cparams("parallel"),
    )(kv)


def _moba_sample_kernel(pt_ref, q_ref, kvn_ref, *refs, bps, nb_past):
    pages = refs[:2 * bps]
    o_ref = refs[2 * bps]
    qa_ref, bm_ref, mst_ref, lst_ref, oacc_ref = refs[2 * bps + 1:]
    s = pl.program_id(1)
    rows = 128
    lane128 = lax.broadcasted_iota(jnp.int32, (8, LANES), 1)

    @pl.when(s == 0)
    def _():
        q = q_ref[...]
        z = jnp.zeros((8, LANES), F32)
        ev, od = [], []
        for sl in range(8):
            qs = q[:, sl * LANES:(sl + 1) * LANES]
            e = jnp.where(lane128 < HEAD_DIM, qs, 0.0)
            d = jnp.where(lane128 >= HEAD_DIM, qs, 0.0)
            if sl // 4 == 0:
                ev.append(jnp.concatenate([e, z], axis=1))
                od.append(jnp.concatenate([d, z], axis=1))
            else:
                ev.append(jnp.concatenate([z, e], axis=1))
                od.append(jnp.concatenate([z, d], axis=1))
        qa_ref[...] = jnp.concatenate(ev + od, axis=0).astype(BF16)
        bm_ref[...] = jnp.zeros_like(bm_ref)
        mst_ref[...] = jnp.zeros_like(mst_ref)
        lst_ref[...] = jnp.zeros_like(lst_ref)

    qa = qa_ref[...]
    col = lax.broadcasted_iota(jnp.int32, (rows, LANES), 1)
    for t in range(bps):
        jg = s * bps + t
        kt = jnp.concatenate([pages[2 * t][0:256, :], pages[2 * t + 1][0:256, :]], axis=1)
        vt = jnp.concatenate([pages[2 * t][256:512, :], pages[2 * t + 1][256:512, :]], axis=1)
        sc = _dot(qa, kt.astype(BF16))
        bm_ref[...] = jnp.where(col == jg, jnp.sum(sc, axis=-1, keepdims=True), bm_ref[...])
        m = jnp.max(sc, axis=-1, keepdims=True)
        p = jnp.exp2(sc - m)
        oacc_ref[jg] = _dot_nt(p.astype(BF16), vt.astype(BF16))
        mst_ref[...] = jnp.where(col == jg, m, mst_ref[...])
        lst_ref[...] = jnp.where(col == jg, jnp.sum(p, axis=-1, keepdims=True), lst_ref[...])

    @pl.when(s == pl.num_programs(1) - 1)
    def _():
        kn = kvn_ref[0:256, :].astype(BF16)
        vn = kvn_ref[256:512, :].astype(BF16)
        sn = _dot(qa, kn)
        qi = lax.broadcasted_iota(jnp.int32, (rows, LANES), 0) & 7
        okn = col <= qi
        sn = jnp.where(okn, sn, NEG)
        mn = jnp.max(sn, axis=-1, keepdims=True)
        pn = jnp.where(okn, jnp.exp2(sn - mn), 0.0)
        ln = jnp.sum(pn, axis=-1, keepdims=True)
        on = _dot_nt(pn.astype(BF16), vn)
        gate_t = bm_ref[...].T[0:nb_past, :]
        sel_t = jnp.where(_topk_mask_t(gate_t, MOBA_TOPK, nb_past), 1.0, 0.0)
        sel = jnp.concatenate([sel_t, jnp.zeros((LANES - nb_past, rows), F32)], axis=0).T > 0.5
        mst = mst_ref[...]
        mstar = jnp.maximum(jnp.max(jnp.where(sel, mst, NEG), axis=-1, keepdims=True), mn)
        w = jnp.where(sel, jnp.exp2(mst - mstar), 0.0)
        wn = jnp.exp2(mn - mstar)
        den = jnp.sum(w * lst_ref[...], axis=-1, keepdims=True) + wn * ln
        acc = wn * on
        for j in range(nb_past):
            acc = acc + w[:, j:j + 1] * oacc_ref[j]
        o = acc / den
        outs = []
        for sl in range(8):
            c = (sl // 4) * LANES
            e = o[sl * 8:(sl + 1) * 8, c:c + LANES]
            d = o[(8 + sl) * 8:(9 + sl) * 8, c:c + LANES]
            outs.append(jnp.where(lane128 < HEAD_DIM, e, d))
        o_ref[...] = jnp.concatenate(outs, axis=1)


def _pages_t(cache):
    return jnp.swapaxes(cache.reshape(cache.shape[0], PAGE, -1), 1, 2)


def _rows_t(x, db):
    xt = jnp.swapaxes(x.reshape(db, x.shape[0] // db, x.shape[1]), 1, 2)
    return jnp.pad(xt, ((0, 0), (0, 0), (0, LANES - xt.shape[2])))


def _moba_sample_attn(q, kvn_t, cache_t, page_table, bps):
    db, n_pages = page_table.shape
    nb_past = n_pages * PAGE // MOBA_BLOCK
    assert n_pages % (2 * bps) == 0 and nb_past < LANES
    nsteps = n_pages // (2 * bps)

    def page_spec(t):
        return pl.BlockSpec((None, 512, PAGE), lambda b, s, pt: (pt[b, s * 2 * bps + t], 0, 0))

    grid_spec = pltpu.PrefetchScalarGridSpec(
        num_scalar_prefetch=1,
        grid=(db, nsteps),
        in_specs=[pl.BlockSpec((8, 1024), lambda b, s, pt: (b, 0)),
                  pl.BlockSpec((None, 512, LANES), lambda b, s, pt: (b, 0, 0))]
                 + [page_spec(t) for t in range(2 * bps)],
        out_specs=pl.BlockSpec((8, 1024), lambda b, s, pt: (b, 0)),
        scratch_shapes=[pltpu.VMEM((128, 256), BF16), pltpu.VMEM((128, LANES), F32),
                        pltpu.VMEM((128, LANES), F32), pltpu.VMEM((128, LANES), F32),
                        pltpu.VMEM((nb_past, 128, 256), F32)],
    )
    return pl.pallas_call(
        functools.partial(_moba_sample_kernel, bps=bps, nb_past=nb_past),
        out_shape=jax.ShapeDtypeStruct((db * 8, 1024), F32),
        grid_spec=grid_spec,
        compiler_params=_cparams("parallel", "arbitrary"),
    )(page_table, q, kvn_t, *([cache_t] * (2 * bps)))


def _moba_layer(xp, xs, cache, page_table, ln_g, w_in, qn, kn, w_o, batch, seq, past_len):
    perm = _slot_perm(MOBA_KV_HEADS)
    w_in_p = jnp.concatenate([w_in[:, :1024][:, perm], w_in[:, 1024:]], axis=1).astype(BF16)
    w_o_p = w_o[perm, :].astype(BF16)
    g = ln_g.reshape(1, -1)
    dec = 8
    q, kv, kvb, kvt = _moba_proj(xp, g, w_in_p, qn, kn, np.arange(seq), 512, BF16)
    bm = _moba_bmean(kv, batch, seq)
    o = _moba_prompt_attn(q, kvb, bm, batch, seq)
    ns = xs.shape[0]
    pos_s = np.tile(past_len + np.arange(dec), ns // dec)
    qs, kvs, _, _ = _moba_proj(xs, g, w_in_p, qn, kn, pos_s, ns, F32)
    db, n_pages = page_table.shape
    bps = max(b for b in (1, 2, 4, 8) if n_pages % (2 * b) == 0)
    os_ = _moba_sample_attn(qs, _rows_t(kvs, db), _pages_t(cache), page_table, bps)
    return o, os_, w_o_p, kvt, kvs


def _nsa_proj_kernel(x_ref, g_ref, w_ref, gm_ref, c_ref, cos_ref, sin_ref,
                     q_ref, kv_ref, kvb_ref, gate_ref, kvt_ref):
    h = _rms_rows(x_ref[...], g_ref[...]).astype(BF16)
    gm = gm_ref[...]
    cos, sin = cos_ref[...], sin_ref[...]
    qg, up = c_ref[0:1, :], c_ref[4:5, :]
    for c in range(4):
        a = _dot(h, w_ref[:, c * 256:(c + 1) * 256])
        y = _rope(_group_norm(a, gm, qg), cos, sin, up, HEAD_DIM // 2)
        q_ref[:, c * 256:(c + 1) * 256] = (y * (HEAD_DIM ** -0.5 * LOG2E)).astype(q_ref.dtype)
    gm1 = gm_ref[0:128, 0:128]
    for br in range(3):
        a = _dot(h, w_ref[:, 1024 + br * 256:1024 + (br + 1) * 256])
        k = _rope(_group_norm(a[:, 0:128], gm1, c_ref[1 + br:2 + br, 0:128]),
                  cos[:, 0:128], sin[:, 0:128], up[:, 0:128], HEAD_DIM // 2)
        v = a[:, 128:256]
        kv_ref[:, br * 256:br * 256 + 128] = k
        kv_ref[:, br * 256 + 128:(br + 1) * 256] = v
        kvb_ref[:, br * 256:br * 256 + 128] = k.astype(BF16)
        kvb_ref[:, br * 256 + 128:(br + 1) * 256] = v.astype(BF16)
        kvt_ref[br * 256:br * 256 + 128, :] = k.T
        kvt_ref[br * 256 + 128:(br + 1) * 256, :] = v.T
    a = _dot(h, w_ref[:, 1792:1920])
    gate_ref[...] = 1.0 / (1.0 + jnp.exp(-a))


def _nsa_proj(x, g, w, qn, kn, pos, tn, qdtype):
    n, d = x.shape
    period = pos.shape[0]
    cos, sin, up = _rope_tables(pos, 256, _head64_segs(256))
    consts = np.zeros((8, 256), np.float32)
    consts[4] = up[0]
    consts = jnp.asarray(consts).at[0].set(jnp.tile(qn, 4))
    for br in range(3):
        consts = consts.at[1 + br].set(jnp.tile(kn[br], 4))
    gm = jnp.asarray(_group_mean_matrix([(lo, lo + 64) for lo in range(0, 256, 64)], 256), BF16)
    nper = period // tn
    tab = pl.BlockSpec((tn, 256), lambda i: (i % nper, 0))
    row = lambda wd: pl.BlockSpec((tn, wd), lambda i: (i, 0))
    return pl.pallas_call(
        _nsa_proj_kernel,
        out_shape=(jax.ShapeDtypeStruct((n, 1024), qdtype),
                   jax.ShapeDtypeStruct((n, 768), F32),
                   jax.ShapeDtypeStruct((n, 768), BF16),
                   jax.ShapeDtypeStruct((n, 128), F32),
                   jax.ShapeDtypeStruct((n // period, 768, period), F32)),
        grid=(n // tn,),
        in_specs=[row(d), _full((1, d)), _full(w.shape), _full((256, 256)), _full((8, 256)), tab, tab],
        out_specs=(row(1024), row(768), row(768), row(128), _t_spec(768, tn, nper)),
        compiler_params=_cparams("parallel"),
    )(x, g, w, gm, consts, jnp.asarray(cos), jnp.asarray(sin))


def _nsa_cmp_weights(cmp_pe, cmp_w1, cmp_w2):
    w1h, pe_rows, w2blk = [], [], []
    for i in range(2):
        w1h.append(jnp.concatenate([cmp_w1[i][:1024], cmp_w1[i][1024:]], axis=1).astype(BF16))
        pe_rows.append(jnp.concatenate([cmp_pe[i].reshape(2, 1024), jnp.zeros((6, 1024), F32)], axis=0))
        z2 = jnp.zeros_like(cmp_w2[i])
        w2blk.append(jnp.concatenate([jnp.concatenate([cmp_w2[i], z2], axis=1),
                                      jnp.concatenate([z2, cmp_w2[i]], axis=1)], axis=0).astype(BF16))
    return w1h, pe_rows, w2blk


def _cmp_chunks(row_ref, m):
    lane = lax.broadcasted_iota(jnp.int32, (m, LANES), 1)
    g0, g1 = [], []
    for a in range(NSA_CMP_STRIDE // 2):
        xe = row_ref[pl.ds(2 * a, m, stride=NSA_CMP_STRIDE), :]
        xo = row_ref[pl.ds(2 * a + 1, m, stride=NSA_CMP_STRIDE), :]
        g0.append(jnp.where(lane < HEAD_DIM, xe, pltpu.roll(xo, HEAD_DIM, 1)))
        g1.append(jnp.where(lane < HEAD_DIM, pltpu.roll(xe, HEAD_DIM, 1), xo))
    return jnp.concatenate([jnp.concatenate(g0, axis=1), jnp.concatenate(g1, axis=1)], axis=0).astype(BF16)


def _cmp_tokens(pre_a, pre_b, pe_ref, w_ref, b1, w2_ref, m):
    pc = _dot(pe_ref[...].astype(BF16), w_ref[...])
    c = pc[0:1, 0:256] + pc[1:2, 256:512] + b1
    hid = _gelu_tanh(pre_a[:, 0:256] + pre_b[:, 256:512] + c)
    hid = jnp.concatenate([hid[0:m], hid[m:2 * m]], axis=1).astype(BF16)
    return _dot(hid, w2_ref[...])


def _nsa_cmp_prompt_kernel(k_ref, v_ref, wk_ref, wv_ref, pek_ref, pev_ref, b1_ref, w2k_ref, w2v_ref, o_ref):
    nch = o_ref.shape[0]
    for i, (r_ref, w_ref, pe_ref, w2_ref) in enumerate(((k_ref, wk_ref, pek_ref, w2k_ref),
                                                         (v_ref, wv_ref, pev_ref, w2v_ref))):
        pre = _dot(_cmp_chunks(r_ref, nch), w_ref[...])
        nxt = pltpu.roll(pre, 2 * nch - 1, 0)
        o_ref[:, i * 128:(i + 1) * 128] = _cmp_tokens(pre, nxt, pe_ref, w_ref, b1_ref[i:i + 1, :], w2_ref, nch)


def _nsa_cmp_prompt(kv, wbig, pe_rows, b1, w2blk, batch, seq):
    nch = seq // NSA_CMP_STRIDE
    return pl.pallas_call(
        _nsa_cmp_prompt_kernel,
        out_shape=jax.ShapeDtypeStruct((batch * nch, 256), F32),
        grid=(batch,),
        in_specs=[pl.BlockSpec((seq, 128), lambda b: (b, 0)), pl.BlockSpec((seq, 128), lambda b: (b, 1)),
                  _full((1024, 512)), _full((1024, 512)), _full((8, 1024)), _full((8, 1024)),
                  _full((8, 256)), _full((512, 128)), _full((512, 128))],
        out_specs=pl.BlockSpec((nch, 256), lambda b: (b, 0)),
        compiler_params=_cparams("parallel"),
    )(kv, kv, wbig[0], wbig[1], pe_rows[0], pe_rows[1], b1, w2blk[0], w2blk[1])


def _flash_init(m_ref, l_ref, acc_ref):
    m_ref[...] = jnp.full(m_ref.shape, NEG, F32)
    l_ref[...] = jnp.zeros(l_ref.shape, F32)
    acc_ref[...] = jnp.zeros(acc_ref.shape, F32)


def _nsa_prompt_kernel(q_ref, ks_ref, vs_ref, kw_ref, vw_ref, kvc_ref, gate_ref, ovl_ref, e_ref,
                       o_ref, m_ref, acc_ref, os_ref):
    i = pl.program_id(1)
    tq = q_ref.shape[0]
    rows = 16 * tq
    tk = 256
    q0 = i * tq
    qa = _slot_rows(q_ref, 8)
    ntok = kvc_ref.shape[0]

    kc = kvc_ref[:, 0:128].astype(BF16)
    vc = kvc_ref[:, 128:256].astype(BF16)
    qpos_c = q0 + (lax.broadcasted_iota(jnp.int32, (rows, ntok), 0) & (tq - 1))
    tok = lax.broadcasted_iota(jnp.int32, (rows, ntok), 1)
    ok = tok * NSA_CMP_STRIDE + (NSA_CMP_LEN - 1) <= qpos_c
    sc = jnp.where(ok, _dot_nt(qa, kc), NEG)
    mc = jnp.max(sc, axis=-1, keepdims=True)
    ec = jnp.where(ok, jnp.exp2(sc - mc), 0.0)
    pc = ec / jnp.maximum(jnp.sum(ec, axis=-1, keepdims=True), TINY)
    o_c = _dot(pc.astype(BF16), vc)

    pg = jnp.sum(pc.reshape(2, 8, tq, ntok), axis=1).reshape(2 * tq, ntok)
    phi, plo = _split_bf16(pg)
    nsb = ks_ref.shape[0] // NSA_SLC_BLOCK
    imp_t = (_dot_nt(ovl_ref[...], phi) + _dot_nt(ovl_ref[...], plo))[0:nsb, :]
    qp2 = q0 + (lax.broadcasted_iota(jnp.int32, (nsb, 2 * tq), 1) & (tq - 1))
    own = qp2 >> 6
    jb = lax.broadcasted_iota(jnp.int32, (nsb, 2 * tq), 0)
    allowed = jb <= own
    forced = (jb == 0) | (jb == own) | (jb == own - 1)
    imp_t = jnp.where(forced, NSA_FORCE_SCORE, imp_t)
    imp_t = jnp.where(allowed, imp_t, -jnp.inf)
    sel_t = _topk_mask_t(imp_t, NSA_SLC_TOPN, nsb) & allowed
    bias = _bias_rows(sel_t).astype(BF16)
    qaug = jnp.concatenate([qa, jnp.concatenate([bias[0:tq]] * 8 + [bias[tq:2 * tq]] * 8, axis=0)], axis=1)

    qp_r = q0 + (lax.broadcasted_iota(jnp.int32, (rows, tk), 0) & (tq - 1))
    klr = lax.broadcasted_iota(jnp.int32, (rows, tk), 1)
    jd = (q0 + tq - 1) // tk
    offd = pl.multiple_of(jd * tk, tk)

    def slc_scores(off):
        kj = jnp.concatenate([ks_ref[pl.ds(off, tk), :], e_ref[pl.ds(off, tk), :]], axis=1)
        return _dot_nt(qaug, kj)

    _flash_first(jnp.where(offd + klr <= qp_r, slc_scores(offd), NEG),
                 lambda p: _dot(p, _with_ones(vs_ref[pl.ds(offd, tk), :])), m_ref, None, acc_ref)

    def slc_body(j, carry):
        off = pl.multiple_of(j * tk, tk)
        _flash_next(slc_scores(off), lambda p: _dot(p, _with_ones(vs_ref[pl.ds(off, tk), :])),
                    m_ref, None, acc_ref)
        return carry

    lax.fori_loop(0, jd, slc_body, 0)
    os_ref[...] = _finish(acc_ref)

    def win_scores(off):
        dist = qp_r - (off + klr)
        return jnp.where((dist >= 0) & (dist < NSA_WINDOW), _dot_nt(qa, kw_ref[pl.ds(off, tk), :]), NEG)

    _flash_first(win_scores(offd), lambda p: _dot(p, _with_ones(vw_ref[pl.ds(offd, tk), :])),
                 m_ref, None, acc_ref)

    def win_body(j, carry):
        off = pl.multiple_of(j * tk, tk)
        _flash_next(win_scores(off), lambda p: _dot(p, _with_ones(vw_ref[pl.ds(off, tk), :])),
                    m_ref, None, acc_ref)
        return carry

    lax.fori_loop(jnp.maximum(q0 - (NSA_WINDOW - 1), 0) // tk, jd, win_body, 0)
    o_w = _finish(acc_ref)
    o_s = os_ref[...]

    gate = gate_ref[...]
    outs = []
    for h in range(16):
        r0 = h * tq
        outs.append(gate[:, 3 * h:3 * h + 1] * o_c[r0:r0 + tq, :]
                    + gate[:, 3 * h + 1:3 * h + 2] * o_s[r0:r0 + tq, :]
                    + gate[:, 3 * h + 2:3 * h + 3] * o_w[r0:r0 + tq, :])
    o_ref[...] = _slot_out(jnp.concatenate(outs, axis=0), 8, tq).astype(o_ref.dtype)


def _nsa_overlap(n_tok, n_blk, tok_shift, rows, cols):
    ovl = np.zeros((rows, cols), np.float32)
    t = np.arange(n_tok)[:, None] * NSA_CMP_STRIDE
    b = np.arange(n_blk)[None, :] * NSA_SLC_BLOCK
    ovl[tok_shift:tok_shift + n_tok, :n_blk] = ((t < b + NSA_SLC_BLOCK) & (t + NSA_CMP_LEN > b))
    return ovl


def _nsa_prompt_attn(q, kvb, kvc, gate, batch, seq):
    tq, tk = 256, 256
    nq = seq // tq
    nch = seq // NSA_CMP_STRIDE
    n_tok = (seq - NSA_CMP_LEN) // NSA_CMP_STRIDE + 1
    nsb = seq // NSA_SLC_BLOCK
    assert nch == LANES and nsb <= LANES
    ovl = jnp.asarray(_nsa_overlap(n_tok, nsb, 0, nch, LANES).T, BF16)
    e = np.zeros((seq, LANES), np.float32)
    e[np.arange(seq), np.arange(seq) // NSA_SLC_BLOCK] = 1.0
    e = jnp.asarray(e, BF16)
    seqcol = lambda c: pl.BlockSpec((seq, 128), lambda b, i: (b, c))
    return pl.pallas_call(
        _nsa_prompt_kernel,
        out_shape=jax.ShapeDtypeStruct((batch * seq, 1024), BF16),
        grid=(batch, nq),
        in_specs=[pl.BlockSpec((tq, 1024), lambda b, i: (b * nq + i, 0)),
                  seqcol(2), seqcol(3), seqcol(4), seqcol(5),
                  pl.BlockSpec((nch, 256), lambda b, i: (b, 0)),
                  pl.BlockSpec((tq, 128), lambda b, i: (b * nq + i, 0)),
                  _full((nch, LANES)), _full(e.shape)],
        out_specs=pl.BlockSpec((tq, 1024), lambda b, i: (b * nq + i, 0)),
        scratch_shapes=[pltpu.VMEM((16 * tq, 128), F32), pltpu.VMEM((16 * tq, 256), F32),
                        pltpu.VMEM((16 * tq, 128), F32)],
        compiler_params=_cparams("parallel", "arbitrary"),
    )(q, kvb, kvb, kvb, kvb, kvc, gate, ovl, e)


def _topk_mask_iter(score, k):
    rows, w = score.shape
    lane = lax.broadcasted_iota(jnp.int32, (rows, w), 1)
    taken = jnp.zeros((rows, w), jnp.int32)
    for _ in range(k):
        free = taken == 0
        cur = jnp.where(free, score, -jnp.inf)
        m = jnp.max(cur, axis=-1, keepdims=True)
        idx = jnp.min(jnp.where(free & (cur == m), lane, w), axis=-1, keepdims=True)
        taken = jnp.where(lane == idx, 1, taken)
    return taken > 0


def _sample_q_rows(q):
    lane = lax.broadcasted_iota(jnp.int32, (8, LANES), 1)
    ev = [jnp.where(lane < HEAD_DIM, q[:, s * LANES:(s + 1) * LANES], 0.0) for s in range(8)]
    od = [jnp.where(lane >= HEAD_DIM, q[:, s * LANES:(s + 1) * LANES], 0.0) for s in range(8)]
    return jnp.concatenate(ev + od, axis=0)


def _nsa_sample_cmp_kernel(pt_ref, q_ref, wk_ref, wv_ref, pek_ref, pev_ref, b1_ref, w2k_ref, w2v_ref,
                           ovl_ref, *refs, pps, past, blk_per_step):
    pages = refs[:pps]
    oc_ref, sel_ref = refs[pps], refs[pps + 1]
    kvc_ref, carry_ref, stage_ref = refs[pps + 2:]
    s = pl.program_id(1)
    m = pps * (PAGE // NSA_CMP_STRIDE)

    @pl.when(s == 0)
    def _():
        carry_ref[...] = jnp.zeros_like(carry_ref)

    row = lax.broadcasted_iota(jnp.int32, (2 * m, 512), 0)
    for i, (w_ref, pe_ref, w2_ref) in enumerate(((wk_ref, pek_ref, w2k_ref), (wv_ref, pev_ref, w2v_ref))):
        for t, pg in enumerate(pages):
            stage_ref[t * PAGE:(t + 1) * PAGE, :] = pg[i * 128:(i + 1) * 128, :].T
        pre = _dot(_cmp_chunks(stage_ref, m), w_ref[...])
        prev = pltpu.roll(pre, 1, 0)
        prev = jnp.where(row == 0, carry_ref[2 * i:2 * i + 1, :], prev)
        prev = jnp.where(row == m, carry_ref[2 * i + 1:2 * i + 2, :], prev)
        carry_ref[2 * i:2 * i + 1, :] = pre[m - 1:m, :]
        carry_ref[2 * i + 1:2 * i + 2, :] = pre[2 * m - 1:2 * m, :]
        kvc_ref[pl.ds(pl.multiple_of(s * m, m), m), i * 128:(i + 1) * 128] = _cmp_tokens(
            prev, pre, pe_ref, w_ref, b1_ref[i:i + 1, :], w2_ref, m)

    @pl.when(s == pl.num_programs(1) - 1)
    def _():
        nt = kvc_ref.shape[0]
        qa = _sample_q_rows(q_ref[...]).astype(BF16)
        kc = kvc_ref[:, 0:128].astype(BF16)
        vc = kvc_ref[:, 128:256].astype(BF16)
        r = lax.broadcasted_iota(jnp.int32, (128, nt), 1)
        qpos = past + (lax.broadcasted_iota(jnp.int32, (128, nt), 0) & 7)
        ok = (r >= 1) & ((r - 1) * NSA_CMP_STRIDE + (NSA_CMP_LEN - 1) <= qpos)
        sc = jnp.where(ok, _dot_nt(qa, kc), NEG)
        mc = jnp.max(sc, axis=-1, keepdims=True)
        ec = jnp.where(ok, jnp.exp2(sc - mc), 0.0)
        pc = ec / jnp.maximum(jnp.sum(ec, axis=-1, keepdims=True), TINY)
        oc_ref[...] = _dot(pc.astype(BF16), vc)
        pg = jnp.sum(pc.reshape(2, 8, 8, nt), axis=1).reshape(16, nt)
        phi, plo = _split_bf16(pg)
        imp = _dot(phi, ovl_ref[...]) + _dot(plo, ovl_ref[...])
        nl = imp.shape[1]
        own = (past + (lax.broadcasted_iota(jnp.int32, (16, nl), 0) & 7)) >> 6
        jb = lax.broadcasted_iota(jnp.int32, (16, nl), 1)
        allowed = jb <= own
        forced = (jb == 0) | (jb == own) | (jb == own - 1)
        imp = jnp.where(forced, NSA_FORCE_SCORE, imp)
        imp = jnp.where(allowed, imp, -jnp.inf)
        sel = jnp.where(_topk_mask_iter(imp, NSA_SLC_TOPN) & allowed, 1.0, 0.0)
        lane = lax.broadcasted_iota(jnp.int32, (16, LANES), 1)
        for st in range(sel_ref.shape[0]):
            piece = sel if st == 0 else pltpu.roll(sel, nl - st * blk_per_step, 1)
            sel_ref[st] = jnp.where(lane < blk_per_step, piece[:, 0:LANES], 0.0)


def _nsa_sample_cmp(q, cache, page_table, w1h, pe_rows, b1, w2blk, past, pps, sel_pps):
    db, n_pages = page_table.shape
    nsteps = n_pages // pps
    nsel = n_pages // sel_pps
    nt = n_pages * (PAGE // NSA_CMP_STRIDE)
    n_tok = (past + 8 - NSA_CMP_LEN) // NSA_CMP_STRIDE + 1
    nsb = -(-(past + 8) // NSA_SLC_BLOCK)
    nl = -(-nsb // LANES) * LANES
    blk_per_step = sel_pps * PAGE // NSA_SLC_BLOCK
    assert n_tok == nt - 1 and blk_per_step <= LANES
    ovl = jnp.asarray(_nsa_overlap(n_tok, nsb, 1, nt, nl), BF16)

    def page_spec(t):
        return pl.BlockSpec((None, 256, PAGE), lambda b, s, pt: (pt[b, s * pps + t], 0, 0))

    grid_spec = pltpu.PrefetchScalarGridSpec(
        num_scalar_prefetch=1,
        grid=(db, nsteps),
        in_specs=[pl.BlockSpec((8, 1024), lambda b, s, pt: (b, 0)),
                  _full((1024, 512)), _full((1024, 512)), _full((8, 1024)), _full((8, 1024)),
                  _full((8, 256)), _full((512, 128)), _full((512, 128)), _full((nt, nl))]
                 + [page_spec(t) for t in range(pps)],
        out_specs=(pl.BlockSpec((128, 128), lambda b, s, pt: (b, 0)),
                   pl.BlockSpec((None, nsel, 16, LANES), lambda b, s, pt: (b, 0, 0, 0))),
        scratch_shapes=[pltpu.VMEM((nt, 256), F32), pltpu.VMEM((8, 512), F32),
                        pltpu.VMEM((pps * PAGE, 128), F32)],
    )
    return pl.pallas_call(
        functools.partial(_nsa_sample_cmp_kernel, pps=pps, past=past, blk_per_step=blk_per_step),
        out_shape=(jax.ShapeDtypeStruct((db * 128, 128), F32),
                   jax.ShapeDtypeStruct((db, nsel, 16, LANES), F32)),
        grid_spec=grid_spec,
        compiler_params=_cparams("parallel", "arbitrary"),
    )(page_table, q, w1h[0], w1h[1], pe_rows[0], pe_rows[1], b1, w2blk[0], w2blk[1], ovl,
      *([cache] * pps))


def _nsa_sample_attn_kernel(pt_ref, q_ref, sel_ref, oc_ref, kvn_ref, win_ref, gate_ref, e_ref, *refs,
                            pps, past):
    pages = refs[:pps]
    o_ref = refs[pps]
    qa_ref, m_ref, acc_ref = refs[pps + 1:]
    s = pl.program_id(1)
    rows = 128

    @pl.when(s == 0)
    def _():
        qa_ref[...] = _sample_q_rows(q_ref[...]).astype(BF16)
        m_ref[...] = jnp.full(m_ref.shape, NEG, F32)
        acc_ref[...] = jnp.zeros(acc_ref.shape, F32)

    qa = qa_ref[...]
    kt = jnp.concatenate([pg[0:128, :] for pg in pages], axis=1).astype(BF16)
    vt = jnp.concatenate([pg[128:256, :] for pg in pages], axis=1).astype(BF16)
    ones_t = jnp.ones(vt.shape, BF16)
    vt1 = jnp.concatenate([vt, ones_t], axis=0)
    sel = sel_ref[...]
    selrows = jnp.concatenate([sel[0:8, :]] * 8 + [sel[8:16, :]] * 8, axis=0)
    qaug = jnp.concatenate([qa, jnp.where(selrows > 0.5, 0.0, NEG).astype(BF16)], axis=1)
    _flash_next(_dot(qaug, jnp.concatenate([kt, e_ref[...]], axis=0)), lambda p: _dot_nt(p, vt1),
                m_ref, None, acc_ref)

    @pl.when(s == pl.num_programs(1) - 1)
    def _():
        col = lax.broadcasted_iota(jnp.int32, (rows, LANES), 1)
        qi = lax.broadcasted_iota(jnp.int32, (rows, LANES), 0) & 7
        kn = kvn_ref[256:384, :].astype(BF16)
        vn = jnp.concatenate([kvn_ref[384:512, :].astype(BF16), ones_t[:, 0:LANES]], axis=0)
        _flash_masked(_dot(qa, kn), col <= qi, lambda p: _dot_nt(p, vn), m_ref, None, acc_ref)
        o_s = _finish(acc_ref)
        wb = win_ref.shape[1]
        kw = jnp.concatenate([win_ref[0:128, :], kvn_ref[512:640, :]], axis=1).astype(BF16)
        vw = jnp.concatenate([win_ref[128:256, :], kvn_ref[640:768, :]], axis=1).astype(BF16)
        nw = wb + LANES
        c = lax.broadcasted_iota(jnp.int32, (rows, nw), 1)
        qpos = past + (lax.broadcasted_iota(jnp.int32, (rows, nw), 0) & 7)
        wpos = past - wb + c
        dist = qpos - wpos
        okw = (dist >= 0) & (dist < NSA_WINDOW) & (wpos >= 0) & (c < wb + 8)
        sw = jnp.where(okw, _dot(qa, kw), NEG)
        mw = jnp.max(sw, axis=-1, keepdims=True)
        pw = jnp.where(okw, jnp.exp2(sw - mw), 0.0)
        o_w = _dot_nt(pw.astype(BF16), vw) / jnp.maximum(jnp.sum(pw, axis=-1, keepdims=True), TINY)
        o_c = oc_ref[...]
        gate = gate_ref[...]
        lane8 = lax.broadcasted_iota(jnp.int32, (8, LANES), 1)
        hs = []
        for h in range(16):
            r0 = h * 8
            hs.append(gate[:, 3 * h:3 * h + 1] * o_c[r0:r0 + 8, :]
                      + gate[:, 3 * h + 1:3 * h + 2] * o_s[r0:r0 + 8, :]
                      + gate[:, 3 * h + 2:3 * h + 3] * o_w[r0:r0 + 8, :])
        o_ref[...] = jnp.concatenate([jnp.where(lane8 < HEAD_DIM, hs[sl], hs[8 + sl]) for sl in range(8)],
                                     axis=1)


def _nsa_sample_attn(q, sel, o_c, kvn, state_win, gate, cache, page_table, past, pps):
    db, n_pages = page_table.shape
    nsteps = n_pages // pps
    wb = state_win.shape[2]
    nk = pps * PAGE
    e = np.zeros((LANES, nk), np.float32)
    e[np.arange(nk) // NSA_SLC_BLOCK, np.arange(nk)] = 1.0

    def page_spec(t):
        return pl.BlockSpec((None, 256, PAGE), lambda b, s, pt: (pt[b, s * pps + t], 1, 0))

    grid_spec = pltpu.PrefetchScalarGridSpec(
        num_scalar_prefetch=1,
        grid=(db, nsteps),
        in_specs=[pl.BlockSpec((8, 1024), lambda b, s, pt: (b, 0)),
                  pl.BlockSpec((None, None, 16, LANES), lambda b, s, pt: (b, s, 0, 0)),
                  pl.BlockSpec((128, 128), lambda b, s, pt: (b, 0)),
                  pl.BlockSpec((None, 768, LANES), lambda b, s, pt: (b, 0, 0)),
                  pl.BlockSpec((None, 256, wb), lambda b, s, pt: (b, 0, 0)),
                  pl.BlockSpec((8, 128), lambda b, s, pt: (b, 0)),
                  _full((LANES, nk))]
                 + [page_spec(t) for t in range(pps)],
        out_specs=pl.BlockSpec((8, 1024), lambda b, s, pt: (b, 0)),
        scratch_shapes=[pltpu.VMEM((128, 128), BF16), pltpu.VMEM((128, 128), F32), pltpu.VMEM((128, 256), F32)],
    )
    return pl.pallas_call(
        functools.partial(_nsa_sample_attn_kernel, pps=pps, past=past),
        out_shape=jax.ShapeDtypeStruct((db * 8, 1024), F32),
        grid_spec=grid_spec,
        compiler_params=_cparams("parallel", "arbitrary"),
    )(page_table, q, sel, o_c, kvn, state_win, gate, jnp.asarray(e, BF16), *([cache] * pps))


def _nsa_layer(xp, xs, cache, state_win, page_table, ln_g, w_in, qn, kn, cmp_pe, cmp_w1, cmp_b1, cmp_w2,
               w_o, batch, seq, past_len):
    perm = _slot_perm(NSA_KV_HEADS)
    w_in_p = jnp.concatenate([w_in[:, :1024][:, perm], w_in[:, 1024:],
                              jnp.zeros((w_in.shape[0], 1920 - w_in.shape[1]), F32)], axis=1).astype(BF16)
    w_o_p = w_o[perm, :].astype(BF16)
    g = ln_g.reshape(1, -1)
    wbig, pe_rows, w2blk = _nsa_cmp_weights(cmp_pe, cmp_w1, cmp_w2)
    b1 = jnp.concatenate([cmp_b1, jnp.zeros((6, NSA_CMP_HIDDEN), F32)], axis=0)
    dec = 8
    q, kv, kvb, gate, kvt = _nsa_proj(xp, g, w_in_p, qn, kn, np.arange(seq), 512, BF16)
    kvc = _nsa_cmp_prompt(kv, wbig, pe_rows, b1, w2blk, batch, seq)
    o = _nsa_prompt_attn(q, kvb, kvc, gate, batch, seq)
    ns = xs.shape[0]
    db, n_pages = page_table.shape
    pps = min(16, n_pages)
    pos_s = np.tile(past_len + np.arange(dec), ns // dec)
    qs, kvs, _, gs, _ = _nsa_proj(xs, g, w_in_p, qn, kn, pos_s, ns, F32)
    cache_t = _pages_t(cache)
    win_t = jnp.swapaxes(state_win.reshape(db, state_win.shape[1], 256), 1, 2)
    o_c, sel = _nsa_sample_cmp(qs, cache_t, page_table, wbig, pe_rows, b1, w2blk, past_len,
                               min(32, n_pages), pps)
    os_ = _nsa_sample_attn(qs, sel, o_c, _rows_t(kvs, db), win_t, gs, cache_t, page_table, past_len, pps)
    return o, os_, w_o_p, kvt, kvs


MLA_QK = MLA_NOPE + MLA_ROPE


def _mla_proj_kernel(x_ref, g_ref, w_ref, wuq_ref, gm_ref, c_ref, cosq_ref, sinq_ref, cosk_ref, sink_ref,
                     q_ref, lat_ref, latt_ref):
    h = _rms_rows(x_ref[...], g_ref[...]).astype(BF16)
    cq = _rms_rows(_dot(h, w_ref[:, 0:MLA_Q_LORA]), c_ref[0:1, 0:MLA_Q_LORA]).astype(BF16)
    ckv = _rms_rows(_dot(h, w_ref[:, MLA_Q_LORA:MLA_Q_LORA + MLA_KV_LORA]), c_ref[1:2, 0:MLA_KV_LORA])
    a = _dot(h, w_ref[:, 640:768])
    ms = jnp.sum(a * a, axis=-1, keepdims=True) * (1.0 / MLA_ROPE)
    kpe = a * lax.rsqrt(ms + NORM_EPS) * c_ref[2:3, 0:128]
    kpe = _rope(kpe, cosk_ref[...], sink_ref[...], c_ref[3:4, 0:128], MLA_ROPE // 2)
    lat_ref[:, 0:MLA_KV_LORA] = ckv
    lat_ref[:, MLA_KV_LORA:MLA_KV_LORA + MLA_ROPE] = kpe[:, 0:MLA_ROPE]
    latt_ref[0:MLA_KV_LORA, :] = ckv.T
    latt_ref[MLA_KV_LORA:MLA_KV_LORA + MLA_ROPE, :] = kpe.T[0:MLA_ROPE, :]
    gm = gm_ref[...]
    qg, up = c_ref[4:5, 0:256], c_ref[5:6, 0:256]
    cos, sin = cosq_ref[...], sinq_ref[...]
    for c in range(8):
        a = _dot(cq, wuq_ref[:, c * 256:(c + 1) * 256])
        y = _rope(_group_norm(a, gm, qg), cos, sin, up, MLA_ROPE // 2)
        q_ref[:, c * 256:(c + 1) * 256] = (y * (MLA_QK ** -0.5 * LOG2E)).astype(q_ref.dtype)


def _mla_cat_groups(width):
    g = []
    for lo in range(0, width, LANES):
        g += [(lo, lo + MLA_NOPE), (lo + MLA_NOPE, lo + MLA_QK)]
    return g


def _mla_cat_cols():
    idx = -np.ones((16, LANES), np.int64)
    for h in range(16):
        idx[h, :MLA_QK] = h * MLA_QK + np.arange(MLA_QK)
    return idx.reshape(-1)


def _take_cols(w, idx):
    wz = jnp.concatenate([w, jnp.zeros((w.shape[0], 1), w.dtype)], axis=1)
    return wz[:, np.where(idx < 0, w.shape[1], idx)]


def _mla_proj(x, g, w_dqkv, g_q, g_kv, w_uq, qn, kn, pos, tn, qdtype):
    n, d = x.shape
    period = pos.shape[0]
    w = jnp.concatenate([w_dqkv, jnp.zeros((d, 768 - w_dqkv.shape[1]), F32)], axis=1).astype(BF16)
    wuq = _take_cols(w_uq, _mla_cat_cols()).astype(BF16)
    cosq, sinq, upq = _rope_tables(pos, 256, [(MLA_NOPE, MLA_ROPE), (LANES + MLA_NOPE, MLA_ROPE)])
    cosk, sink, upk = _rope_tables(pos, 128, [(0, MLA_ROPE)])
    consts = jnp.zeros((8, 384), F32)
    consts = consts.at[0, :].set(g_q).at[1, 0:256].set(g_kv).at[2, 0:MLA_ROPE].set(kn[MLA_NOPE:])
    consts = consts.at[3, 0:128].set(jnp.asarray(upk[0])).at[5, 0:256].set(jnp.asarray(upq[0]))
    qgain = jnp.concatenate([qn, jnp.zeros((LANES - MLA_QK,), F32)])
    consts = consts.at[4, 0:256].set(jnp.tile(qgain, 2))
    gm = jnp.asarray(_group_mean_matrix(_mla_cat_groups(256), 256), BF16)
    nper = period // tn
    tabq = pl.BlockSpec((tn, 256), lambda i: (i % nper, 0))
    tabk = pl.BlockSpec((tn, 128), lambda i: (i % nper, 0))
    row = lambda wd: pl.BlockSpec((tn, wd), lambda i: (i, 0))
    return pl.pallas_call(
        _mla_proj_kernel,
        out_shape=(jax.ShapeDtypeStruct((n, 2048), qdtype), jax.ShapeDtypeStruct((n, 288), F32),
                   jax.ShapeDtypeStruct((n // period, 288, period), F32)),
        grid=(n // tn,),
        in_specs=[row(d), _full((1, d)), _full(w.shape), _full(wuq.shape), _full((256, 256)), _full((8, 384)),
                  tabq, tabq, tabk, tabk],
        out_specs=(row(2048), row(288), _t_spec(288, tn, nper)),
        compiler_params=_cparams("parallel"),
    )(x, g, w, wuq, gm, consts, jnp.asarray(cosq), jnp.asarray(sinq), jnp.asarray(cosk), jnp.asarray(sink))


def _mla_expand_kernel(lat_ref, wk_ref, wv_ref, gm_ref, c_ref, place_ref, k_ref, v_ref):
    ckv = lat_ref[:, 0:MLA_KV_LORA].astype(BF16)
    kpe = _dot(lat_ref[:, MLA_KV_LORA:MLA_KV_LORA + MLA_ROPE].astype(BF16), place_ref[...])
    gm = gm_ref[...]
    kg = c_ref[0:1, :]
    for c in range(8):
        e = _dot(ckv, wk_ref[:, c * 256:(c + 1) * 256])
        k_ref[:, c * 256:(c + 1) * 256] = (_group_norm(e, gm, kg) + kpe).astype(BF16)
    v_ref[...] = _dot(ckv, wv_ref[...]).astype(BF16)


def _mla_split_ukv(w_ukv):
    w = w_ukv.reshape(MLA_KV_LORA, 16, MLA_NOPE + MLA_V)
    return w[:, :, :MLA_NOPE].reshape(MLA_KV_LORA, 16 * MLA_NOPE), w[:, :, MLA_NOPE:].reshape(MLA_KV_LORA, 16 * MLA_V)


def _mla_expand(lat, w_ukv, kn, tn):
    n = lat.shape[0]
    wk_nat, wv = _mla_split_ukv(w_ukv)
    idx = -np.ones((16, LANES), np.int64)
    for h in range(16):
        idx[h, :MLA_NOPE] = h * MLA_NOPE + np.arange(MLA_NOPE)
    wk = _take_cols(wk_nat, idx.reshape(-1)).astype(BF16)
    gm = jnp.asarray(_group_mean_matrix([(lo, lo + MLA_NOPE) for lo in (0, LANES)], 256), BF16)
    kgain = jnp.concatenate([kn[:MLA_NOPE], jnp.zeros((LANES - MLA_NOPE,), F32)])
    consts = jnp.zeros((8, 256), F32).at[0].set(jnp.tile(kgain, 2))
    place = np.zeros((MLA_ROPE, 256), np.float32)
    for lo in (MLA_NOPE, LANES + MLA_NOPE):
        place[np.arange(MLA_ROPE), lo + np.arange(MLA_ROPE)] = 1.0
    row = lambda wd: pl.BlockSpec((tn, wd), lambda i: (i, 0))
    return pl.pallas_call(
        _mla_expand_kernel,
        out_shape=(jax.ShapeDtypeStruct((n, 2048), BF16), jax.ShapeDtypeStruct((n, 1024), BF16)),
        grid=(n // tn,),
        in_specs=[row(288), _full(wk.shape), _full((256, 1024)), _full((256, 256)), _full((8, 256)),
                  _full((MLA_ROPE, 256))],
        out_specs=(row(2048), row(1024)),
        compiler_params=_cparams("parallel"),
    )(lat, wk, wv.astype(BF16), gm, consts, jnp.asarray(place, BF16))


def _mla_prompt_kernel(q_ref, k_ref, v_ref, o_ref, m_ref, acc_ref):
    i = pl.program_id(2)
    tq = q_ref.shape[0]
    tk = tq
    q0 = q_ref[:, 0:128]
    q1 = q_ref[:, 128:256]

    def scores(off):
        return jnp.concatenate([_dot_nt(q0, k_ref[pl.ds(off, tk), 0:128]),
                                _dot_nt(q1, k_ref[pl.ds(off, tk), 128:256])], axis=0)

    offd = pl.multiple_of(i * tk, tk)
    qi = lax.broadcasted_iota(jnp.int32, (2 * tq, tk), 0) & (tq - 1)
    ki = lax.broadcasted_iota(jnp.int32, (2 * tq, tk), 1)
    _flash_first(jnp.where(ki <= qi, scores(offd), NEG),
                 lambda p: _dot(p, _with_ones(v_ref[pl.ds(offd, tk), :])), m_ref, None, acc_ref)

    def body(j, carry):
        off = pl.multiple_of(j * tk, tk)
        _flash_next(scores(off), lambda p: _dot(p, _with_ones(v_ref[pl.ds(off, tk), :])), m_ref, None, acc_ref)
        return carry

    lax.fori_loop(0, i, body, 0)
    o = _finish(acc_ref)
    lane = lax.broadcasted_iota(jnp.int32, (tq, LANES), 1)
    o_ref[...] = jnp.where(lane < MLA_V, o[0:tq, :], o[tq:2 * tq, :]).astype(o_ref.dtype)


def _mla_prompt_attn(q, k, v, batch, seq):
    tq = 512
    nq = seq // tq
    return pl.pallas_call(
        _mla_prompt_kernel,
        out_shape=jax.ShapeDtypeStruct((batch * seq, 1024), BF16),
        grid=(batch, 8, nq),
        in_specs=[pl.BlockSpec((tq, 256), lambda b, p, i: (b * nq + i, p)),
                  pl.BlockSpec((seq, 256), lambda b, p, i: (b, p)),
                  pl.BlockSpec((seq, 128), lambda b, p, i: (b, p))],
        out_specs=pl.BlockSpec((tq, 128), lambda b, p, i: (b * nq + i, p)),
        scratch_shapes=[pltpu.VMEM((2 * tq, 128), F32), pltpu.VMEM((2 * tq, 256), F32)],
        compiler_params=_cparams("parallel", "parallel", "arbitrary"),
    )(q, k, v)


def _mla_sample_kernel(pt_ref, q_ref, latn_ref, wk_ref, wkt_ref, wv_ref, kg_ref, *refs, pps):
    pages = refs[:pps]
    o_ref = refs[pps]
    qt_ref, qpe_ref, m_ref, l_ref, acc_ref = refs[pps + 1:]
    s = pl.program_id(1)
    lane8 = lax.broadcasted_iota(jnp.int32, (8, LANES), 1)

    @pl.when(s == 0)
    def _():
        z = jnp.zeros((8, LANES), F32)
        qg_rows, pe_rows = [], []
        for h in range(16):
            ch = q_ref[:, h * LANES:(h + 1) * LANES]
            nope = jnp.where(lane8 < MLA_NOPE, ch, 0.0)
            if h % 2:
                nope = pltpu.roll(nope, MLA_NOPE, 1)
            qg_rows.append(jnp.concatenate([z] * (h // 2) + [nope] + [z] * (7 - h // 2), axis=1))
            pe_rows.append(pltpu.roll(ch, LANES - MLA_NOPE, 1)[:, 0:MLA_ROPE])
        qg = (jnp.concatenate(qg_rows, axis=0) * kg_ref[...]).astype(BF16)
        qt_ref[...] = _dot_nt(qg, wk_ref[...]).astype(BF16)
        qpe_ref[...] = jnp.concatenate(pe_rows, axis=0).astype(BF16)
        _flash_init(m_ref, l_ref, acc_ref)

    def update(lat_t, ok):
        ckv = lat_t[0:MLA_KV_LORA, :].astype(BF16)
        kpe = lat_t[MLA_KV_LORA:MLA_KV_LORA + MLA_ROPE, :].astype(BF16)
        n = lat_t.shape[1]
        e = _dot(wkt_ref[...], ckv)
        ms = jnp.sum((e * e).reshape(16, MLA_NOPE, n), axis=1) * (1.0 / MLA_NOPE)
        rs = lax.rsqrt(ms + NORM_EPS)
        rs = jnp.broadcast_to(rs[:, None, :], (16, 8, n)).reshape(LANES, n)
        sc = _dot(qt_ref[...], ckv) * rs + _dot(qpe_ref[...], kpe)
        pv = lambda p: _dot_nt(p, ckv)
        if ok is None:
            _flash_next(sc, pv, m_ref, l_ref, acc_ref)
        else:
            _flash_masked(sc, ok, pv, m_ref, l_ref, acc_ref)

    update(jnp.concatenate([pg[...] for pg in pages], axis=1), None)

    @pl.when(s == pl.num_programs(1) - 1)
    def _():
        rho = lax.broadcasted_iota(jnp.int32, (LANES, LANES), 0)
        t = lax.broadcasted_iota(jnp.int32, (LANES, LANES), 1)
        update(latn_ref[...], t <= (rho & 7))
        olat = (acc_ref[...] / _rep(l_ref[...], 2)).astype(BF16)
        ofull = _dot(olat, wv_ref[...])
        r_head = lax.broadcasted_iota(jnp.int32, (LANES, 1024), 0) >> 3
        c_head = lax.broadcasted_iota(jnp.int32, (LANES, 1024), 1) >> 6
        ofull = jnp.where(r_head == c_head, ofull, 0.0)
        out = ofull[0:8, :]
        for h in range(1, 16):
            out = out + ofull[h * 8:(h + 1) * 8, :]
        o_ref[...] = out


def _mla_sample_attn(q, latn_t, cache_t, page_table, w_ukv, kn, pps):
    db, n_pages = page_table.shape
    nsteps = n_pages // pps
    wk, wv = _mla_split_ukv(w_ukv)
    wk = wk.astype(BF16)
    kg = jnp.tile(kn[:MLA_NOPE], 16).reshape(1, 1024)

    def page_spec(t):
        return pl.BlockSpec((None, 288, PAGE), lambda b, s, pt: (pt[b, s * pps + t], 0, 0))

    grid_spec = pltpu.PrefetchScalarGridSpec(
        num_scalar_prefetch=1,
        grid=(db, nsteps),
        in_specs=[pl.BlockSpec((8, 2048), lambda b, s, pt: (b, 0)),
                  pl.BlockSpec((None, 288, LANES), lambda b, s, pt: (b, 0, 0)),
                  _full((256, 1024)), _full((1024, 256)), _full((256, 1024)), _full((1, 1024))]
                 + [page_spec(t) for t in range(pps)],
        out_specs=pl.BlockSpec((8, 1024), lambda b, s, pt: (b, 0)),
        scratch_shapes=[pltpu.VMEM((128, 256), BF16), pltpu.VMEM((128, MLA_ROPE), BF16),
                        pltpu.VMEM((128, 128), F32), pltpu.VMEM((128, 128), F32), pltpu.VMEM((128, 256), F32)],
    )
    return pl.pallas_call(
        functools.partial(_mla_sample_kernel, pps=pps),
        out_shape=jax.ShapeDtypeStruct((db * 8, 1024), F32),
        grid_spec=grid_spec,
        compiler_params=_cparams("parallel", "arbitrary"),
    )(page_table, q, latn_t, wk, wk.T, wv.astype(BF16), kg, *([cache_t] * pps))


def _mla_layer(xp, xs, cache, page_table, ln_g, w_dqkv, g_q, g_kv, w_uq, w_ukv, qn, kn, w_o,
               batch, seq, past_len):
    g = ln_g.reshape(1, -1)
    w_o_b = w_o.astype(BF16)
    dec = 8
    q, lat, latt = _mla_proj(xp, g, w_dqkv, g_q, g_kv, w_uq, qn, kn, np.arange(seq), 512, BF16)
    k, v = _mla_expand(lat, w_ukv, kn, 512)
    o = _mla_prompt_attn(q, k, v, batch, seq)
    ns = xs.shape[0]
    pos_s = np.tile(past_len + np.arange(dec), ns // dec)
    qs, lats, _ = _mla_proj(xs, g, w_dqkv, g_q, g_kv, w_uq, qn, kn, pos_s, ns, F32)
    os_ = _mla_sample_attn(qs, _rows_t(lats, page_table.shape[0]), _pages_t(cache), page_table, w_ukv, kn,
                           min(16, page_table.shape[1]))
    return o, os_, w_o_b, latt, lats


def kernel(x_prompt, x_sample, cache_kv_0, cache_kv_1, state_win_1, cache_lat_2, cache_kv_3, page_table, ln1_g, ln2_g, mlp_w1, mlp_w2, moba_w_in_0, moba_qn_0, moba_kn_0, moba_w_o_0, nsa_w_in_1, nsa_qn_1, nsa_kn_1, nsa_cmp_pe_1, nsa_cmp_w1_1, nsa_cmp_b1_1, nsa_cmp_w2_1, nsa_w_o_1, mla_w_dqkv_2, mla_g_q_2, mla_g_kv_2, mla_w_uq_2, mla_w_ukv_2, mla_qn_2, mla_kn_2, mla_w_o_2, moba_w_in_3, moba_qn_3, moba_kn_3, moba_w_o_3):
    batch, seq, d = x_prompt.shape
    db, dec, _ = x_sample.shape
    past = page_table.shape[1] * PAGE
    assert dec == 8 and seq % 512 == 0
    xp = x_prompt.reshape(batch * seq, d)
    xs = x_sample.reshape(db * dec, d)

    def finish_layer(i, xp, xs, op, os_, wo):
        w1 = mlp_w1[i].astype(BF16)
        w2 = mlp_w2[i].astype(BF16)
        g = ln2_g[i].reshape(1, d)
        return (_mixer_out_mlp(xp, op, wo, g, w1, w2, 512), _mixer_out_mlp(xs, os_, wo, g, w1, w2, db * dec))

    op, os_, wo, kv0_p, kv0_s = _moba_layer(xp, xs, cache_kv_0, page_table, ln1_g[0], moba_w_in_0, moba_qn_0,
                                            moba_kn_0, moba_w_o_0, batch, seq, past)
    xp, xs = finish_layer(0, xp, xs, op, os_, wo)
    op, os_, wo, kv1_p, kv1_s = _nsa_layer(xp, xs, cache_kv_1, state_win_1, page_table, ln1_g[1], nsa_w_in_1,
                                           nsa_qn_1, nsa_kn_1, nsa_cmp_pe_1, nsa_cmp_w1_1, nsa_cmp_b1_1,
                                           nsa_cmp_w2_1, nsa_w_o_1, batch, seq, past)
    xp, xs = finish_layer(1, xp, xs, op, os_, wo)
    op, os_, wo, lat_p, lat_s = _mla_layer(xp, xs, cache_lat_2, page_table, ln1_g[2], mla_w_dqkv_2, mla_g_q_2,
                                           mla_g_kv_2, mla_w_uq_2, mla_w_ukv_2, mla_qn_2, mla_kn_2, mla_w_o_2,
                                           batch, seq, past)
    xp, xs = finish_layer(2, xp, xs, op, os_, wo)
    op, os_, wo, kv3_p, kv3_s = _moba_layer(xp, xs, cache_kv_3, page_table, ln1_g[3], moba_w_in_3, moba_qn_3,
                                            moba_kn_3, moba_w_o_3, batch, seq, past)
    xp, xs = finish_layer(3, xp, xs, op, os_, wo)

    def rows_major(x_t, *feat):
        return jnp.moveaxis(x_t.reshape((batch,) + feat + (x_t.shape[-1],)), -1, 1)

    wb_p = min(NSA_WINDOW, seq)
    win_new = kv1_s[:, 512:768].reshape(db, dec, 2, NSA_KV_HEADS, HEAD_DIM)
    win_s = jnp.concatenate([state_win_1, win_new], axis=1)[:, dec:]
    return (xp.reshape(batch, seq, d), xs.reshape(db, dec, d),
            rows_major(kv0_p, 2, MOBA_KV_HEADS, HEAD_DIM),
            rows_major(kv1_p[:, :512], 4, NSA_KV_HEADS, HEAD_DIM),
            rows_major(kv1_p[:, 512:768, seq - wb_p:], 2, NSA_KV_HEADS, HEAD_DIM),
            rows_major(lat_p, MLA_KV_LORA + MLA_ROPE),
            rows_major(kv3_p, 2, MOBA_KV_HEADS, HEAD_DIM),
            kv0_s.reshape(db, dec, 2, MOBA_KV_HEADS, HEAD_DIM),
            kv1_s[:, :512].reshape(db, dec, 4, NSA_KV_HEADS, HEAD_DIM),
            win_s,
            lat_s.reshape(db, dec, MLA_KV_LORA + MLA_ROPE),
            kv3_s.reshape(db, dec, 2, MOBA_KV_HEADS, HEAD_DIM))
```
